```python
import jax, jax.numpy as jnp
from jax import lax
import numpy as np

D_MODEL = 1024
BATCH = 4
SEQ = 4096
DEPTH = 2

CHUNK = 64
Q_BLOCK = 128
HEAD_DIM = 64
SB_WIDTH = 3 * D_MODEL // 8
SB_HEADS = SB_WIDTH // HEAD_DIM
RW_WIDTH = 3 * D_MODEL // 8
RW_HEADS = RW_WIDTH // HEAD_DIM
RW_DECAY_RANK = 32
RW_A_RANK = 32
RW_GATE_RANK = 64
RW_GN_EPS = 64e-5
RW_SPLIT = (RW_WIDTH, RW_WIDTH, RW_WIDTH, RW_DECAY_RANK, RW_A_RANK, RW_GATE_RANK)
RW_IN_WIDTH = sum(RW_SPLIT)
GLA_VAL_WIDTH = D_MODEL // 4
GLA_KEY_WIDTH = GLA_VAL_WIDTH // 2
GLA_DV = 64
GLA_HEADS = GLA_VAL_WIDTH // GLA_DV
GLA_DK = GLA_KEY_WIDTH // GLA_HEADS
GLA_GATE_RANK = 16
GLA_GATE_NORMALIZER = 16.0
GLA_SPLIT = (GLA_KEY_WIDTH, GLA_KEY_WIDTH, GLA_VAL_WIDTH, GLA_GATE_RANK, GLA_VAL_WIDTH)
GLA_IN_WIDTH = sum(GLA_SPLIT)

MIX_WIDTH = SB_WIDTH + RW_WIDTH + GLA_VAL_WIDTH
IN_WIDTH = 3 * SB_WIDTH + RW_IN_WIDTH + GLA_IN_WIDTH
D_FF = 4 * D_MODEL
EPS = 1e-6

kernel_name = 'hybrid_sb_rwkv7_gla_block'


def split_sizes(t, sizes):
    idx = [int(i) for i in np.cumsum(sizes)[:-1]]
    return jnp.split(t, idx, axis=-1)


def rmsnorm(x, gain):
    xf = x.astype(jnp.float32)
    y = xf * lax.rsqrt(jnp.mean(xf * xf, axis=-1, keepdims=True) + EPS)
    return (y * gain.astype(jnp.float32)).astype(x.dtype)


def head_rmsnorm(o, gain, n_heads):
    B, T, C = o.shape
    oh = o.reshape(B, T, n_heads, C // n_heads)
    return rmsnorm(oh, gain.reshape(n_heads, C // n_heads)).reshape(B, T, C)


def token_shift(p, mu):
    prev = jnp.pad(p, ((0, 0), (1, 0), (0, 0)))[:, :-1]
    return p + (prev - p) * mu


def stick_breaking_attention(q, k, v):
    B, T, C = q.shape
    H = SB_HEADS
    Dh = C // H
    n_blk = T // Q_BLOCK
    qh = q.reshape(B, n_blk, Q_BLOCK, H, Dh).transpose(1, 0, 2, 3, 4)
    kh = k.reshape(B, T, H, Dh)
    vh = v.reshape(B, T, H, Dh)
    key_pos = jnp.arange(T)
    scale = Dh ** -0.5

    def block(args):
        q_blk, blk = args
        z = jnp.einsum('bqhd,bkhd->bhqk', q_blk, kh).astype(jnp.float32) * scale
        q_pos = blk * Q_BLOCK + jnp.arange(Q_BLOCK)
        mask = key_pos[None, :] < q_pos[:, None]
        log_1m_beta = jnp.where(mask, jax.nn.log_sigmoid(-z), 0.0)
        between = lax.cumsum(log_1m_beta, axis=3, reverse=True) - log_1m_beta
        weights = jnp.where(mask, jnp.exp(jax.nn.log_sigmoid(z) + between), 0.0)
        return jnp.einsum('bhqk,bkhd->bqhd', weights.astype(vh.dtype), vh)

    out = lax.map(block, (qh, jnp.arange(n_blk)))
    return out.transpose(1, 0, 2, 3, 4).reshape(B, T, C)


def rwkv7_time_mix(rw_in, mu, w0, w_up, a0, a_up, g_up, k_k, k_a, r_k, gn_g, gn_b):
    B, T, _ = rw_in.shape
    H, N = RW_HEADS, HEAD_DIM
    f32 = jnp.float32
    r, k, v, wd, ad, gd = split_sizes(token_shift(rw_in, mu), RW_SPLIT)
    log_w = -jax.nn.softplus(-(w0 + jnp.tanh(wd) @ w_up).astype(f32)) - 0.5
    decay = jnp.exp(-jnp.exp(log_w))
    a = jax.nn.sigmoid((a0 + ad @ a_up).astype(f32))
    g = jax.nn.sigmoid(gd) @ g_up

    def heads(t):
        return t.astype(f32).reshape(B, T, H, N)

    kk = heads(k * k_k)
    kk = kk * lax.rsqrt(jnp.maximum(jnp.sum(kk * kk, axis=-1, keepdims=True), 1e-12))
    k_mod = k.astype(f32) * (1.0 + (a - 1.0) * k_a)
    rh, kh, vh, wh, ah = heads(r), heads(k_mod), heads(v), heads(decay), heads(a)
    a_vec = -kk
    b_vec = kk * ah

    def step(S, inp):
        r_t, w_t, k_t, v_t, a_t, b_t = inp
        sa = jnp.einsum('bhvk,bhk->bhv', S, a_t)
        S = S * w_t[:, :, None, :] + sa[..., None] * b_t[:, :, None, :] + v_t[..., None] * k_t[:, :, None, :]
        return S, jnp.einsum('bhvk,bhk->bhv', S, r_t)

    xs = tuple(t.transpose(1, 0, 2, 3) for t in (rh, wh, kh, vh, a_vec, b_vec))
    S0 = jnp.zeros((B, H, N, N), f32)
    _, y = lax.scan(step, S0, xs)
    y = y.transpose(1, 0, 2, 3)
    mean = jnp.mean(y, axis=-1, keepdims=True)
    var = jnp.mean(jnp.square(y - mean), axis=-1, keepdims=True)
    yn = (y - mean) * lax.rsqrt(var + RW_GN_EPS) * gn_g.astype(f32).reshape(H, N) + gn_b.astype(f32).reshape(H, N)
    bonus = jnp.sum(rh * kh * r_k.astype(f32).reshape(H, N), axis=-1, keepdims=True) * vh
    out = (yn + bonus).reshape(B, T, H * N) * g.astype(f32)
    return out.astype(rw_in.dtype)


def gla_linear_attention(q, k, v, gd, g, gate_up, gate_b, norm_g):
    B, T, _ = q.shape
    H, DK, DV, C = GLA_HEADS, GLA_DK, GLA_DV, CHUNK
    nc = T // C
    f32 = jnp.float32
    log_alpha = jax.nn.log_sigmoid((gd @ gate_up + gate_b).astype(f32)) / GLA_GATE_NORMALIZER

    def chunks(t, d):
        return t.astype(f32).reshape(B, nc, C, H, d).transpose(1, 0, 3, 2, 4)

    qc = chunks(q, DK) * (DK ** -0.5)
    kc, vc, lc = chunks(k, DK), chunks(v, DV), chunks(log_alpha, DK)
    causal = jnp.tril(jnp.ones((C, C), dtype=bool))

    def step(S, inp):
        q_c, k_c, v_c, la_c = inp
        b = jnp.cumsum(la_c, axis=2)
        o_inter = jnp.einsum('bhtk,bhkv->bhtv', q_c * jnp.exp(b), S)
        diff = b[:, :, :, None, :] - b[:, :, None, :, :]
        decay = jnp.exp(jnp.where(causal[:, :, None], diff, -jnp.inf))
        scores = jnp.einsum('bhtk,bhsk,bhtsk->bhts', q_c, k_c, decay)
        o_intra = jnp.einsum('bhts,bhsv->bhtv', scores, v_c)
        b_last = b[:, :, -1:, :]
        S = S * jnp.exp(b_last)[:, :, 0, :, None] + jnp.einsum('bhsk,bhsv->bhkv', k_c * jnp.exp(b_last - b), v_c)
        return S, o_inter + o_intra

    S0 = jnp.zeros((B, H, DK, DV), f32)
    _, o = lax.scan(step, S0, (qc, kc, vc, lc))
    o = o.transpose(1, 0, 3, 2, 4).reshape(B, T, H * DV)
    o = head_rmsnorm(o, norm_g, H) * jax.nn.silu(g.astype(f32))
    return o.astype(q.dtype)


def setup_inputs(seed: int = 0) -> dict:
    key = jax.random.key(seed)
    ks = jax.random.split(key, 26)
    f32 = jnp.float32
    L = DEPTH

    def nrm(k, shape, scale):
        return jax.random.normal(k, shape, f32) * scale

    def gain(k, shape):
        return 1.0 + 0.05 * jax.random.normal(k, shape, f32)

    return {
        'x': nrm(ks[0], (BATCH, SEQ, D_MODEL), 1.0),
        'pre_mix_g': gain(ks[1], (L, D_MODEL)),
        'w_in': nrm(ks[2], (L, D_MODEL, IN_WIDTH), D_MODEL ** -0.5),
        'sb_norm_g': gain(ks[3], (L, SB_WIDTH)),
        'rw_mu': jax.random.uniform(ks[4], (L, RW_IN_WIDTH), f32, 0.2, 0.8),
        'rw_w0': jax.random.uniform(ks[5], (L, RW_WIDTH), f32, -5.0, 1.0),
        'rw_w_up': nrm(ks[6], (L, RW_DECAY_RANK, RW_WIDTH), 0.5 * RW_DECAY_RANK ** -0.5),
        'rw_a0': nrm(ks[7], (L, RW_WIDTH), 0.1),
        'rw_a_up': nrm(ks[8], (L, RW_A_RANK, RW_WIDTH), 0.5 * RW_A_RANK ** -0.5),
        'rw_g_up': nrm(ks[9], (L, RW_GATE_RANK, RW_WIDTH), RW_GATE_RANK ** -0.5),
        'rw_k_k': 0.85 + 0.05 * jax.random.normal(ks[10], (L, RW_WIDTH), f32),
        'rw_k_a': gain(ks[11], (L, RW_WIDTH)),
        'rw_r_k': nrm(ks[12], (L, RW_WIDTH), 0.1),
        'rw_gn_g': gain(ks[13], (L, RW_WIDTH)),
        'rw_gn_b': nrm(ks[14], (L, RW_WIDTH), 0.01),
        'gla_gate_up': nrm(ks[15], (L, GLA_GATE_RANK, GLA_KEY_WIDTH), GLA_GATE_RANK ** -0.5),
        'gla_gate_b': nrm(ks[16], (L, GLA_KEY_WIDTH), 0.5),
        'gla_norm_g': gain(ks[17], (L, GLA_VAL_WIDTH)),
        'w_out': nrm(ks[18], (L, MIX_WIDTH, D_MODEL), MIX_WIDTH ** -0.5),
        'post_mix_g': gain(ks[19], (L, D_MODEL)),
        'pre_ffn_g': gain(ks[20], (L, D_MODEL)),
        'w_ff1': nrm(ks[21], (L, D_MODEL, D_FF), D_MODEL ** -0.5),
        'w_ff2': nrm(ks[22], (L, D_FF, D_MODEL), D_FF ** -0.5),
        'post_ffn_g': gain(ks[23], (L, D_MODEL)),
    }


def reference(x, pre_mix_g, w_in, sb_norm_g, rw_mu, rw_w0, rw_w_up, rw_a0, rw_a_up, rw_g_up,
              rw_k_k, rw_k_a, rw_r_k, rw_gn_g, rw_gn_b, gla_gate_up, gla_gate_b, gla_norm_g,
              w_out, post_mix_g, pre_ffn_g, w_ff1, w_ff2, post_ffn_g):
    for l in range(DEPTH):
        h = rmsnorm(x, pre_mix_g[l])
        proj = h @ w_in[l]
        sb_in, rw_in, gla_in = split_sizes(proj, (3 * SB_WIDTH, RW_IN_WIDTH, GLA_IN_WIDTH))
        sb_q, sb_k, sb_v = split_sizes(sb_in, (SB_WIDTH, SB_WIDTH, SB_WIDTH))
        o_sb = head_rmsnorm(stick_breaking_attention(sb_q, sb_k, sb_v), sb_norm_g[l], SB_HEADS)
        o_rw = rwkv7_time_mix(rw_in, rw_mu[l], rw_w0[l], rw_w_up[l], rw_a0[l], rw_a_up[l], rw_g_up[l],
                              rw_k_k[l], rw_k_a[l], rw_r_k[l], rw_gn_g[l], rw_gn_b[l])
        g_q, g_k, g_v, g_gd, g_g = split_sizes(gla_in, GLA_SPLIT)
        o_gla = gla_linear_attention(g_q, g_k, g_v, g_gd, g_g, gla_gate_up[l], gla_gate_b[l], gla_norm_g[l])
        mixed = jnp.concatenate([o_sb, o_rw, o_gla], axis=-1) @ w_out[l]
        x = x + rmsnorm(mixed, post_mix_g[l])
        h = rmsnorm(x, pre_ffn_g[l])
        ff = jnp.square(jax.nn.relu(h @ w_ff1[l])) @ w_ff2[l]
        x = x + rmsnorm(ff, post_ffn_g[l])
    return x
```

```python
import functools

import jax
import jax.numpy as jnp
from jax import lax
from jax.experimental import pallas as pl
from jax.experimental.pallas import tpu as pltpu

F32 = jnp.float32
BF16 = jnp.bfloat16

D_MODEL = 1024
HEAD_DIM = 64
LANES = 128
SB_WIDTH = 384
RW_WIDTH = 384
RW_IN_WIDTH = 1280
RW_CODE = 128
GLA_KEY_WIDTH = 128
GLA_VAL_WIDTH = 256
GLA_DK = 32
GLA_GATE_RANK = 16
GLA_IN_WIDTH = 784
GLA_PAD_WIDTH = 896
GLA_GATE_NORMALIZER = 16.0
D_FF = 4096
EPS = 1e-6
RW_GN_EPS = 64e-5
CHUNK = 64
VMEM_LIMIT = 48 * 1024 * 1024

SB_BLOCK = 256
SB_DEAD = -110.0


def _cparams(sem):
    return pltpu.CompilerParams(dimension_semantics=sem, vmem_limit_bytes=VMEM_LIMIT)


def _dot(a, b):
    return jnp.dot(a, b, preferred_element_type=F32)


def _dot_nt(a, b):
    return lax.dot_general(a, b, (((1,), (1,)), ((), ())), preferred_element_type=F32)


def _dot_tn(a, b):
    return lax.dot_general(a, b, (((0,), (0,)), ((), ())), preferred_element_type=F32)


def _split2(x):
    hi = x.astype(BF16)
    lo = (x - hi.astype(F32)).astype(BF16)
    return hi, lo


def _split3(x):
    hi = x.astype(BF16)
    r1 = x - hi.astype(F32)
    mid = r1.astype(BF16)
    lo = (r1 - mid.astype(F32)).astype(BF16)
    return hi, mid, lo


def _dot_x3(a, w_hi, w_lo):
    a_hi, a_lo = _split2(a)
    return _dot(a_hi, w_hi) + (_dot(a_lo, w_hi) + _dot(a_hi, w_lo))


def _dot_exact_lhs(m_bf16, x):
    hi, mid, lo = _split3(x)
    return _dot(m_bf16, hi) + (_dot(m_bf16, mid) + _dot(m_bf16, lo))


def _rms(x, gain):
    return x * lax.rsqrt(jnp.mean(x * x, axis=-1, keepdims=True) + EPS) * gain


def _pair_head_sum(x, low_half):
    s_all = jnp.sum(x, axis=-1, keepdims=True)
    s_low = jnp.sum(jnp.where(low_half, x, 0.0), axis=-1, keepdims=True)
    return jnp.where(low_half, s_low, s_all - s_low)


def _inproj_kernel(x_ref, g_ref, wsb_ref, wrw_ref, wgl_ref, sb_ref, rw_ref, gl_ref):
    h = _rms(x_ref[...], g_ref[...]).astype(BF16)
    sb_ref[...] = _dot(h, wsb_ref[...]).astype(BF16)
    rw_ref[...] = _dot(h, wrw_ref[...])
    gl_ref[...] = _dot(h, wgl_ref[...])


def _inproj(x2, gain, w_sb, w_rw, w_gl, tm=512):
    n = x2.shape[0]
    full = lambda w: pl.BlockSpec(w.shape, lambda i: (0, 0))
    row = lambda width: pl.BlockSpec((tm, width), lambda i: (i, 0))
    return pl.pallas_call(
        _inproj_kernel,
        grid=(n // tm,),
        in_specs=[row(D_MODEL), full(gain), full(w_sb), full(w_rw), full(w_gl)],
        out_specs=[row(3 * SB_WIDTH), row(RW_IN_WIDTH), row(GLA_PAD_WIDTH)],
        out_shape=[jax.ShapeDtypeStruct((n, 3 * SB_WIDTH), BF16),
                   jax.ShapeDtypeStruct((n, RW_IN_WIDTH), F32),
                   jax.ShapeDtypeStruct((n, GLA_PAD_WIDTH), F32)],
        compiler_params=_cparams(("parallel",)),
        name="in_proj",
    )(x2, gain, w_sb, w_rw, w_gl)


def _sb_kernel(q_ref, k_ref, v_ref, g_ref, o_ref):
    qb = SB_BLOCK
    qi = pl.program_id(2)
    lane = lax.broadcasted_iota(jnp.int32, (1, LANES), 1)
    low_half = lane < HEAD_DIM
    row = lax.broadcasted_iota(jnp.int32, (qb, qb), 0)
    col = lax.broadcasted_iota(jnp.int32, (qb, qb), 1)
    causal = col < row
    suffix = (row > col).astype(BF16)
    q = q_ref[...]

    def block(j, qh, vmask, carry, acc, masked):
        start = pl.multiple_of(j * qb, qb)
        kb = k_ref[pl.ds(start, qb), :]
        vb = jnp.where(vmask, v_ref[pl.ds(start, qb), :], jnp.zeros((), BF16))
        z = _dot_nt(qh, kb)
        t = jnp.log(1.0 + jnp.exp(-jnp.abs(z)))
        lg = -(jnp.maximum(z, 0.0) + t)
        if masked:
            lg = jnp.where(causal, lg, 0.0)
        hi, lo = _split2(lg)
        between = _dot(hi, suffix) + _dot(lo, suffix)
        total = between[:, 0:1] + lg[:, 0:1]
        w = jnp.exp(z + lg + between + carry)
        if masked:
            w = jnp.where(causal, w, 0.0)
        return carry + total, acc + _dot(w.astype(BF16), vb)

    acc = jnp.zeros((qb, LANES), F32)
    for h in range(2):
        vmask = low_half if h == 0 else jnp.logical_not(low_half)
        qh = jnp.where(vmask, q, jnp.zeros((), BF16)) * jnp.asarray(HEAD_DIM ** -0.5, BF16)
        carry, acc = block(qi, qh, vmask, jnp.zeros((qb, 1), F32), acc, True)

        def cond(s):
            j, carry, _ = s
            return jnp.logical_and(j >= 0, jnp.max(carry) > SB_DEAD)

        def body(s, qh=qh, vmask=vmask):
            j, carry, acc = s
            carry, acc = block(j, qh, vmask, carry, acc, False)
            return j - 1, carry, acc

        _, _, acc = lax.while_loop(cond, body, (qi - 1, carry, acc))

    ms = _pair_head_sum(acc * acc, low_half) * (1.0 / HEAD_DIM)
    o_ref[...] = (acc * lax.rsqrt(ms + EPS) * g_ref[...]).astype(o_ref.dtype)


def _sb_attention(sb, gain, batch, seq):
    nq = seq // SB_BLOCK
    n = sb.shape[0]
    pairs = SB_WIDTH // LANES
    return pl.pallas_call(
        _sb_kernel,
        grid=(batch, pairs, nq),
        in_specs=[
            pl.BlockSpec((SB_BLOCK, LANES), lambda b, p, i: (b * nq + i, p)),
            pl.BlockSpec((seq, LANES), lambda b, p, i: (b, pairs + p)),
            pl.BlockSpec((seq, LANES), lambda b, p, i: (b, 2 * pairs + p)),
            pl.BlockSpec((1, LANES), lambda b, p, i: (0, p)),
        ],
        out_specs=pl.BlockSpec((SB_BLOCK, LANES), lambda b, p, i: (b * nq + i, p)),
        out_shape=jax.ShapeDtypeStruct((n, SB_WIDTH), BF16),
        compiler_params=_cparams(("parallel", "parallel", "arbitrary")),
        name="sb_attention",
    )(sb, sb, sb, gain)


RW_BLOCK = 256


def _rw_kernel(x_ref, mu_ref, w0_ref, a0_ref, kk_ref, ka_ref, rk_ref, gng_ref, gnb_ref,
               uphi_ref, uplo_ref, o_ref,
               state_s, prev_s, r_s, k_s, v_s, lw_s, a_s, b_s, y_s):
    tb = RW_BLOCK
    c = CHUNK
    pairs = RW_WIDTH // LANES

    @pl.when(pl.program_id(1) == 0)
    def _():
        state_s[...] = jnp.zeros_like(state_s)
        prev_s[...] = jnp.zeros_like(prev_s)

    lane = lax.broadcasted_iota(jnp.int32, (1, LANES), 1)
    low_half = lane < HEAD_DIM

    x = x_ref[...]
    first = lax.broadcasted_iota(jnp.int32, (tb, 1), 0) == 0
    prev = jnp.where(first, prev_s[...], pltpu.roll(x, 1, 0))
    prev_s[...] = x[tb - 1:tb, :]
    xs = x + (prev - x) * mu_ref[...]

    r = xs[:, 0:RW_WIDTH]
    k = xs[:, RW_WIDTH:2 * RW_WIDTH]
    v = xs[:, 2 * RW_WIDTH:3 * RW_WIDTH]
    code = xs[:, 3 * RW_WIDTH:]
    act = jnp.where(lane < 32, jnp.tanh(code), jnp.where(lane < 64, code, jax.nn.sigmoid(code)))
    up = _dot_x3(act, uphi_ref[...], uplo_ref[...])
    wpre = w0_ref[...] + up[:, 0:RW_WIDTH]
    log_w = -(jnp.maximum(-wpre, 0.0) + jnp.log(1.0 + jnp.exp(-jnp.abs(wpre)))) - 0.5
    lw = -jnp.exp(log_w)
    a = jax.nn.sigmoid(a0_ref[...] + up[:, RW_WIDTH:2 * RW_WIDTH])
    gate = up[:, 2 * RW_WIDTH:]
    kmod = k * (1.0 + (a - 1.0) * ka_ref[...])
    kk = k * kk_ref[...]
    bonus_in = r * kmod * rk_ref[...]
    kkn, bon = [], []
    for p in range(pairs):
        sl = slice(p * LANES, (p + 1) * LANES)
        ss = _pair_head_sum(kk[:, sl] * kk[:, sl], low_half)
        kkn.append(kk[:, sl] * lax.rsqrt(jnp.maximum(ss, 1e-12)))
        bon.append(_pair_head_sum(bonus_in[:, sl], low_half))
    kkn = jnp.concatenate(kkn, axis=1)
    bonus = jnp.concatenate(bon, axis=1) * v
    r_s[...] = r
    k_s[...] = kmod
    v_s[...] = v
    lw_s[...] = lw
    a_s[...] = -kkn
    b_s[...] = kkn * a

    rr = lax.broadcasted_iota(jnp.int32, (2 * c, 2 * c), 0)
    cc = lax.broadcasted_iota(jnp.int32, (2 * c, 2 * c), 1)
    same_head = (rr // c) == (cc // c)
    strict = jnp.logical_and(same_head, cc < rr)
    incl = jnp.logical_and(same_head, cc <= rr)
    tr = lax.broadcasted_iota(jnp.int32, (c, c), 0)
    tc = lax.broadcasted_iota(jnp.int32, (c, c), 1)
    tri = (tc <= tr).astype(BF16)
    m_lo = low_half
    m_hi = jnp.logical_not(low_half)

    def stack(xp):
        return jnp.concatenate([jnp.where(m_lo, xp, 0.0), jnp.where(m_hi, xp, 0.0)], axis=0)

    def chunk_body(ci, carry):
        rows = pl.ds(pl.multiple_of(ci * c, c), c)
        for p in range(pairs):
            sl = slice(p * LANES, (p + 1) * LANES)
            lwc = lw_s[rows, sl]
            lc = _dot_exact_lhs(tri, lwc)
            e_incl = jnp.exp(lc)
            e_excl = jnp.exp(lc - lwc)
            e_inv = jnp.exp(-lc)
            g_end = e_incl[c - 1:c, :]
            ag = stack(a_s[rows, sl] * e_excl).astype(BF16)
            rg = stack(r_s[rows, sl] * e_incl).astype(BF16)
            bd = b_s[rows, sl] * e_inv
            kd = k_s[rows, sl] * e_inv
            vst = stack(v_s[rows, sl]).astype(BF16)
            bd2 = jnp.concatenate([bd, bd], axis=0).astype(BF16)
            kd2 = jnp.concatenate([kd, kd], axis=0).astype(BF16)
            a_ab = jnp.where(strict, _dot_nt(ag, bd2), 0.0)
            a_ak = jnp.where(strict, _dot_nt(ag, kd2), 0.0)
            a_rb = jnp.where(incl, _dot_nt(rg, bd2), 0.0)
            a_rk = jnp.where(incl, _dot_nt(rg, kd2), 0.0)
            xw = ag.astype(F32)
            xv = _dot(a_ak.astype(BF16), vst)
            npow = a_ab
            for it in range(6):
                nb = npow.astype(BF16)
                xw = xw + _dot(nb, xw.astype(BF16))
                xv = xv + _dot(nb, xv.astype(BF16))
                if it < 5:
                    npow = _dot(nb, nb)
            s0 = state_s[p]
            s0b = s0.astype(BF16)
            u = _dot_nt(xw.astype(BF16), s0b) + xv
            ub = u.astype(BF16)
            y = _dot_nt(rg, s0b) + _dot(a_rb.astype(BF16), ub) + _dot(a_rk.astype(BF16), vst)
            bde = stack(bd * g_end).astype(BF16)
            kde = stack(kd * g_end).astype(BF16)
            state_s[p] = s0 * g_end + _dot_tn(ub, bde) + _dot_tn(vst, kde)
            y_s[rows, sl] = y[0:c, :] + y[c:2 * c, :]
        return carry

    lax.fori_loop(0, tb // c, chunk_body, 0)

    outs = []
    for p in range(pairs):
        sl = slice(p * LANES, (p + 1) * LANES)
        y = y_s[:, sl]
        mean = _pair_head_sum(y, low_half) * (1.0 / HEAD_DIM)
        d = y - mean
        var = _pair_head_sum(d * d, low_half) * (1.0 / HEAD_DIM)
        outs.append(d * lax.rsqrt(var + RW_GN_EPS))
    yn = jnp.concatenate(outs, axis=1) * gng_ref[...] + gnb_ref[...]
    o_ref[...] = ((yn + bonus) * gate).astype(o_ref.dtype)


def _rwkv7(rw, prm, batch, seq):
    n = rw.shape[0]
    nb = seq // RW_BLOCK
    vec = pl.BlockSpec((1, RW_WIDTH), lambda b, i: (0, 0))
    up = pl.BlockSpec((RW_CODE, 3 * RW_WIDTH), lambda b, i: (0, 0))
    blk = lambda: pltpu.VMEM((RW_BLOCK, RW_WIDTH), F32)
    return pl.pallas_call(
        _rw_kernel,
        grid=(batch, nb),
        in_specs=[pl.BlockSpec((RW_BLOCK, RW_IN_WIDTH), lambda b, i: (b * nb + i, 0)),
                  pl.BlockSpec((1, RW_IN_WIDTH), lambda b, i: (0, 0)),
                  vec, vec, vec, vec, vec, vec, vec, up, up],
        out_specs=pl.BlockSpec((RW_BLOCK, RW_WIDTH), lambda b, i: (b * nb + i, 0)),
        out_shape=jax.ShapeDtypeStruct((n, RW_WIDTH), BF16),
        scratch_shapes=[pltpu.VMEM((RW_WIDTH // LANES, LANES, LANES), F32),
                        pltpu.VMEM((1, RW_IN_WIDTH), F32),
                        blk(), blk(), blk(), blk(), blk(), blk(), blk()],
        compiler_params=_cparams(("parallel", "arbitrary")),
        name="rwkv7",
    )(rw, prm["mu"], prm["w0"], prm["a0"], prm["k_k"], prm["k_a"], prm["r_k"],
      prm["gn_g"], prm["gn_b"], prm["up_hi"], prm["up_lo"])


GLA_BLOCK = 256


def _gla_kernel(x_ref, gb_ref, ng_ref, uphi_ref, uplo_ref, o_ref, state_s, q_s, k_s, la_s, o_s):
    tb = GLA_BLOCK
    c = CHUNK
    heads = GLA_VAL_WIDTH // HEAD_DIM

    @pl.when(pl.program_id(1) == 0)
    def _():
        state_s[...] = jnp.zeros_like(state_s)

    kw, vw = GLA_KEY_WIDTH, GLA_VAL_WIDTH
    code = x_ref[:, 2 * kw + 2 * vw:]
    pre = _dot_x3(code, uphi_ref[...], uplo_ref[...]) + gb_ref[...]
    la_s[...] = -(jnp.maximum(-pre, 0.0) + jnp.log(1.0 + jnp.exp(-jnp.abs(pre)))) * (1.0 / GLA_GATE_NORMALIZER)
    q_s[...] = x_ref[:, 0:kw] * (GLA_DK ** -0.5)
    k_s[...] = x_ref[:, kw:2 * kw]

    tr = lax.broadcasted_iota(jnp.int32, (c, c), 0)
    tc = lax.broadcasted_iota(jnp.int32, (c, c), 1)
    tri = (tc <= tr).astype(BF16)
    srow = lax.broadcasted_iota(jnp.int32, (heads * c, c), 0)
    scol = lax.broadcasted_iota(jnp.int32, (heads * c, c), 1)
    causal = scol <= (srow % c)
    klane = lax.broadcasted_iota(jnp.int32, (1, kw), 1) // GLA_DK
    vlane = lax.broadcasted_iota(jnp.int32, (1, vw), 1) // HEAD_DIM
    st_r = lax.broadcasted_iota(jnp.int32, (vw, kw), 0) // HEAD_DIM
    st_c = lax.broadcasted_iota(jnp.int32, (vw, kw), 1) // GLA_DK
    st_mask = st_r == st_c

    def chunk_body(ci, carry):
        rows = pl.ds(pl.multiple_of(ci * c, c), c)
        bc = _dot_exact_lhs(tri, la_s[rows, :])
        e_pos = jnp.exp(bc)
        qe = q_s[rows, :] * e_pos
        kc = k_s[rows, :]
        ke = (kc * jnp.exp(-bc)).astype(BF16)
        g_end = e_pos[c - 1:c, :]
        k2 = (kc * jnp.exp(bc[c - 1:c, :] - bc)).astype(BF16)
        vb = x_ref[rows, 2 * kw:2 * kw + vw].astype(BF16)
        qst = jnp.concatenate([jnp.where(klane == h, qe, 0.0) for h in range(heads)], axis=0).astype(BF16)
        scores = jnp.where(causal, _dot_nt(qst, ke), 0.0)
        o_st = _dot(scores.astype(BF16), vb)
        s0 = state_s[...]
        o = _dot_nt(qe.astype(BF16), s0.astype(BF16))
        for h in range(heads):
            o = o + jnp.where(vlane == h, o_st[h * c:(h + 1) * c, :], 0.0)
        state_s[...] = s0 * g_end + jnp.where(st_mask, _dot_tn(vb, k2), 0.0)
        o_s[rows, :] = o
        return carry

    lax.fori_loop(0, tb // c, chunk_body, 0)

    lane = lax.broadcasted_iota(jnp.int32, (1, LANES), 1)
    low_half = lane < HEAD_DIM
    outs = []
    for p in range(vw // LANES):
        o = o_s[:, p * LANES:(p + 1) * LANES]
        ms = _pair_head_sum(o * o, low_half) * (1.0 / HEAD_DIM)
        outs.append(o * lax.rsqrt(ms + EPS))
    g = x_ref[:, 2 * kw + vw:2 * kw + 2 * vw]
    on = jnp.concatenate(outs, axis=1) * ng_ref[...]
    o_ref[...] = (on * (g * jax.nn.sigmoid(g))).astype(o_ref.dtype)


def _gla(gl, prm, batch, seq):
    n = gl.shape[0]
    nb = seq // GLA_BLOCK
    return pl.pallas_call(
        _gla_kernel,
        grid=(batch, nb),
        in_specs=[pl.BlockSpec((GLA_BLOCK, GLA_PAD_WIDTH), lambda b, i: (b * nb + i, 0)),
                  pl.BlockSpec((1, GLA_KEY_WIDTH), lambda b, i: (0, 0)),
                  pl.BlockSpec((1, GLA_VAL_WIDTH), lambda b, i: (0, 0)),
                  pl.BlockSpec((LANES, GLA_KEY_WIDTH), lambda b, i: (0, 0)),
                  pl.BlockSpec((LANES, GLA_KEY_WIDTH), lambda b, i: (0, 0))],
        out_specs=pl.BlockSpec((GLA_BLOCK, GLA_VAL_WIDTH), lambda b, i: (b * nb + i, 0)),
        out_shape=jax.ShapeDtypeStruct((n, GLA_VAL_WIDTH), BF16),
        scratch_shapes=[pltpu.VMEM((GLA_VAL_WIDTH, GLA_KEY_WIDTH), F32),
                        pltpu.VMEM((GLA_BLOCK, GLA_KEY_WIDTH), F32),
                        pltpu.VMEM((GLA_BLOCK, GLA_KEY_WIDTH), F32),
                        pltpu.VMEM((GLA_BLOCK, GLA_KEY_WIDTH), F32),
                        pltpu.VMEM((GLA_BLOCK, GLA_VAL_WIDTH), F32)],
        compiler_params=_cparams(("parallel", "arbitrary")),
        name="gla",
    )(gl, prm["gate_b"], prm["norm_g"], prm["up_hi"], prm["up_lo"])


def _outproj_kernel(x_ref, sb_ref, rw_ref, gl_ref, wsb_ref, wrw_ref, wgl_ref, g_ref, o_ref):
    mixed = _dot(sb_ref[...], wsb_ref[...]) + _dot(rw_ref[...], wrw_ref[...]) + _dot(gl_ref[...], wgl_ref[...])
    o_ref[...] = x_ref[...] + _rms(mixed, g_ref[...])


def _outproj(x2, o_sb, o_rw, o_gl, w_sb, w_rw, w_gl, gain, tm=512):
    n = x2.shape[0]
    full = lambda w: pl.BlockSpec(w.shape, lambda i: (0, 0))
    row = lambda width: pl.BlockSpec((tm, width), lambda i: (i, 0))
    return pl.pallas_call(
        _outproj_kernel,
        grid=(n // tm,),
        in_specs=[row(D_MODEL), row(SB_WIDTH), row(RW_WIDTH), row(GLA_VAL_WIDTH),
                  full(w_sb), full(w_rw), full(w_gl), full(gain)],
        out_specs=row(D_MODEL),
        out_shape=jax.ShapeDtypeStruct((n, D_MODEL), F32),
        compiler_params=_cparams(("parallel",)),
        name="out_proj",
    )(x2, o_sb, o_rw, o_gl, w_sb, w_rw, w_gl, gain)


def _ffn_kernel(x_ref, g1_ref, w1_ref, w2_ref, g2_ref, o_ref, h_s, acc_s):
    j = pl.program_id(1)

    @pl.when(j == 0)
    def _():
        h_s[...] = _rms(x_ref[...], g1_ref[...]).astype(BF16)
        acc_s[...] = jnp.zeros_like(acc_s)

    a = jnp.maximum(_dot(h_s[...], w1_ref[...]), 0.0)
    acc_s[...] += _dot((a * a).astype(BF16), w2_ref[...])

    @pl.when(j == pl.num_programs(1) - 1)
    def _():
        o_ref[...] = x_ref[...] + _rms(acc_s[...], g2_ref[...])


def _ffn(x2, g1, w1, w2, g2, tm=1024, tf=1024):
    n = x2.shape[0]
    return pl.pallas_call(
        _ffn_kernel,
        grid=(n // tm, D_FF // tf),
        in_specs=[pl.BlockSpec((tm, D_MODEL), lambda i, j: (i, 0)),
                  pl.BlockSpec((1, D_MODEL), lambda i, j: (0, 0)),
                  pl.BlockSpec((D_MODEL, tf), lambda i, j: (0, j)),
                  pl.BlockSpec((tf, D_MODEL), lambda i, j: (j, 0)),
                  pl.BlockSpec((1, D_MODEL), lambda i, j: (0, 0))],
        out_specs=pl.BlockSpec((tm, D_MODEL), lambda i, j: (i, 0)),
        out_shape=jax.ShapeDtypeStruct((n, D_MODEL), F32),
        scratch_shapes=[pltpu.VMEM((tm, D_MODEL), BF16), pltpu.VMEM((tm, D_MODEL), F32)],
        compiler_params=_cparams(("parallel", "arbitrary")),
        name="ffn",
    )(x2, g1, w1, w2, g2)


def _hi_lo(w):
    hi = w.astype(BF16)
    return hi, (w - hi.astype(F32)).astype(BF16)


def _prep_layer(l, w_in, rw_w_up, rw_a_up, rw_g_up, gla_gate_up, w_out, w_ff1, w_ff2):
    w = w_in[l]
    sb_end = 3 * SB_WIDTH
    rw_end = sb_end + RW_IN_WIDTH
    w_sb = w[:, :sb_end].astype(BF16)
    w_rw = w[:, sb_end:rw_end].astype(BF16)
    g0 = rw_end
    kw, vw, gr = GLA_KEY_WIDTH, GLA_VAL_WIDTH, GLA_GATE_RANK
    w_gl = jnp.concatenate([
        w[:, g0:g0 + 2 * kw + vw],
        w[:, g0 + 2 * kw + vw + gr:g0 + GLA_IN_WIDTH],
        w[:, g0 + 2 * kw + vw:g0 + 2 * kw + vw + gr],
        jnp.zeros((D_MODEL, GLA_PAD_WIDTH - GLA_IN_WIDTH), F32)], axis=1).astype(BF16)
    up = jnp.zeros((RW_CODE, 3 * RW_WIDTH), F32)
    up = up.at[0:32, 0:RW_WIDTH].set(rw_w_up[l])
    up = up.at[32:64, RW_WIDTH:2 * RW_WIDTH].set(rw_a_up[l])
    up = up.at[64:128, 2 * RW_WIDTH:].set(rw_g_up[l])
    rw_up_hi, rw_up_lo = _hi_lo(up)
    gup = jnp.zeros((LANES, GLA_KEY_WIDTH), F32).at[0:gr, :].set(gla_gate_up[l])
    gl_up_hi, gl_up_lo = _hi_lo(gup)
    wo = w_out[l].astype(BF16)
    return dict(w_sb=w_sb, w_rw=w_rw, w_gl=w_gl, rw_up_hi=rw_up_hi, rw_up_lo=rw_up_lo,
                gl_up_hi=gl_up_hi, gl_up_lo=gl_up_lo,
                wo_sb=wo[:SB_WIDTH], wo_rw=wo[SB_WIDTH:SB_WIDTH + RW_WIDTH], wo_gl=wo[SB_WIDTH + RW_WIDTH:],
                w1=w_ff1[l].astype(BF16), w2=w_ff2[l].astype(BF16))


def kernel(x, pre_mix_g, w_in, sb_norm_g, rw_mu, rw_w0, rw_w_up, rw_a0, rw_a_up, rw_g_up, rw_k_k, rw_k_a, rw_r_k, rw_gn_g, rw_gn_b, gla_gate_up, gla_gate_b, gla_norm_g, w_out, post_mix_g, pre_ffn_g, w_ff1, w_ff2, post_ffn_g):
    batch, seq, d = x.shape
    depth = w_in.shape[0]
    x2 = x.reshape(batch * seq, d)
    row = lambda t, l: t[l][None, :]
    for l in range(depth):
        wp = _prep_layer(l, w_in, rw_w_up, rw_a_up, rw_g_up, gla_gate_up, w_out, w_ff1, w_ff2)
        sb, rw, gl = _inproj(x2, row(pre_mix_g, l), wp["w_sb"], wp["w_rw"], wp["w_gl"])
        o_sb = _sb_attention(sb, row(sb_norm_g, l), batch, seq)
        o_rw = _rwkv7(rw, dict(mu=row(rw_mu, l), w0=row(rw_w0, l), a0=row(rw_a0, l), k_k=row(rw_k_k, l),
                               k_a=row(rw_k_a, l), r_k=row(rw_r_k, l), gn_g=row(rw_gn_g, l),
                               gn_b=row(rw_gn_b, l), up_hi=wp["rw_up_hi"], up_lo=wp["rw_up_lo"]), batch, seq)
        o_gl = _gla(gl, dict(gate_b=row(gla_gate_b, l), norm_g=row(gla_norm_g, l),
                             up_hi=wp["gl_up_hi"], up_lo=wp["gl_up_lo"]), batch, seq)
        x2 = _outproj(x2, o_sb, o_rw, o_gl, wp["wo_sb"], wp["wo_rw"], wp["wo_gl"], row(post_mix_g, l))
        x2 = _ffn(x2, row(pre_ffn_g, l), wp["w1"], wp["w2"], row(post_ffn_g, l))
    return x2.reshape(batch, seq, d)
```

```python
import functools

import jax
import jax.numpy as jnp
from jax import lax
from jax.experimental import pallas as pl
from jax.experimental.pallas import tpu as pltpu

F32 = jnp.float32
BF16 = jnp.bfloat16

D_MODEL = 1024
HEAD_DIM = 64
LANES = 128
SB_WIDTH = 384
RW_WIDTH = 384
RW_IN_WIDTH = 1280
RW_CODE = 128
GLA_KEY_WIDTH = 128
GLA_VAL_WIDTH = 256
GLA_DK = 32
GLA_GATE_RANK = 16
GLA_IN_WIDTH = 784
GLA_PAD_WIDTH = 896
GLA_GATE_NORMALIZER = 16.0
D_FF = 4096
EPS = 1e-6
RW_GN_EPS = 64e-5
CHUNK = 64
VMEM_LIMIT = 48 * 1024 * 1024

SB_BLOCK = 256
SB_DEAD = -110.0


def _cparams(sem):
    return pltpu.CompilerParams(dimension_semantics=sem, vmem_limit_bytes=VMEM_LIMIT)


def _dot(a, b):
    return jnp.dot(a, b, preferred_element_type=F32)


def _dot_nt(a, b):
    return lax.dot_general(a, b, (((1,), (1,)), ((), ())), preferred_element_type=F32)


def _dot_tn(a, b):
    return lax.dot_general(a, b, (((0,), (0,)), ((), ())), preferred_element_type=F32)


def _split2(x):
    hi = x.astype(BF16)
    lo = (x - hi.astype(F32)).astype(BF16)
    return hi, lo


def _split3(x):
    hi = x.astype(BF16)
    r1 = x - hi.astype(F32)
    mid = r1.astype(BF16)
    lo = (r1 - mid.astype(F32)).astype(BF16)
    return hi, mid, lo


def _dot_x3(a, w_hi, w_lo):
    a_hi, a_lo = _split2(a)
    return _dot(a_hi, w_hi) + (_dot(a_lo, w_hi) + _dot(a_hi, w_lo))


def _dot_exact_lhs(m_bf16, x):
    hi, mid, lo = _split3(x)
    return _dot(m_bf16, hi) + (_dot(m_bf16, mid) + _dot(m_bf16, lo))


def _rms(x, gain):
    return x * lax.rsqrt(jnp.mean(x * x, axis=-1, keepdims=True) + EPS) * gain


def _pair_head_sum(x, low_half):
    s_all = jnp.sum(x, axis=-1, keepdims=True)
    s_low = jnp.sum(jnp.where(low_half, x, 0.0), axis=-1, keepdims=True)
    return jnp.where(low_half, s_low, s_all - s_low)


def _inproj_kernel(x_ref, g_ref, wsb_ref, wrw_ref, wgl_ref, sb_ref, rw_ref, gl_ref):
    h = _rms(x_ref[...], g_ref[...]).astype(BF16)
    sb_ref[...] = _dot(h, wsb_ref[...]).astype(BF16)
    rw_ref[...] = _dot(h, wrw_ref[...])
    gl_ref[...] = _dot(h, wgl_ref[...])


def _inproj(x2, gain, w_sb, w_rw, w_gl, tm=512):
    n = x2.shape[0]
    full = lambda w: pl.BlockSpec(w.shape, lambda i: (0, 0))
    row = lambda width: pl.BlockSpec((tm, width), lambda i: (i, 0))
    return pl.pallas_call(
        _inproj_kernel,
        grid=(n // tm,),
        in_specs=[row(D_MODEL), full(gain), full(w_sb), full(w_rw), full(w_gl)],
        out_specs=[row(3 * SB_WIDTH), row(RW_IN_WIDTH), row(GLA_PAD_WIDTH)],
        out_shape=[jax.ShapeDtypeStruct((n, 3 * SB_WIDTH), BF16),
                   jax.ShapeDtypeStruct((n, RW_IN_WIDTH), F32),
                   jax.ShapeDtypeStruct((n, GLA_PAD_WIDTH), F32)],
        compiler_params=_cparams(("parallel",)),
        name="in_proj",
    )(x2, gain, w_sb, w_rw, w_gl)


def _sb_kernel(q_ref, k_ref, v_ref, g_ref, o_ref):
    qb = SB_BLOCK
    qi = pl.program_id(2)
    lane = lax.broadcasted_iota(jnp.int32, (1, LANES), 1)
    low_half = lane < HEAD_DIM
    row = lax.broadcasted_iota(jnp.int32, (qb, qb), 0)
    col = lax.broadcasted_iota(jnp.int32, (qb, qb), 1)
    causal = col < row
    suffix = (row > col).astype(BF16)
    half = qb // 2
    q = q_ref[...]
    zero = jnp.zeros((), BF16)
    vmasks = (low_half, jnp.logical_not(low_half))
    qhs = [jnp.where(m, q, zero) * jnp.asarray(HEAD_DIM ** -0.5, BF16) for m in vmasks]
    chains = [(h, r) for r in range(2) for h in range(2)]
    qcs = [qhs[h][r * half:(r + 1) * half, :] for h, r in chains]

    def sweep(start, nkeys, carries, accs, masks):
        kb = k_ref[pl.ds(start, qb), :]
        vb = v_ref[pl.ds(start, qb), :]
        zs = [_dot_nt(qc, kb[0:n, :]) for qc, n in zip(qcs, nkeys)]
        lgs = []
        for z, m in zip(zs, masks):
            t = jnp.log(1.0 + jnp.exp(-jnp.abs(z)))
            lg = -(jnp.maximum(z, 0.0) + t)
            lgs.append(lg if m is None else jnp.where(m, lg, 0.0))
        parts = [_split2(lg) for lg in lgs]
        btws = [_dot(hi, suffix[0:n, 0:n]) + _dot(lo, suffix[0:n, 0:n]) for (hi, lo), n in zip(parts, nkeys)]
        ws, new_carries = [], []
        for z, lg, btw, carry, m in zip(zs, lgs, btws, carries, masks):
            w = jnp.exp(z + lg + btw + carry)
            ws.append((w if m is None else jnp.where(m, w, 0.0)).astype(BF16))
            new_carries.append(carry + (btw[:, 0:1] + lg[:, 0:1]))
        new_accs = list(accs)
        for ci, ((h, r), w, n) in enumerate(zip(chains, ws, nkeys)):
            vh = jnp.where(vmasks[h], vb[0:n, :], zero)
            new_accs[r] = new_accs[r] + _dot(w, vh)
        return new_carries, new_accs

    carries = [jnp.zeros((half, 1), F32) for _ in chains]
    accs = [jnp.zeros((half, LANES), F32) for _ in range(2)]
    diag_keys = [half if r == 0 else qb for _, r in chains]
    diag_masks = [causal[0:half, 0:half] if r == 0 else causal[half:, :] for _, r in chains]
    carries, accs = sweep(pl.multiple_of(qi * qb, qb), diag_keys, carries, accs, diag_masks)

    def cond(s):
        j, carries, _ = s
        alive = jnp.maximum(jnp.maximum(jnp.max(carries[0]), jnp.max(carries[1])),
                            jnp.maximum(jnp.max(carries[2]), jnp.max(carries[3])))
        return jnp.logical_and(j >= 0, alive > SB_DEAD)

    def body(s):
        j, carries, accs = s
        carries, accs = sweep(pl.multiple_of(j * qb, qb), [qb] * 4, carries, accs, [None] * 4)
        return j - 1, carries, accs

    _, _, accs = lax.while_loop(cond, body, (qi - 1, carries, accs))
    acc = jnp.concatenate(accs, axis=0)

    ms = _pair_head_sum(acc * acc, low_half) * (1.0 / HEAD_DIM)
    o_ref[...] = (acc * lax.rsqrt(ms + EPS) * g_ref[...]).astype(o_ref.dtype)


def _sb_attention(sb, gain, batch, seq):
    nq = seq // SB_BLOCK
    n = sb.shape[0]
    pairs = SB_WIDTH // LANES
    return pl.pallas_call(
        _sb_kernel,
        grid=(batch, pairs, nq),
        in_specs=[
            pl.BlockSpec((SB_BLOCK, LANES), lambda b, p, i: (b * nq + i, p)),
            pl.BlockSpec((seq, LANES), lambda b, p, i: (b, pairs + p)),
            pl.BlockSpec((seq, LANES), lambda b, p, i: (b, 2 * pairs + p)),
            pl.BlockSpec((1, LANES), lambda b, p, i: (0, p)),
        ],
        out_specs=pl.BlockSpec((SB_BLOCK, LANES), lambda b, p, i: (b * nq + i, p)),
        out_shape=jax.ShapeDtypeStruct((n, SB_WIDTH), BF16),
        compiler_params=_cparams(("parallel", "parallel", "arbitrary")),
        name="sb_attention",
    )(sb, sb, sb, gain)


RW_BLOCK = 256
RW_GROUP = 2


def _rw_kernel(x_ref, mu_ref, w0_ref, a0_ref, kk_ref, ka_ref, rk_ref, gng_ref, gnb_ref,
               uphi_ref, uplo_ref, o_ref,
               state_s, prev_s, r_s, k_s, v_s, lw_s, a_s, b_s, y_s,
               xr_s, xv_s, arb_s, yv_s, bde_s, sv_s, ge_s):
    tb = RW_BLOCK
    c = CHUNK
    pairs = RW_WIDTH // LANES

    @pl.when(pl.program_id(1) == 0)
    def _():
        state_s[...] = jnp.zeros_like(state_s)
        prev_s[...] = jnp.zeros_like(prev_s)

    lane = lax.broadcasted_iota(jnp.int32, (1, LANES), 1)
    low_half = lane < HEAD_DIM

    x = x_ref[...]
    first = lax.broadcasted_iota(jnp.int32, (tb, 1), 0) == 0
    prev = jnp.where(first, prev_s[...], pltpu.roll(x, 1, 0))
    prev_s[...] = x[tb - 1:tb, :]
    xs = x + (prev - x) * mu_ref[...]

    r = xs[:, 0:RW_WIDTH]
    k = xs[:, RW_WIDTH:2 * RW_WIDTH]
    v = xs[:, 2 * RW_WIDTH:3 * RW_WIDTH]
    code = xs[:, 3 * RW_WIDTH:]
    act = jnp.where(lane < 32, jnp.tanh(code), jnp.where(lane < 64, code, jax.nn.sigmoid(code)))
    up = _dot_x3(act, uphi_ref[...], uplo_ref[...])
    wpre = w0_ref[...] + up[:, 0:RW_WIDTH]
    log_w = -(jnp.maximum(-wpre, 0.0) + jnp.log(1.0 + jnp.exp(-jnp.abs(wpre)))) - 0.5
    lw = -jnp.exp(log_w)
    a = jax.nn.sigmoid(a0_ref[...] + up[:, RW_WIDTH:2 * RW_WIDTH])
    gate = up[:, 2 * RW_WIDTH:]
    kmod = k * (1.0 + (a - 1.0) * ka_ref[...])
    kk = k * kk_ref[...]
    bonus_in = r * kmod * rk_ref[...]
    kkn, bon = [], []
    for p in range(pairs):
        sl = slice(p * LANES, (p + 1) * LANES)
        ss = _pair_head_sum(kk[:, sl] * kk[:, sl], low_half)
        kkn.append(kk[:, sl] * lax.rsqrt(jnp.maximum(ss, 1e-12)))
        bon.append(_pair_head_sum(bonus_in[:, sl], low_half))
    kkn = jnp.concatenate(kkn, axis=1)
    bonus = jnp.concatenate(bon, axis=1) * v
    r_s[...] = r
    k_s[...] = kmod
    v_s[...] = v
    lw_s[...] = lw
    a_s[...] = -kkn
    b_s[...] = kkn * a

    rr = lax.broadcasted_iota(jnp.int32, (2 * c, 2 * c), 0)
    cc = lax.broadcasted_iota(jnp.int32, (2 * c, 2 * c), 1)
    same_head = (rr // c) == (cc // c)
    strict = jnp.logical_and(same_head, cc < rr)
    incl = jnp.logical_and(same_head, cc <= rr)
    tr = lax.broadcasted_iota(jnp.int32, (c, c), 0)
    tc = lax.broadcasted_iota(jnp.int32, (c, c), 1)
    tri = (tc <= tr).astype(BF16)
    m_lo = low_half
    m_hi = jnp.logical_not(low_half)

    def stack(xp):
        return jnp.concatenate([jnp.where(m_lo, xp, 0.0), jnp.where(m_hi, xp, 0.0)], axis=0)

    def prepare(gi, carry):
        chains = [(gi * RW_GROUP + cj, p) for cj in range(RW_GROUP) for p in range(pairs)]
        pre = []
        for ci, p in chains:
            rows = pl.ds(pl.multiple_of(ci * c, c), c)
            sl = slice(p * LANES, (p + 1) * LANES)
            lwc = lw_s[rows, sl]
            lc = _dot_exact_lhs(tri, lwc)
            e_incl = jnp.exp(lc)
            e_excl = jnp.exp(lc - lwc)
            e_inv = jnp.exp(-lc)
            g_end = e_incl[c - 1:c, :]
            ag = stack(a_s[rows, sl] * e_excl).astype(BF16)
            rg = stack(r_s[rows, sl] * e_incl).astype(BF16)
            bd = b_s[rows, sl] * e_inv
            kd = k_s[rows, sl] * e_inv
            vst = stack(v_s[rows, sl]).astype(BF16)
            lhs = jnp.concatenate([ag, rg], axis=0)
            rhs = jnp.concatenate([bd, bd, kd, kd], axis=0).astype(BF16)
            bde = stack(bd * g_end).astype(BF16)
            kde = stack(kd * g_end).astype(BF16)
            idx = ci * pairs + p
            ge_s[idx] = g_end
            bde_s[idx] = bde
            pre.append((idx, ag, rg, vst, lhs, rhs, kde))
        gs = [_dot_nt(t[4], t[5]) for t in pre]
        npows, xs_, avs = [], [], []
        for t, g in zip(pre, gs):
            idx = t[0]
            a_ab = jnp.where(strict, g[0:2 * c, 0:2 * c], 0.0)
            a_ak = jnp.where(strict, g[0:2 * c, 2 * c:], 0.0)
            a_rb = jnp.where(incl, g[2 * c:, 0:2 * c], 0.0)
            a_rk = jnp.where(incl, g[2 * c:, 2 * c:], 0.0)
            arb_s[idx] = a_rb.astype(BF16)
            npows.append(a_ab)
            avs.append(jnp.concatenate([a_ak, a_rk], axis=0).astype(BF16))
        avs = [_dot(av, t[3]) for av, t in zip(avs, pre)]
        svs = [_dot_tn(t[3], t[6]) for t in pre]
        for t, av, sv in zip(pre, avs, svs):
            idx = t[0]
            yv_s[idx] = av[2 * c:, :]
            sv_s[idx] = sv
            xs_.append(jnp.concatenate([t[1].astype(F32), av[0:2 * c, :]], axis=1))
        for it in range(6):
            nbs = [n.astype(BF16) for n in npows]
            xs_ = [xc + _dot(nb, xc.astype(BF16)) for xc, nb in zip(xs_, nbs)]
            if it < 5:
                npows = [_dot(nb, nb) for nb in nbs]
        for t, xc in zip(pre, xs_):
            idx = t[0]
            xr_s[idx] = jnp.concatenate([xc[:, 0:LANES].astype(BF16), t[2]], axis=0)
            xv_s[idx] = xc[:, LANES:]
        return carry

    lax.fori_loop(0, tb // (c * RW_GROUP), prepare, 0)

    def advance(ci, carry):
        rows = pl.ds(pl.multiple_of(ci * c, c), c)
        idxs = [ci * pairs + p for p in range(pairs)]
        s0s = [state_s[p] for p in range(pairs)]
        urs = [_dot_nt(xr_s[idx], s0.astype(BF16)) for idx, s0 in zip(idxs, s0s)]
        us = [ur[0:2 * c, :] + xv_s[idx] for ur, idx in zip(urs, idxs)]
        ubs = [u.astype(BF16) for u in us]
        ys = [ur[2 * c:, :] + _dot(arb_s[idx], ub) + yv_s[idx] for ur, idx, ub in zip(urs, idxs, ubs)]
        for p in range(pairs):
            idx = idxs[p]
            state_s[p] = s0s[p] * ge_s[idx] + _dot_tn(ubs[p], bde_s[idx]) + sv_s[idx]
            y_s[rows, p * LANES:(p + 1) * LANES] = ys[p][0:c, :] + ys[p][c:2 * c, :]
        return carry

    lax.fori_loop(0, tb // c, advance, 0)

    outs = []
    for p in range(pairs):
        sl = slice(p * LANES, (p + 1) * LANES)
        y = y_s[:, sl]
        mean = _pair_head_sum(y, low_half) * (1.0 / HEAD_DIM)
        d = y - mean
        var = _pair_head_sum(d * d, low_half) * (1.0 / HEAD_DIM)
        outs.append(d * lax.rsqrt(var + RW_GN_EPS))
    yn = jnp.concatenate(outs, axis=1) * gng_ref[...] + gnb_ref[...]
    o_ref[...] = ((yn + bonus) * gate).astype(o_ref.dtype)


def _rwkv7(rw, prm, batch, seq):
    n = rw.shape[0]
    nb = seq // RW_BLOCK
    vec = pl.BlockSpec((1, RW_WIDTH), lambda b, i: (0, 0))
    up = pl.BlockSpec((RW_CODE, 3 * RW_WIDTH), lambda b, i: (0, 0))
    blk = lambda: pltpu.VMEM((RW_BLOCK, RW_WIDTH), F32)
    n_chain = (RW_BLOCK // CHUNK) * (RW_WIDTH // LANES)
    tile = lambda rows, dt: pltpu.VMEM((n_chain, rows, LANES), dt)
    return pl.pallas_call(
        _rw_kernel,
        grid=(batch, nb),
        in_specs=[pl.BlockSpec((RW_BLOCK, RW_IN_WIDTH), lambda b, i: (b * nb + i, 0)),
                  pl.BlockSpec((1, RW_IN_WIDTH), lambda b, i: (0, 0)),
                  vec, vec, vec, vec, vec, vec, vec, up, up],
        out_specs=pl.BlockSpec((RW_BLOCK, RW_WIDTH), lambda b, i: (b * nb + i, 0)),
        out_shape=jax.ShapeDtypeStruct((n, RW_WIDTH), BF16),
        scratch_shapes=[pltpu.VMEM((RW_WIDTH // LANES, LANES, LANES), F32),
                        pltpu.VMEM((1, RW_IN_WIDTH), F32),
                        blk(), blk(), blk(), blk(), blk(), blk(), blk(),
                        tile(2 * LANES, BF16), tile(LANES, F32), tile(LANES, BF16), tile(LANES, F32),
                        tile(LANES, BF16), tile(LANES, F32), tile(1, F32)],
        compiler_params=_cparams(("parallel", "arbitrary")),
        name="rwkv7",
    )(rw, prm["mu"], prm["w0"], prm["a0"], prm["k_k"], prm["k_a"], prm["r_k"],
      prm["gn_g"], prm["gn_b"], prm["up_hi"], prm["up_lo"])


GLA_BLOCK = 256


def _gla_kernel(x_ref, gb_ref, ng_ref, uphi_ref, uplo_ref, o_ref, state_s, q_s, k_s, la_s, o_s):
    tb = GLA_BLOCK
    c = CHUNK
    heads = GLA_VAL_WIDTH // HEAD_DIM

    @pl.when(pl.program_id(1) == 0)
    def _():
        state_s[...] = jnp.zeros_like(state_s)

    kw, vw = GLA_KEY_WIDTH, GLA_VAL_WIDTH
    code = x_ref[:, 2 * kw + 2 * vw:]
    pre = _dot_x3(code, uphi_ref[...], uplo_ref[...]) + gb_ref[...]
    la_s[...] = -(jnp.maximum(-pre, 0.0) + jnp.log(1.0 + jnp.exp(-jnp.abs(pre)))) * (1.0 / GLA_GATE_NORMALIZER)
    q_s[...] = x_ref[:, 0:kw] * (GLA_DK ** -0.5)
    k_s[...] = x_ref[:, kw:2 * kw]

    tr = lax.broadcasted_iota(jnp.int32, (c, c), 0)
    tc = lax.broadcasted_iota(jnp.int32, (c, c), 1)
    tri = (tc <= tr).astype(BF16)
    srow = lax.broadcasted_iota(jnp.int32, (heads * c, c), 0)
    scol = lax.broadcasted_iota(jnp.int32, (heads * c, c), 1)
    causal = scol <= (srow % c)
    klane = lax.broadcasted_iota(jnp.int32, (1, kw), 1) // GLA_DK
    vlane = lax.broadcasted_iota(jnp.int32, (1, vw), 1) // HEAD_DIM
    st_r = lax.broadcasted_iota(jnp.int32, (vw, kw), 0) // HEAD_DIM
    st_c = lax.broadcasted_iota(jnp.int32, (vw, kw), 1) // GLA_DK
    st_mask = st_r == st_c

    n_chunk = tb // c
    pre = []
    for ci in range(n_chunk):
        rows = slice(ci * c, (ci + 1) * c)
        bc = _dot_exact_lhs(tri, la_s[rows, :])
        e_pos = jnp.exp(bc)
        qe = q_s[rows, :] * e_pos
        kc = k_s[rows, :]
        ke = (kc * jnp.exp(-bc)).astype(BF16)
        k2 = (kc * jnp.exp(bc[c - 1:c, :] - bc)).astype(BF16)
        vb = x_ref[rows, 2 * kw:2 * kw + vw].astype(BF16)
        qst = jnp.concatenate([jnp.where(klane == h, qe, 0.0) for h in range(heads)], axis=0).astype(BF16)
        pre.append((qe.astype(BF16), ke, k2, vb, qst, e_pos[c - 1:c, :]))
    scores = [jnp.where(causal, _dot_nt(t[4], t[1]), 0.0).astype(BF16) for t in pre]
    o_sts = [_dot(sc, t[3]) for sc, t in zip(scores, pre)]
    incs = [jnp.where(st_mask, _dot_tn(t[3], t[2]), 0.0) for t in pre]
    state = state_s[...]
    for ci in range(n_chunk):
        o = _dot_nt(pre[ci][0], state.astype(BF16))
        for h in range(heads):
            o = o + jnp.where(vlane == h, o_sts[ci][h * c:(h + 1) * c, :], 0.0)
        o_s[ci * c:(ci + 1) * c, :] = o
        state = state * pre[ci][5] + incs[ci]
    state_s[...] = state

    lane = lax.broadcasted_iota(jnp.int32, (1, LANES), 1)
    low_half = lane < HEAD_DIM
    outs = []
    for p in range(vw // LANES):
        o = o_s[:, p * LANES:(p + 1) * LANES]
        ms = _pair_head_sum(o * o, low_half) * (1.0 / HEAD_DIM)
        outs.append(o * lax.rsqrt(ms + EPS))
    g = x_ref[:, 2 * kw + vw:2 * kw + 2 * vw]
    on = jnp.concatenate(outs, axis=1) * ng_ref[...]
    o_ref[...] = (on * (g * jax.nn.sigmoid(g))).astype(o_ref.dtype)


def _gla(gl, prm, batch, seq):
    n = gl.shape[0]
    nb = seq // GLA_BLOCK
    return pl.pallas_call(
        _gla_kernel,
        grid=(batch, nb),
        in_specs=[pl.BlockSpec((GLA_BLOCK, GLA_PAD_WIDTH), lambda b, i: (b * nb + i, 0)),
                  pl.BlockSpec((1, GLA_KEY_WIDTH), lambda b, i: (0, 0)),
                  pl.BlockSpec((1, GLA_VAL_WIDTH), lambda b, i: (0, 0)),
                  pl.BlockSpec((LANES, GLA_KEY_WIDTH), lambda b, i: (0, 0)),
                  pl.BlockSpec((LANES, GLA_KEY_WIDTH), lambda b, i: (0, 0))],
        out_specs=pl.BlockSpec((GLA_BLOCK, GLA_VAL_WIDTH), lambda b, i: (b * nb + i, 0)),
        out_shape=jax.ShapeDtypeStruct((n, GLA_VAL_WIDTH), BF16),
        scratch_shapes=[pltpu.VMEM((GLA_VAL_WIDTH, GLA_KEY_WIDTH), F32),
                        pltpu.VMEM((GLA_BLOCK, GLA_KEY_WIDTH), F32),
                        pltpu.VMEM((GLA_BLOCK, GLA_KEY_WIDTH), F32),
                        pltpu.VMEM((GLA_BLOCK, GLA_KEY_WIDTH), F32),
                        pltpu.VMEM((GLA_BLOCK, GLA_VAL_WIDTH), F32)],
        compiler_params=_cparams(("parallel", "arbitrary")),
        name="gla",
    )(gl, prm["gate_b"], prm["norm_g"], prm["up_hi"], prm["up_lo"])


def _outproj_kernel(x_ref, sb_ref, rw_ref, gl_ref, wsb_ref, wrw_ref, wgl_ref, g_ref, o_ref):
    mixed = _dot(sb_ref[...], wsb_ref[...]) + _dot(rw_ref[...], wrw_ref[...]) + _dot(gl_ref[...], wgl_ref[...])
    o_ref[...] = x_ref[...] + _rms(mixed, g_ref[...])


def _outproj(x2, o_sb, o_rw, o_gl, w_sb, w_rw, w_gl, gain, tm=512):
    n = x2.shape[0]
    full = lambda w: pl.BlockSpec(w.shape, lambda i: (0, 0))
    row = lambda width: pl.BlockSpec((tm, width), lambda i: (i, 0))
    return pl.pallas_call(
        _outproj_kernel,
        grid=(n // tm,),
        in_specs=[row(D_MODEL), row(SB_WIDTH), row(RW_WIDTH), row(GLA_VAL_WIDTH),
                  full(w_sb), full(w_rw), full(w_gl), full(gain)],
        out_specs=row(D_MODEL),
        out_shape=jax.ShapeDtypeStruct((n, D_MODEL), F32),
        compiler_params=_cparams(("parallel",)),
        name="out_proj",
    )(x2, o_sb, o_rw, o_gl, w_sb, w_rw, w_gl, gain)


def _ffn_kernel(x_ref, g1_ref, w1_ref, w2_ref, g2_ref, o_ref, h_s, acc_s):
    j = pl.program_id(1)

    @pl.when(j == 0)
    def _():
        h_s[...] = _rms(x_ref[...], g1_ref[...]).astype(BF16)
        acc_s[...] = jnp.zeros_like(acc_s)

    a = jnp.maximum(_dot(h_s[...], w1_ref[...]), 0.0)
    acc_s[...] += _dot((a * a).astype(BF16), w2_ref[...])

    @pl.when(j == pl.num_programs(1) - 1)
    def _():
        o_ref[...] = x_ref[...] + _rms(acc_s[...], g2_ref[...])


def _ffn(x2, g1, w1, w2, g2, tm=1024, tf=1024):
    n = x2.shape[0]
    return pl.pallas_call(
        _ffn_kernel,
        grid=(n // tm, D_FF // tf),
        in_specs=[pl.BlockSpec((tm, D_MODEL), lambda i, j: (i, 0)),
                  pl.BlockSpec((1, D_MODEL), lambda i, j: (0, 0)),
                  pl.BlockSpec((D_MODEL, tf), lambda i, j: (0, j)),
                  pl.BlockSpec((tf, D_MODEL), lambda i, j: (j, 0)),
                  pl.BlockSpec((1, D_MODEL), lambda i, j: (0, 0))],
        out_specs=pl.BlockSpec((tm, D_MODEL), lambda i, j: (i, 0)),
        out_shape=jax.ShapeDtypeStruct((n, D_MODEL), F32),
        scratch_shapes=[pltpu.VMEM((tm, D_MODEL), BF16), pltpu.VMEM((tm, D_MODEL), F32)],
        compiler_params=_cparams(("parallel", "arbitrary")),
        name="ffn",
    )(x2, g1, w1, w2, g2)


def _hi_lo(w):
    hi = w.astype(BF16)
    return hi, (w - hi.astype(F32)).astype(BF16)


def _prep_layer(l, w_in, rw_w_up, rw_a_up, rw_g_up, gla_gate_up, w_out, w_ff1, w_ff2):
    w = w_in[l]
    sb_end = 3 * SB_WIDTH
    rw_end = sb_end + RW_IN_WIDTH
    w_sb = w[:, :sb_end].astype(BF16)
    w_rw = w[:, sb_end:rw_end].astype(BF16)
    g0 = rw_end
    kw, vw, gr = GLA_KEY_WIDTH, GLA_VAL_WIDTH, GLA_GATE_RANK
    w_gl = jnp.concatenate([
        w[:, g0:g0 + 2 * kw + vw],
        w[:, g0 + 2 * kw + vw + gr:g0 + GLA_IN_WIDTH],
        w[:, g0 + 2 * kw + vw:g0 + 2 * kw + vw + gr],
        jnp.zeros((D_MODEL, GLA_PAD_WIDTH - GLA_IN_WIDTH), F32)], axis=1).astype(BF16)
    up = jnp.zeros((RW_CODE, 3 * RW_WIDTH), F32)
    up = up.at[0:32, 0:RW_WIDTH].set(rw_w_up[l])
    up = up.at[32:64, RW_WIDTH:2 * RW_WIDTH].set(rw_a_up[l])
    up = up.at[64:128, 2 * RW_WIDTH:].set(rw_g_up[l])
    rw_up_hi, rw_up_lo = _hi_lo(up)
    gup = jnp.zeros((LANES, GLA_KEY_WIDTH), F32).at[0:gr, :].set(gla_gate_up[l])
    gl_up_hi, gl_up_lo = _hi_lo(gup)
    wo = w_out[l].astype(BF16)
    return dict(w_sb=w_sb, w_rw=w_rw, w_gl=w_gl, rw_up_hi=rw_up_hi, rw_up_lo=rw_up_lo,
                gl_up_hi=gl_up_hi, gl_up_lo=gl_up_lo,
                wo_sb=wo[:SB_WIDTH], wo_rw=wo[SB_WIDTH:SB_WIDTH + RW_WIDTH], wo_gl=wo[SB_WIDTH + RW_WIDTH:],
                w1=w_ff1[l].astype(BF16), w2=w_ff2[l].astype(BF16))


def kernel(x, pre_mix_g, w_in, sb_norm_g, rw_mu, rw_w0, rw_w_up, rw_a0, rw_a_up, rw_g_up, rw_k_k, rw_k_a, rw_r_k, rw_gn_g, rw_gn_b, gla_gate_up, gla_gate_b, gla_norm_g, w_out, post_mix_g, pre_ffn_g, w_ff1, w_ff2, post_ffn_g):
    batch, seq, d = x.shape
    depth = w_in.shape[0]
    x2 = x.reshape(batch * seq, d)
    row = lambda t, l: t[l][None, :]
    for l in range(depth):
        wp = _prep_layer(l, w_in, rw_w_up, rw_a_up, rw_g_up, gla_gate_up, w_out, w_ff1, w_ff2)
        sb, rw, gl = _inproj(x2, row(pre_mix_g, l), wp["w_sb"], wp["w_rw"], wp["w_gl"])
        o_sb = _sb_attention(sb, row(sb_norm_g, l), batch, seq)
        o_rw = _rwkv7(rw, dict(mu=row(rw_mu, l), w0=row(rw_w0, l), a0=row(rw_a0, l), k_k=row(rw_k_k, l),
                               k_a=row(rw_k_a, l), r_k=row(rw_r_k, l), gn_g=row(rw_gn_g, l),
                               gn_b=row(rw_gn_b, l), up_hi=wp["rw_up_hi"], up_lo=wp["rw_up_lo"]), batch, seq)
        o_gl = _gla(gl, dict(gate_b=row(gla_gate_b, l), norm_g=row(gla_norm_g, l),
                             up_hi=wp["gl_up_hi"], up_lo=wp["gl_up_lo"]), batch, seq)
        x2 = _outproj(x2, o_sb, o_rw, o_gl, wp["wo_sb"], wp["wo_rw"], wp["wo_gl"], row(post_mix_g, l))
        x2 = _ffn(x2, row(pre_ffn_g, l), wp["w1"], wp["w2"], row(post_ffn_g, l))
    return x2.reshape(batch, seq, d)
```

```python
import functools

import jax
import jax.numpy as jnp
from jax import lax
from jax.experimental import pallas as pl
from jax.experimental.pallas import tpu as pltpu

F32 = jnp.float32
BF16 = jnp.bfloat16

D_MODEL = 1024
HEAD_DIM = 64
LANES = 128
SB_WIDTH = 384
RW_WIDTH = 384
RW_IN_WIDTH = 1280
RW_CODE = 128
GLA_KEY_WIDTH = 128
GLA_VAL_WIDTH = 256
GLA_DK = 32
GLA_GATE_RANK = 16
GLA_IN_WIDTH = 784
GLA_PAD_WIDTH = 896
GLA_GATE_NORMALIZER = 16.0
D_FF = 4096
EPS = 1e-6
RW_GN_EPS = 64e-5
CHUNK = 64
VMEM_LIMIT = 48 * 1024 * 1024

SB_BLOCK = 256
LOG2E = 1.4426950408889634
SB_DEAD = 110.0 * LOG2E


def _cparams(sem):
    return pltpu.CompilerParams(dimension_semantics=sem, vmem_limit_bytes=VMEM_LIMIT)


def _dot(a, b):
    return jnp.dot(a, b, preferred_element_type=F32)


def _dot_nt(a, b):
    return lax.dot_general(a, b, (((1,), (1,)), ((), ())), preferred_element_type=F32)


def _dot_tn(a, b):
    return lax.dot_general(a, b, (((0,), (0,)), ((), ())), preferred_element_type=F32)


def _split2(x):
    hi = x.astype(BF16)
    lo = (x - hi.astype(F32)).astype(BF16)
    return hi, lo


def _split3(x):
    hi = x.astype(BF16)
    r1 = x - hi.astype(F32)
    mid = r1.astype(BF16)
    lo = (r1 - mid.astype(F32)).astype(BF16)
    return hi, mid, lo


def _dot_x3(a, w_hi, w_lo):
    a_hi, a_lo = _split2(a)
    return _dot(a_hi, w_hi) + (_dot(a_lo, w_hi) + _dot(a_hi, w_lo))


def _dot_exact_lhs(m_bf16, x):
    hi, mid, lo = _split3(x)
    return _dot(m_bf16, hi) + (_dot(m_bf16, mid) + _dot(m_bf16, lo))


def _rms(x, gain):
    return x * lax.rsqrt(jnp.mean(x * x, axis=-1, keepdims=True) + EPS) * gain


def _pair_head_sum(x, low_half):
    s_all = jnp.sum(x, axis=-1, keepdims=True)
    s_low = jnp.sum(jnp.where(low_half, x, 0.0), axis=-1, keepdims=True)
    return jnp.where(low_half, s_low, s_all - s_low)


def _inproj_kernel(x_ref, g_ref, wsb_ref, wrw_ref, wgl_ref, sb_ref, rw_ref, gl_ref):
    h = _rms(x_ref[...], g_ref[...]).astype(BF16)
    sb_ref[...] = _dot(h, wsb_ref[...]).astype(BF16)
    rw_ref[...] = _dot(h, wrw_ref[...])
    gl_ref[...] = _dot(h, wgl_ref[...])


def _inproj(x2, gain, w_sb, w_rw, w_gl, tm=512):
    n = x2.shape[0]
    full = lambda w: pl.BlockSpec(w.shape, lambda i: (0, 0))
    row = lambda width: pl.BlockSpec((tm, width), lambda i: (i, 0))
    return pl.pallas_call(
        _inproj_kernel,
        grid=(n // tm,),
        in_specs=[row(D_MODEL), full(gain), full(w_sb), full(w_rw), full(w_gl)],
        out_specs=[row(3 * SB_WIDTH), row(RW_IN_WIDTH), row(GLA_PAD_WIDTH)],
        out_shape=[jax.ShapeDtypeStruct((n, 3 * SB_WIDTH), BF16),
                   jax.ShapeDtypeStruct((n, RW_IN_WIDTH), F32),
                   jax.ShapeDtypeStruct((n, GLA_PAD_WIDTH), F32)],
        compiler_params=_cparams(("parallel",)),
        name="in_proj",
    )(x2, gain, w_sb, w_rw, w_gl)


def _sb_kernel(q_ref, k_ref, v_ref, g_ref, o_ref):
    qb = SB_BLOCK
    qi = pl.program_id(2)
    lane = lax.broadcasted_iota(jnp.int32, (1, LANES), 1)
    low_half = lane < HEAD_DIM
    row = lax.broadcasted_iota(jnp.int32, (qb, qb), 0)
    col = lax.broadcasted_iota(jnp.int32, (qb, qb), 1)
    causal = col < row
    suffix = (row > col).astype(BF16)
    half = qb // 2
    q = q_ref[...]
    zero = jnp.zeros((), BF16)
    vmasks = (low_half, jnp.logical_not(low_half))
    qhs = [jnp.where(m, q, zero) * jnp.asarray(HEAD_DIM ** -0.5, BF16) for m in vmasks]
    chains = [(h, r) for r in range(2) for h in range(2)]
    qcs = [qhs[h][r * half:(r + 1) * half, :] for h, r in chains]

    def front(qc, kb, n, mask):
        z = _dot_nt(qc, kb[0:n, :]) * LOG2E
        p = jnp.maximum(z, 0.0) + jnp.log2(1.0 + jnp.exp2(-jnp.abs(z)))
        if mask is not None:
            p = jnp.where(mask, p, 0.0)
        later = _dot(p.astype(BF16), suffix[0:n, 0:n])
        return z, p, later

    def back(z, p, later, spent, mask):
        w = jnp.exp2((z - p) - (later + spent))
        if mask is not None:
            w = jnp.where(mask, w, 0.0)
        return w.astype(BF16), spent + (later[:, 0:1] + p[:, 0:1])

    def values(vb, n):
        return [jnp.where(m, vb[0:n, :], zero) for m in vmasks]

    d_start = pl.multiple_of(qi * qb, qb)
    p_start = pl.multiple_of(jnp.maximum(qi - 1, 0) * qb, qb)
    kd, vd = k_ref[pl.ds(d_start, qb), :], v_ref[pl.ds(d_start, qb), :]
    kp, vp = k_ref[pl.ds(p_start, qb), :], v_ref[pl.ds(p_start, qb), :]
    vp = jnp.where(qi > 0, vp, zero)
    d_keys = [half if r == 0 else qb for _, r in chains]
    d_masks = [causal[0:half, 0:half] if r == 0 else causal[half:, :] for _, r in chains]
    fd = [front(qc, kd, n, m) for qc, n, m in zip(qcs, d_keys, d_masks)]
    fp = [front(qc, kp, qb, None) for qc in qcs]
    bd = [back(*f, jnp.zeros((half, 1), F32), m) for f, m in zip(fd, d_masks)]
    bp = [back(*f, b[1], None) for f, b in zip(fp, bd)]
    spents = [b[1] for b in bp]
    accs = []
    for r in range(2):
        n = d_keys[2 * r]
        wcat = jnp.concatenate([bd[2 * r][0], bd[2 * r + 1][0], bp[2 * r][0], bp[2 * r + 1][0]], axis=1)
        vcat = jnp.concatenate(values(vd, n) + values(vp, qb), axis=0)
        accs.append(_dot(wcat, vcat))

    def cond(s):
        j, spents, _ = s
        alive = jnp.minimum(jnp.minimum(jnp.min(spents[0]), jnp.min(spents[1])),
                            jnp.minimum(jnp.min(spents[2]), jnp.min(spents[3])))
        return jnp.logical_and(j >= 0, alive < SB_DEAD)

    def body(s):
        j, spents, accs = s
        start = pl.multiple_of(j * qb, qb)
        kb, vb = k_ref[pl.ds(start, qb), :], v_ref[pl.ds(start, qb), :]
        fs = [front(qc, kb, qb, None) for qc in qcs]
        bs = [back(*f, sp, None) for f, sp in zip(fs, spents)]
        vh = jnp.concatenate(values(vb, qb), axis=0)
        accs = [acc + _dot(jnp.concatenate([bs[2 * r][0], bs[2 * r + 1][0]], axis=1), vh)
                for r, acc in enumerate(accs)]
        return j - 1, [b[1] for b in bs], accs

    _, _, accs = lax.while_loop(cond, body, (qi - 2, spents, accs))
    acc = jnp.concatenate(accs, axis=0)

    ms = _pair_head_sum(acc * acc, low_half) * (1.0 / HEAD_DIM)
    o_ref[...] = (acc * lax.rsqrt(ms + EPS) * g_ref[...]).astype(o_ref.dtype)


def _sb_attention(sb, gain, batch, seq):
    nq = seq // SB_BLOCK
    n = sb.shape[0]
    pairs = SB_WIDTH // LANES
    return pl.pallas_call(
        _sb_kernel,
        grid=(batch, pairs, nq),
        in_specs=[
            pl.BlockSpec((SB_BLOCK, LANES), lambda b, p, i: (b * nq + i, p)),
            pl.BlockSpec((seq, LANES), lambda b, p, i: (b, pairs + p)),
            pl.BlockSpec((seq, LANES), lambda b, p, i: (b, 2 * pairs + p)),
            pl.BlockSpec((1, LANES), lambda b, p, i: (0, p)),
        ],
        out_specs=pl.BlockSpec((SB_BLOCK, LANES), lambda b, p, i: (b * nq + i, p)),
        out_shape=jax.ShapeDtypeStruct((n, SB_WIDTH), BF16),
        compiler_params=_cparams(("parallel", "parallel", "arbitrary")),
        name="sb_attention",
    )(sb, sb, sb, gain)


RW_BLOCK = 256
RW_GROUP = 2


def _rw_kernel(x_ref, mu_ref, w0_ref, a0_ref, kk_ref, ka_ref, rk_ref, gng_ref, gnb_ref,
               uphi_ref, uplo_ref, o_ref,
               state_s, prev_s, r_s, k_s, v_s, lw_s, a_s, b_s, y_s,
               rp_s, yp_s, p_s, q_s, ge_s):
    tb = RW_BLOCK
    c = CHUNK
    pairs = RW_WIDTH // LANES

    @pl.when(pl.program_id(1) == 0)
    def _():
        state_s[...] = jnp.zeros_like(state_s)
        prev_s[...] = jnp.zeros_like(prev_s)

    lane = lax.broadcasted_iota(jnp.int32, (1, LANES), 1)
    low_half = lane < HEAD_DIM

    x = x_ref[...]
    first = lax.broadcasted_iota(jnp.int32, (tb, 1), 0) == 0
    prev = jnp.where(first, prev_s[...], pltpu.roll(x, 1, 0))
    prev_s[...] = x[tb - 1:tb, :]
    xs = x + (prev - x) * mu_ref[...]

    r = xs[:, 0:RW_WIDTH]
    k = xs[:, RW_WIDTH:2 * RW_WIDTH]
    v = xs[:, 2 * RW_WIDTH:3 * RW_WIDTH]
    code = xs[:, 3 * RW_WIDTH:]
    act = jnp.where(lane < 32, jnp.tanh(code), jnp.where(lane < 64, code, jax.nn.sigmoid(code)))
    up = _dot_x3(act, uphi_ref[...], uplo_ref[...])
    wpre = w0_ref[...] + up[:, 0:RW_WIDTH]
    log_w = -(jnp.maximum(-wpre, 0.0) + jnp.log(1.0 + jnp.exp(-jnp.abs(wpre)))) - 0.5
    lw = -jnp.exp(log_w)
    a = jax.nn.sigmoid(a0_ref[...] + up[:, RW_WIDTH:2 * RW_WIDTH])
    gate = up[:, 2 * RW_WIDTH:]
    kmod = k * (1.0 + (a - 1.0) * ka_ref[...])
    kk = k * kk_ref[...]
    bonus_in = r * kmod * rk_ref[...]
    kkn, bon = [], []
    for p in range(pairs):
        sl = slice(p * LANES, (p + 1) * LANES)
        ss = _pair_head_sum(kk[:, sl] * kk[:, sl], low_half)
        kkn.append(kk[:, sl] * lax.rsqrt(jnp.maximum(ss, 1e-12)))
        bon.append(_pair_head_sum(bonus_in[:, sl], low_half))
    kkn = jnp.concatenate(kkn, axis=1)
    bonus = jnp.concatenate(bon, axis=1) * v
    r_s[...] = r
    k_s[...] = kmod
    v_s[...] = v
    lw_s[...] = lw
    a_s[...] = -kkn
    b_s[...] = kkn * a

    rr = lax.broadcasted_iota(jnp.int32, (2 * c, 2 * c), 0)
    cc = lax.broadcasted_iota(jnp.int32, (2 * c, 2 * c), 1)
    same_head = (rr // c) == (cc // c)
    strict = jnp.logical_and(same_head, cc < rr)
    incl = jnp.logical_and(same_head, cc <= rr)
    tr = lax.broadcasted_iota(jnp.int32, (c, c), 0)
    tc = lax.broadcasted_iota(jnp.int32, (c, c), 1)
    tri = (tc <= tr).astype(BF16)
    m_lo = low_half
    m_hi = jnp.logical_not(low_half)

    def stack(xp):
        return jnp.concatenate([jnp.where(m_lo, xp, 0.0), jnp.where(m_hi, xp, 0.0)], axis=0)

    def prepare(gi, carry):
        chains = [(gi * RW_GROUP + cj, p) for cj in range(RW_GROUP) for p in range(pairs)]
        pre = []
        for ci, p in chains:
            rows = pl.ds(pl.multiple_of(ci * c, c), c)
            sl = slice(p * LANES, (p + 1) * LANES)
            lwc = lw_s[rows, sl]
            lc = _dot_exact_lhs(tri, lwc)
            e_incl = jnp.exp(lc)
            e_excl = jnp.exp(lc - lwc)
            e_inv = jnp.exp(-lc)
            g_end = e_incl[c - 1:c, :]
            ag = stack(a_s[rows, sl] * e_excl).astype(BF16)
            rg = stack(r_s[rows, sl] * e_incl).astype(BF16)
            bd = b_s[rows, sl] * e_inv
            kd = k_s[rows, sl] * e_inv
            vst = stack(v_s[rows, sl]).astype(BF16)
            lhs = jnp.concatenate([ag, rg], axis=0)
            rhs = jnp.concatenate([bd, bd, kd, kd], axis=0).astype(BF16)
            bde = stack(bd * g_end).astype(BF16)
            kde = stack(kd * g_end).astype(BF16)
            idx = ci * pairs + p
            ge_s[idx] = g_end
            pre.append((idx, ag, rg, vst, lhs, rhs, kde, bde))
        gs = [_dot_nt(t[4], t[5]) for t in pre]
        npows, xs_, avs, arbs = [], [], [], []
        for t, g in zip(pre, gs):
            a_ab = jnp.where(strict, g[0:2 * c, 0:2 * c], 0.0)
            a_ak = jnp.where(strict, g[0:2 * c, 2 * c:], 0.0)
            a_rb = jnp.where(incl, g[2 * c:, 0:2 * c], 0.0)
            a_rk = jnp.where(incl, g[2 * c:, 2 * c:], 0.0)
            arbs.append(a_rb.astype(BF16))
            npows.append(a_ab)
            avs.append(jnp.concatenate([a_ak, a_rk], axis=0).astype(BF16))
        avs = [_dot(av, t[3]) for av, t in zip(avs, pre)]
        svs = [_dot_tn(t[3], t[6]) for t in pre]
        for t, av in zip(pre, avs):
            xs_.append(jnp.concatenate([t[1].astype(F32), av[0:2 * c, :]], axis=1))
        for it in range(6):
            nbs = [n.astype(BF16) for n in npows]
            xs_ = [xc + _dot(nb, xc.astype(BF16)) for xc, nb in zip(xs_, nbs)]
            if it < 5:
                npows = [_dot(nb, nb) for nb in nbs]
        xbs = [xc.astype(BF16) for xc in xs_]
        ras = [_dot(arb, xb) for arb, xb in zip(arbs, xbs)]
        pqs = [_dot_tn(xb, t[7]) for xb, t in zip(xbs, pre)]
        for t, av, sv, ra, pq in zip(pre, avs, svs, ras, pqs):
            idx = t[0]
            rp_s[idx] = (t[2].astype(F32) + ra[:, 0:LANES]).astype(BF16)
            yp_s[idx] = ra[:, LANES:] + av[2 * c:, :]
            p_s[idx] = pq[0:LANES, :].astype(BF16)
            q_s[idx] = pq[LANES:, :] + sv
        return carry

    lax.fori_loop(0, tb // (c * RW_GROUP), prepare, 0)

    states = [state_s[p] for p in range(pairs)]
    for ci in range(tb // c):
        for p in range(pairs):
            idx = ci * pairs + p
            sb = states[p].astype(BF16)
            y = _dot_nt(rp_s[idx], sb) + yp_s[idx]
            y_s[ci * c:(ci + 1) * c, p * LANES:(p + 1) * LANES] = y[0:c, :] + y[c:2 * c, :]
            states[p] = states[p] * ge_s[idx] + _dot(sb, p_s[idx]) + q_s[idx]
    for p in range(pairs):
        state_s[p] = states[p]

    outs = []
    for p in range(pairs):
        sl = slice(p * LANES, (p + 1) * LANES)
        y = y_s[:, sl]
        mean = _pair_head_sum(y, low_half) * (1.0 / HEAD_DIM)
        d = y - mean
        var = _pair_head_sum(d * d, low_half) * (1.0 / HEAD_DIM)
        outs.append(d * lax.rsqrt(var + RW_GN_EPS))
    yn = jnp.concatenate(outs, axis=1) * gng_ref[...] + gnb_ref[...]
    o_ref[...] = ((yn + bonus) * gate).astype(o_ref.dtype)


def _rwkv7(rw, prm, batch, seq):
    n = rw.shape[0]
    nb = seq // RW_BLOCK
    vec = pl.BlockSpec((1, RW_WIDTH), lambda b, i: (0, 0))
    up = pl.BlockSpec((RW_CODE, 3 * RW_WIDTH), lambda b, i: (0, 0))
    blk = lambda: pltpu.VMEM((RW_BLOCK, RW_WIDTH), F32)
    n_chain = (RW_BLOCK // CHUNK) * (RW_WIDTH // LANES)
    tile = lambda rows, dt: pltpu.VMEM((n_chain, rows, LANES), dt)
    return pl.pallas_call(
        _rw_kernel,
        grid=(batch, nb),
        in_specs=[pl.BlockSpec((RW_BLOCK, RW_IN_WIDTH), lambda b, i: (b * nb + i, 0)),
                  pl.BlockSpec((1, RW_IN_WIDTH), lambda b, i: (0, 0)),
                  vec, vec, vec, vec, vec, vec, vec, up, up],
        out_specs=pl.BlockSpec((RW_BLOCK, RW_WIDTH), lambda b, i: (b * nb + i, 0)),
        out_shape=jax.ShapeDtypeStruct((n, RW_WIDTH), BF16),
        scratch_shapes=[pltpu.VMEM((RW_WIDTH // LANES, LANES, LANES), F32),
                        pltpu.VMEM((1, RW_IN_WIDTH), F32),
                        blk(), blk(), blk(), blk(), blk(), blk(), blk(),
                        tile(LANES, BF16), tile(LANES, F32), tile(LANES, BF16), tile(LANES, F32),
                        tile(1, F32)],
        compiler_params=_cparams(("parallel", "arbitrary")),
        name="rwkv7",
    )(rw, prm["mu"], prm["w0"], prm["a0"], prm["k_k"], prm["k_a"], prm["r_k"],
      prm["gn_g"], prm["gn_b"], prm["up_hi"], prm["up_lo"])


GLA_BLOCK = 256


def _gla_kernel(x_ref, gb_ref, ng_ref, uphi_ref, uplo_ref, o_ref, state_s, q_s, k_s, la_s, o_s):
    tb = GLA_BLOCK
    c = CHUNK
    heads = GLA_VAL_WIDTH // HEAD_DIM

    @pl.when(pl.program_id(1) == 0)
    def _():
        state_s[...] = jnp.zeros_like(state_s)

    kw, vw = GLA_KEY_WIDTH, GLA_VAL_WIDTH
    code = x_ref[:, 2 * kw + 2 * vw:]
    pre = _dot_x3(code, uphi_ref[...], uplo_ref[...]) + gb_ref[...]
    la_s[...] = -(jnp.maximum(-pre, 0.0) + jnp.log(1.0 + jnp.exp(-jnp.abs(pre)))) * (1.0 / GLA_GATE_NORMALIZER)
    q_s[...] = x_ref[:, 0:kw] * (GLA_DK ** -0.5)
    k_s[...] = x_ref[:, kw:2 * kw]

    tr = lax.broadcasted_iota(jnp.int32, (c, c), 0)
    tc = lax.broadcasted_iota(jnp.int32, (c, c), 1)
    tri = (tc <= tr).astype(BF16)
    srow = lax.broadcasted_iota(jnp.int32, (heads * c, c), 0)
    scol = lax.broadcasted_iota(jnp.int32, (heads * c, c), 1)
    causal = scol <= (srow % c)
    klane = lax.broadcasted_iota(jnp.int32, (1, kw), 1) // GLA_DK
    vlane = lax.broadcasted_iota(jnp.int32, (1, vw), 1) // HEAD_DIM
    st_r = lax.broadcasted_iota(jnp.int32, (vw, kw), 0) // HEAD_DIM
    st_c = lax.broadcasted_iota(jnp.int32, (vw, kw), 1) // GLA_DK
    st_mask = st_r == st_c

    n_chunk = tb // c
    pre = []
    for ci in range(n_chunk):
        rows = slice(ci * c, (ci + 1) * c)
        bc = _dot_exact_lhs(tri, la_s[rows, :])
        e_pos = jnp.exp(bc)
        qe = q_s[rows, :] * e_pos
        kc = k_s[rows, :]
        ke = (kc * jnp.exp(-bc)).astype(BF16)
        k2 = (kc * jnp.exp(bc[c - 1:c, :] - bc)).astype(BF16)
        vb = x_ref[rows, 2 * kw:2 * kw + vw].astype(BF16)
        qst = jnp.concatenate([jnp.where(klane == h, qe, 0.0) for h in range(heads)], axis=0).astype(BF16)
        pre.append((qe.astype(BF16), ke, k2, vb, qst, e_pos[c - 1:c, :]))
    scores = [jnp.where(causal, _dot_nt(t[4], t[1]), 0.0).astype(BF16) for t in pre]
    o_sts = [_dot(sc, t[3]) for sc, t in zip(scores, pre)]
    incs = [jnp.where(st_mask, _dot_tn(t[3], t[2]), 0.0) for t in pre]
    state = state_s[...]
    for ci in range(n_chunk):
        o = _dot_nt(pre[ci][0], state.astype(BF16))
        for h in range(heads):
            o = o + jnp.where(vlane == h, o_sts[ci][h * c:(h + 1) * c, :], 0.0)
        o_s[ci * c:(ci + 1) * c, :] = o
        state = state * pre[ci][5] + incs[ci]
    state_s[...] = state

    lane = lax.broadcasted_iota(jnp.int32, (1, LANES), 1)
    low_half = lane < HEAD_DIM
    outs = []
    for p in range(vw // LANES):
        o = o_s[:, p * LANES:(p + 1) * LANES]
        ms = _pair_head_sum(o * o, low_half) * (1.0 / HEAD_DIM)
        outs.append(o * lax.rsqrt(ms + EPS))
    g = x_ref[:, 2 * kw + vw:2 * kw + 2 * vw]
    on = jnp.concatenate(outs, axis=1) * ng_ref[...]
    o_ref[...] = (on * (g * jax.nn.sigmoid(g))).astype(o_ref.dtype)


def _gla(gl, prm, batch, seq):
    n = gl.shape[0]
    nb = seq // GLA_BLOCK
    return pl.pallas_call(
        _gla_kernel,
        grid=(batch, nb),
        in_specs=[pl.BlockSpec((GLA_BLOCK, GLA_PAD_WIDTH), lambda b, i: (b * nb + i, 0)),
                  pl.BlockSpec((1, GLA_KEY_WIDTH), lambda b, i: (0, 0)),
                  pl.BlockSpec((1, GLA_VAL_WIDTH), lambda b, i: (0, 0)),
                  pl.BlockSpec((LANES, GLA_KEY_WIDTH), lambda b, i: (0, 0)),
                  pl.BlockSpec((LANES, GLA_KEY_WIDTH), lambda b, i: (0, 0))],
        out_specs=pl.BlockSpec((GLA_BLOCK, GLA_VAL_WIDTH), lambda b, i: (b * nb + i, 0)),
        out_shape=jax.ShapeDtypeStruct((n, GLA_VAL_WIDTH), BF16),
        scratch_shapes=[pltpu.VMEM((GLA_VAL_WIDTH, GLA_KEY_WIDTH), F32),
                        pltpu.VMEM((GLA_BLOCK, GLA_KEY_WIDTH), F32),
                        pltpu.VMEM((GLA_BLOCK, GLA_KEY_WIDTH), F32),
                        pltpu.VMEM((GLA_BLOCK, GLA_KEY_WIDTH), F32),
                        pltpu.VMEM((GLA_BLOCK, GLA_VAL_WIDTH), F32)],
        compiler_params=_cparams(("parallel", "arbitrary")),
        name="gla",
    )(gl, prm["gate_b"], prm["norm_g"], prm["up_hi"], prm["up_lo"])


def _outproj_kernel(x_ref, sb_ref, rw_ref, gl_ref, wsb_ref, wrw_ref, wgl_ref, g_ref, o_ref):
    mixed = _dot(sb_ref[...], wsb_ref[...]) + _dot(rw_ref[...], wrw_ref[...]) + _dot(gl_ref[...], wgl_ref[...])
    o_ref[...] = x_ref[...] + _rms(mixed, g_ref[...])


def _outproj(x2, o_sb, o_rw, o_gl, w_sb, w_rw, w_gl, gain, tm=512):
    n = x2.shape[0]
    full = lambda w: pl.BlockSpec(w.shape, lambda i: (0, 0))
    row = lambda width: pl.BlockSpec((tm, width), lambda i: (i, 0))
    return pl.pallas_call(
        _outproj_kernel,
        grid=(n // tm,),
        in_specs=[row(D_MODEL), row(SB_WIDTH), row(RW_WIDTH), row(GLA_VAL_WIDTH),
                  full(w_sb), full(w_rw), full(w_gl), full(gain)],
        out_specs=row(D_MODEL),
        out_shape=jax.ShapeDtypeStruct((n, D_MODEL), F32),
        compiler_params=_cparams(("parallel",)),
        name="out_proj",
    )(x2, o_sb, o_rw, o_gl, w_sb, w_rw, w_gl, gain)


def _ffn_kernel(x_ref, g1_ref, w1_ref, w2_ref, g2_ref, o_ref, h_s, acc_s):
    j = pl.program_id(1)

    @pl.when(j == 0)
    def _():
        h_s[...] = _rms(x_ref[...], g1_ref[...]).astype(BF16)
        acc_s[...] = jnp.zeros_like(acc_s)

    a = jnp.maximum(_dot(h_s[...], w1_ref[...]), 0.0)
    acc_s[...] += _dot((a * a).astype(BF16), w2_ref[...])

    @pl.when(j == pl.num_programs(1) - 1)
    def _():
        o_ref[...] = x_ref[...] + _rms(acc_s[...], g2_ref[...])


def _ffn(x2, g1, w1, w2, g2, tm=1024, tf=1024):
    n = x2.shape[0]
    return pl.pallas_call(
        _ffn_kernel,
        grid=(n // tm, D_FF // tf),
        in_specs=[pl.BlockSpec((tm, D_MODEL), lambda i, j: (i, 0)),
                  pl.BlockSpec((1, D_MODEL), lambda i, j: (0, 0)),
                  pl.BlockSpec((D_MODEL, tf), lambda i, j: (0, j)),
                  pl.BlockSpec((tf, D_MODEL), lambda i, j: (j, 0)),
                  pl.BlockSpec((1, D_MODEL), lambda i, j: (0, 0))],
        out_specs=pl.BlockSpec((tm, D_MODEL), lambda i, j: (i, 0)),
        out_shape=jax.ShapeDtypeStruct((n, D_MODEL), F32),
        scratch_shapes=[pltpu.VMEM((tm, D_MODEL), BF16), pltpu.VMEM((tm, D_MODEL), F32)],
        compiler_params=_cparams(("parallel", "arbitrary")),
        name="ffn",
    )(x2, g1, w1, w2, g2)


def _hi_lo(w):
    hi = w.astype(BF16)
    return hi, (w - hi.astype(F32)).astype(BF16)


def _prep_layer(l, w_in, rw_w_up, rw_a_up, rw_g_up, gla_gate_up, w_out, w_ff1, w_ff2):
    w = w_in[l]
    sb_end = 3 * SB_WIDTH
    rw_end = sb_end + RW_IN_WIDTH
    w_sb = w[:, :sb_end].astype(BF16)
    w_rw = w[:, sb_end:rw_end].astype(BF16)
    g0 = rw_end
    kw, vw, gr = GLA_KEY_WIDTH, GLA_VAL_WIDTH, GLA_GATE_RANK
    w_gl = jnp.concatenate([
        w[:, g0:g0 + 2 * kw + vw],
        w[:, g0 + 2 * kw + vw + gr:g0 + GLA_IN_WIDTH],
        w[:, g0 + 2 * kw + vw:g0 + 2 * kw + vw + gr],
        jnp.zeros((D_MODEL, GLA_PAD_WIDTH - GLA_IN_WIDTH), F32)], axis=1).astype(BF16)
    up = jnp.zeros((RW_CODE, 3 * RW_WIDTH), F32)
    up = up.at[0:32, 0:RW_WIDTH].set(rw_w_up[l])
    up = up.at[32:64, RW_WIDTH:2 * RW_WIDTH].set(rw_a_up[l])
    up = up.at[64:128, 2 * RW_WIDTH:].set(rw_g_up[l])
    rw_up_hi, rw_up_lo = _hi_lo(up)
    gup = jnp.zeros((LANES, GLA_KEY_WIDTH), F32).at[0:gr, :].set(gla_gate_up[l])
    gl_up_hi, gl_up_lo = _hi_lo(gup)
    wo = w_out[l].astype(BF16)
    return dict(w_sb=w_sb, w_rw=w_rw, w_gl=w_gl, rw_up_hi=rw_up_hi, rw_up_lo=rw_up_lo,
                gl_up_hi=gl_up_hi, gl_up_lo=gl_up_lo,
                wo_sb=wo[:SB_WIDTH], wo_rw=wo[SB_WIDTH:SB_WIDTH + RW_WIDTH], wo_gl=wo[SB_WIDTH + RW_WIDTH:],
                w1=w_ff1[l].astype(BF16), w2=w_ff2[l].astype(BF16))


def kernel(x, pre_mix_g, w_in, sb_norm_g, rw_mu, rw_w0, rw_w_up, rw_a0, rw_a_up, rw_g_up, rw_k_k, rw_k_a, rw_r_k, rw_gn_g, rw_gn_b, gla_gate_up, gla_gate_b, gla_norm_g, w_out, post_mix_g, pre_ffn_g, w_ff1, w_ff2, post_ffn_g):
    batch, seq, d = x.shape
    depth = w_in.shape[0]
    x2 = x.reshape(batch * seq, d)
    row = lambda t, l: t[l][None, :]
    for l in range(depth):
        wp = _prep_layer(l, w_in, rw_w_up, rw_a_up, rw_g_up, gla_gate_up, w_out, w_ff1, w_ff2)
        sb, rw, gl = _inproj(x2, row(pre_mix_g, l), wp["w_sb"], wp["w_rw"], wp["w_gl"])
        o_sb = _sb_attention(sb, row(sb_norm_g, l), batch, seq)
        o_rw = _rwkv7(rw, dict(mu=row(rw_mu, l), w0=row(rw_w0, l), a0=row(rw_a0, l), k_k=row(rw_k_k, l),
                               k_a=row(rw_k_a, l), r_k=row(rw_r_k, l), gn_g=row(rw_gn_g, l),
                               gn_b=row(rw_gn_b, l), up_hi=wp["rw_up_hi"], up_lo=wp["rw_up_lo"]), batch, seq)
        o_gl = _gla(gl, dict(gate_b=row(gla_gate_b, l), norm_g=row(gla_norm_g, l),
                             up_hi=wp["gl_up_hi"], up_lo=wp["gl_up_lo"]), batch, seq)
        x2 = _outproj(x2, o_sb, o_rw, o_gl, wp["wo_sb"], wp["wo_rw"], wp["wo_gl"], row(post_mix_g, l))
        x2 = _ffn(x2, row(pre_ffn_g, l), wp["w1"], wp["w2"], row(post_ffn_g, l))
    return x2.reshape(batch, seq, d)
```

```python
import functools

import jax
import jax.numpy as jnp
from jax import lax
from jax.experimental import pallas as pl
from jax.experimental.pallas import tpu as pltpu

F32 = jnp.float32
BF16 = jnp.bfloat16

D_MODEL = 1024
HEAD_DIM = 64
LANES = 128
SB_WIDTH = 384
RW_WIDTH = 384
RW_IN_WIDTH = 1280
RW_CODE = 128
GLA_KEY_WIDTH = 128
GLA_VAL_WIDTH = 256
GLA_DK = 32
GLA_GATE_RANK = 16
GLA_IN_WIDTH = 784
GLA_PAD_WIDTH = 896
GLA_GATE_NORMALIZER = 16.0
D_FF = 4096
EPS = 1e-6
RW_GN_EPS = 64e-5
CHUNK = 64
VMEM_LIMIT = 48 * 1024 * 1024

SB_BLOCK = 256
LOG2E = 1.4426950408889634
SB_DEAD = 110.0 * LOG2E


def _cparams(sem):
    return pltpu.CompilerParams(dimension_semantics=sem, vmem_limit_bytes=VMEM_LIMIT)


def _dot(a, b):
    return jnp.dot(a, b, preferred_element_type=F32)


def _dot_nt(a, b):
    return lax.dot_general(a, b, (((1,), (1,)), ((), ())), preferred_element_type=F32)


def _dot_tn(a, b):
    return lax.dot_general(a, b, (((0,), (0,)), ((), ())), preferred_element_type=F32)


def _split2(x):
    hi = x.astype(BF16)
    lo = (x - hi.astype(F32)).astype(BF16)
    return hi, lo


def _split3(x):
    hi = x.astype(BF16)
    r1 = x - hi.astype(F32)
    mid = r1.astype(BF16)
    lo = (r1 - mid.astype(F32)).astype(BF16)
    return hi, mid, lo


def _dot_x3(a, w_hi, w_lo):
    a_hi, a_lo = _split2(a)
    return _dot(a_hi, w_hi) + (_dot(a_lo, w_hi) + _dot(a_hi, w_lo))


def _dot_exact_lhs(m_bf16, x):
    hi, mid, lo = _split3(x)
    return _dot(m_bf16, hi) + (_dot(m_bf16, mid) + _dot(m_bf16, lo))


def _rms(x, gain):
    return x * lax.rsqrt(jnp.mean(x * x, axis=-1, keepdims=True) + EPS) * gain


def _pair_head_sum(x, low_half):
    s_all = jnp.sum(x, axis=-1, keepdims=True)
    s_low = jnp.sum(jnp.where(low_half, x, 0.0), axis=-1, keepdims=True)
    return jnp.where(low_half, s_low, s_all - s_low)


def _inproj_kernel(x_ref, g_ref, wsb_ref, wrw_ref, wgl_ref, sb_ref, rw_ref, gl_ref):
    h = _rms(x_ref[...], g_ref[...]).astype(BF16)
    col = lax.broadcasted_iota(jnp.int32, (1, 3 * SB_WIDTH), 1)
    qscale = jnp.where(col < SB_WIDTH, LOG2E * HEAD_DIM ** -0.5, 1.0)
    sb_ref[...] = (_dot(h, wsb_ref[...]) * qscale).astype(BF16)
    rw_ref[...] = _dot(h, wrw_ref[...])
    gl_ref[...] = _dot(h, wgl_ref[...])


def _inproj(x2, gain, w_sb, w_rw, w_gl, tm=512):
    n = x2.shape[0]
    full = lambda w: pl.BlockSpec(w.shape, lambda i: (0, 0))
    row = lambda width: pl.BlockSpec((tm, width), lambda i: (i, 0))
    return pl.pallas_call(
        _inproj_kernel,
        grid=(n // tm,),
        in_specs=[row(D_MODEL), full(gain), full(w_sb), full(w_rw), full(w_gl)],
        out_specs=[row(3 * SB_WIDTH), row(RW_IN_WIDTH), row(GLA_PAD_WIDTH)],
        out_shape=[jax.ShapeDtypeStruct((n, 3 * SB_WIDTH), BF16),
                   jax.ShapeDtypeStruct((n, RW_IN_WIDTH), F32),
                   jax.ShapeDtypeStruct((n, GLA_PAD_WIDTH), F32)],
        compiler_params=_cparams(("parallel",)),
        name="in_proj",
    )(x2, gain, w_sb, w_rw, w_gl)


def _sb_kernel(q_ref, k_ref, v_ref, g_ref, o_ref):
    qb = SB_BLOCK
    qi = pl.program_id(1)
    lane = lax.broadcasted_iota(jnp.int32, (1, LANES), 1)
    low_half = lane < HEAD_DIM
    row = lax.broadcasted_iota(jnp.int32, (qb, qb), 0)
    col = lax.broadcasted_iota(jnp.int32, (qb, qb), 1)
    causal = col < row
    suffix = (row > col).astype(BF16)
    half = qb // 2
    pairs = SB_WIDTH // LANES
    zero = jnp.zeros((), BF16)
    vmasks = (low_half, jnp.logical_not(low_half))
    chains = [(h, r) for r in range(2) for h in range(2)]
    lanes_of = [slice(pr * LANES, (pr + 1) * LANES) for pr in range(pairs)]
    qcs = []
    for sl in lanes_of:
        q = q_ref[:, sl]
        qhs = [jnp.where(m, q, zero) for m in vmasks]
        qcs.append([qhs[h][r * half:(r + 1) * half, :] for h, r in chains])

    def front(qc, kb, n, mask):
        z = _dot_nt(qc, kb[0:n, :])
        p = jnp.maximum(z, 0.0) + jnp.log2(1.0 + jnp.exp2(-jnp.abs(z)))
        if mask is not None:
            p = jnp.where(mask, p, 0.0)
        later = _dot(p.astype(BF16), suffix[0:n, 0:n])
        return z, p, later

    def back(z, p, later, spent, mask):
        w = jnp.exp2((z - p) - (later + spent))
        if mask is not None:
            w = jnp.where(mask, w, 0.0)
        return w.astype(BF16), spent + (later[:, 0:1] + p[:, 0:1])

    def values(vb, n):
        return [jnp.where(m, vb[0:n, :], zero) for m in vmasks]

    d_start = pl.multiple_of(qi * qb, qb)
    p_start = pl.multiple_of(jnp.maximum(qi - 1, 0) * qb, qb)
    d_keys = [half if r == 0 else qb for _, r in chains]
    d_masks = [causal[0:half, 0:half] if r == 0 else causal[half:, :] for _, r in chains]
    kds = [k_ref[pl.ds(d_start, qb), sl] for sl in lanes_of]
    kps = [k_ref[pl.ds(p_start, qb), sl] for sl in lanes_of]
    fd = [[front(qc, kd, n, m) for qc, n, m in zip(qcp, d_keys, d_masks)] for qcp, kd in zip(qcs, kds)]
    fp = [[front(qc, kp, qb, None) for qc in qcp] for qcp, kp in zip(qcs, kps)]
    bd = [[back(*f, jnp.zeros((half, 1), F32), m) for f, m in zip(fdp, d_masks)] for fdp in fd]
    bp = [[back(*f, b[1], None) for f, b in zip(fpp, bdp)] for fpp, bdp in zip(fp, bd)]
    accs_all = []
    for pr, sl in enumerate(lanes_of):
        vd = v_ref[pl.ds(d_start, qb), sl]
        vp = jnp.where(qi > 0, v_ref[pl.ds(p_start, qb), sl], zero)
        accs = []
        for r in range(2):
            n = d_keys[2 * r]
            wcat = jnp.concatenate([bd[pr][2 * r][0], bd[pr][2 * r + 1][0],
                                    bp[pr][2 * r][0], bp[pr][2 * r + 1][0]], axis=1)
            vcat = jnp.concatenate(values(vd, n) + values(vp, qb), axis=0)
            accs.append(_dot(wcat, vcat))
        accs_all.append(accs)

    outs = []
    for pr, sl in enumerate(lanes_of):
        accs = accs_all[pr]

        def cond(s):
            j, spents, _ = s
            alive = jnp.minimum(jnp.minimum(jnp.min(spents[0]), jnp.min(spents[1])),
                                jnp.minimum(jnp.min(spents[2]), jnp.min(spents[3])))
            return jnp.logical_and(j >= 0, alive < SB_DEAD)

        def body(s, pr=pr, sl=sl):
            j, spents, accs = s
            start = pl.multiple_of(j * qb, qb)
            kb, vb = k_ref[pl.ds(start, qb), sl], v_ref[pl.ds(start, qb), sl]
            fs = [front(qc, kb, qb, None) for qc in qcs[pr]]
            bs = [back(*f, sp, None) for f, sp in zip(fs, spents)]
            vh = jnp.concatenate(values(vb, qb), axis=0)
            accs = [acc + _dot(jnp.concatenate([bs[2 * r][0], bs[2 * r + 1][0]], axis=1), vh)
                    for r, acc in enumerate(accs)]
            return j - 1, [b[1] for b in bs], accs

        _, _, accs = lax.while_loop(cond, body, (qi - 2, [b[1] for b in bp[pr]], accs))
        acc = jnp.concatenate(accs, axis=0)
        ms = _pair_head_sum(acc * acc, low_half) * (1.0 / HEAD_DIM)
        outs.append(acc * lax.rsqrt(ms + EPS))
    o_ref[...] = (jnp.concatenate(outs, axis=1) * g_ref[...]).astype(o_ref.dtype)


def _sb_attention(sb, gain, batch, seq):
    nq = seq // SB_BLOCK
    n = sb.shape[0]
    pairs = SB_WIDTH // LANES
    return pl.pallas_call(
        _sb_kernel,
        grid=(batch, nq),
        in_specs=[
            pl.BlockSpec((SB_BLOCK, SB_WIDTH), lambda b, i: (b * nq + i, 0)),
            pl.BlockSpec((seq, SB_WIDTH), lambda b, i: (b, 1)),
            pl.BlockSpec((seq, SB_WIDTH), lambda b, i: (b, 2)),
            pl.BlockSpec((1, SB_WIDTH), lambda b, i: (0, 0)),
        ],
        out_specs=pl.BlockSpec((SB_BLOCK, SB_WIDTH), lambda b, i: (b * nq + i, 0)),
        out_shape=jax.ShapeDtypeStruct((n, SB_WIDTH), BF16),
        compiler_params=_cparams(("parallel", "arbitrary")),
        name="sb_attention",
    )(sb, sb, sb, gain)


RW_BLOCK = 256
RW_GROUP = 2


def _rw_kernel(x_ref, mu_ref, w0_ref, a0_ref, kk_ref, ka_ref, rk_ref, gng_ref, gnb_ref,
               uphi_ref, uplo_ref, o_ref,
               state_s, prev_s, r_s, k_s, v_s, lw_s, a_s, b_s, y_s, lc_s,
               rp_s, yp_s, p_s, q_s, ge_s):
    tb = RW_BLOCK
    c = CHUNK
    pairs = RW_WIDTH // LANES

    @pl.when(pl.program_id(1) == 0)
    def _():
        state_s[...] = jnp.zeros_like(state_s)
        prev_s[...] = jnp.zeros_like(prev_s)

    lane = lax.broadcasted_iota(jnp.int32, (1, LANES), 1)
    low_half = lane < HEAD_DIM

    x = x_ref[...]
    first = lax.broadcasted_iota(jnp.int32, (tb, 1), 0) == 0
    prev = jnp.where(first, prev_s[...], pltpu.roll(x, 1, 0))
    prev_s[...] = x[tb - 1:tb, :]
    xs = x + (prev - x) * mu_ref[...]

    r = xs[:, 0:RW_WIDTH]
    k = xs[:, RW_WIDTH:2 * RW_WIDTH]
    v = xs[:, 2 * RW_WIDTH:3 * RW_WIDTH]
    code = xs[:, 3 * RW_WIDTH:]
    act = jnp.where(lane < 32, jnp.tanh(code), jnp.where(lane < 64, code, jax.nn.sigmoid(code)))
    up = _dot_x3(act, uphi_ref[...], uplo_ref[...])
    wpre = w0_ref[...] + up[:, 0:RW_WIDTH]
    log_w = -(jnp.maximum(-wpre, 0.0) + jnp.log(1.0 + jnp.exp(-jnp.abs(wpre)))) - 0.5
    lw = -jnp.exp(log_w)
    a = jax.nn.sigmoid(a0_ref[...] + up[:, RW_WIDTH:2 * RW_WIDTH])
    gate = up[:, 2 * RW_WIDTH:]
    kmod = k * (1.0 + (a - 1.0) * ka_ref[...])
    kk = k * kk_ref[...]
    bonus_in = r * kmod * rk_ref[...]
    kkn, bon = [], []
    for p in range(pairs):
        sl = slice(p * LANES, (p + 1) * LANES)
        ss = _pair_head_sum(kk[:, sl] * kk[:, sl], low_half)
        kkn.append(kk[:, sl] * lax.rsqrt(jnp.maximum(ss, 1e-12)))
        bon.append(_pair_head_sum(bonus_in[:, sl], low_half))
    kkn = jnp.concatenate(kkn, axis=1)
    bonus = jnp.concatenate(bon, axis=1) * v
    r_s[...] = r
    k_s[...] = kmod
    v_s[...] = v
    lw_s[...] = lw
    a_s[...] = -kkn
    b_s[...] = kkn * a
    tr = lax.broadcasted_iota(jnp.int32, (tb, tb), 0)
    tc = lax.broadcasted_iota(jnp.int32, (tb, tb), 1)
    tri = jnp.logical_and(tr // c == tc // c, tc <= tr).astype(BF16)
    lc_s[...] = _dot_exact_lhs(tri, lw)

    rr = lax.broadcasted_iota(jnp.int32, (2 * c, 2 * c), 0)
    cc = lax.broadcasted_iota(jnp.int32, (2 * c, 2 * c), 1)
    same_head = (rr // c) == (cc // c)
    strict = jnp.logical_and(same_head, cc < rr)
    incl = jnp.logical_and(same_head, cc <= rr)
    m_lo = low_half
    m_hi = jnp.logical_not(low_half)

    def stack(xp):
        return jnp.concatenate([jnp.where(m_lo, xp, 0.0), jnp.where(m_hi, xp, 0.0)], axis=0)

    def prepare(gi, carry):
        chains = [(gi * RW_GROUP + cj, p) for cj in range(RW_GROUP) for p in range(pairs)]
        pre = []
        for ci, p in chains:
            rows = pl.ds(pl.multiple_of(ci * c, c), c)
            sl = slice(p * LANES, (p + 1) * LANES)
            lwc = lw_s[rows, sl]
            lc = lc_s[rows, sl]
            e_incl = jnp.exp(lc)
            e_excl = jnp.exp(lc - lwc)
            e_inv = jnp.exp(-lc)
            g_end = e_incl[c - 1:c, :]
            ag = stack(a_s[rows, sl] * e_excl).astype(BF16)
            rg = stack(r_s[rows, sl] * e_incl).astype(BF16)
            bd = b_s[rows, sl] * e_inv
            kd = k_s[rows, sl] * e_inv
            vst = stack(v_s[rows, sl]).astype(BF16)
            lhs = jnp.concatenate([ag, rg], axis=0)
            rhs = jnp.concatenate([bd, bd, kd, kd], axis=0).astype(BF16)
            bde = stack(bd * g_end).astype(BF16)
            kde = stack(kd * g_end).astype(BF16)
            idx = ci * pairs + p
            ge_s[idx] = g_end
            pre.append((idx, ag, rg, vst, lhs, rhs, kde, bde))
        gs = [_dot_nt(t[4], t[5]) for t in pre]
        npows, xs_, avs, arbs = [], [], [], []
        for t, g in zip(pre, gs):
            a_ab = jnp.where(strict, g[0:2 * c, 0:2 * c], 0.0)
            a_ak = jnp.where(strict, g[0:2 * c, 2 * c:], 0.0)
            a_rb = jnp.where(incl, g[2 * c:, 0:2 * c], 0.0)
            a_rk = jnp.where(incl, g[2 * c:, 2 * c:], 0.0)
            arbs.append(a_rb.astype(BF16))
            npows.append(a_ab)
            avs.append(jnp.concatenate([a_ak, a_rk], axis=0).astype(BF16))
        avs = [_dot(av, t[3]) for av, t in zip(avs, pre)]
        svs = [_dot_tn(t[3], t[6]) for t in pre]
        for t, av in zip(pre, avs):
            xs_.append(jnp.concatenate([t[1].astype(F32), av[0:2 * c, :]], axis=1))
        for it in range(6):
            nbs = [n.astype(BF16) for n in npows]
            xbs = [xc.astype(BF16) for xc in xs_]
            if it < 5:
                prods = [_dot(nb, jnp.concatenate([nb, xb], axis=1)) for nb, xb in zip(nbs, xbs)]
                npows = [pr[:, 0:LANES] for pr in prods]
                xs_ = [xc + pr[:, LANES:] for xc, pr in zip(xs_, prods)]
            else:
                xs_ = [xc + _dot(nb, xb) for xc, nb, xb in zip(xs_, nbs, xbs)]
        xbs = [xc.astype(BF16) for xc in xs_]
        ras = [_dot(arb, xb) for arb, xb in zip(arbs, xbs)]
        pqs = [_dot_tn(xb, t[7]) for xb, t in zip(xbs, pre)]
        for t, av, sv, ra, pq in zip(pre, avs, svs, ras, pqs):
            idx = t[0]
            rp_s[idx] = (t[2].astype(F32) + ra[:, 0:LANES]).astype(BF16)
            yp_s[idx] = ra[:, LANES:] + av[2 * c:, :]
            p_s[idx] = pq[0:LANES, :].astype(BF16)
            q_s[idx] = pq[LANES:, :] + sv
        return carry

    lax.fori_loop(0, tb // (c * RW_GROUP), prepare, 0)

    states = [state_s[p] for p in range(pairs)]
    for ci in range(tb // c):
        for p in range(pairs):
            idx = ci * pairs + p
            sb = states[p].astype(BF16)
            y = _dot_nt(rp_s[idx], sb) + yp_s[idx]
            y_s[ci * c:(ci + 1) * c, p * LANES:(p + 1) * LANES] = y[0:c, :] + y[c:2 * c, :]
            states[p] = states[p] * ge_s[idx] + _dot(sb, p_s[idx]) + q_s[idx]
    for p in range(pairs):
        state_s[p] = states[p]

    outs = []
    for p in range(pairs):
        sl = slice(p * LANES, (p + 1) * LANES)
        y = y_s[:, sl]
        mean = _pair_head_sum(y, low_half) * (1.0 / HEAD_DIM)
        d = y - mean
        var = _pair_head_sum(d * d, low_half) * (1.0 / HEAD_DIM)
        outs.append(d * lax.rsqrt(var + RW_GN_EPS))
    yn = jnp.concatenate(outs, axis=1) * gng_ref[...] + gnb_ref[...]
    o_ref[...] = ((yn + bonus) * gate).astype(o_ref.dtype)


def _rwkv7(rw, prm, batch, seq):
    n = rw.shape[0]
    nb = seq // RW_BLOCK
    vec = pl.BlockSpec((1, RW_WIDTH), lambda b, i: (0, 0))
    up = pl.BlockSpec((RW_CODE, 3 * RW_WIDTH), lambda b, i: (0, 0))
    blk = lambda: pltpu.VMEM((RW_BLOCK, RW_WIDTH), F32)
    n_chain = (RW_BLOCK // CHUNK) * (RW_WIDTH // LANES)
    tile = lambda rows, dt: pltpu.VMEM((n_chain, rows, LANES), dt)
    return pl.pallas_call(
        _rw_kernel,
        grid=(batch, nb),
        in_specs=[pl.BlockSpec((RW_BLOCK, RW_IN_WIDTH), lambda b, i: (b * nb + i, 0)),
                  pl.BlockSpec((1, RW_IN_WIDTH), lambda b, i: (0, 0)),
                  vec, vec, vec, vec, vec, vec, vec, up, up],
        out_specs=pl.BlockSpec((RW_BLOCK, RW_WIDTH), lambda b, i: (b * nb + i, 0)),
        out_shape=jax.ShapeDtypeStruct((n, RW_WIDTH), BF16),
        scratch_shapes=[pltpu.VMEM((RW_WIDTH // LANES, LANES, LANES), F32),
                        pltpu.VMEM((1, RW_IN_WIDTH), F32),
                        blk(), blk(), blk(), blk(), blk(), blk(), blk(), blk(),
                        tile(LANES, BF16), tile(LANES, F32), tile(LANES, BF16), tile(LANES, F32),
                        tile(1, F32)],
        compiler_params=_cparams(("parallel", "arbitrary")),
        name="rwkv7",
    )(rw, prm["mu"], prm["w0"], prm["a0"], prm["k_k"], prm["k_a"], prm["r_k"],
      prm["gn_g"], prm["gn_b"], prm["up_hi"], prm["up_lo"])


GLA_BLOCK = 256
GLA_SUB = 16
GLA_EXP_CAP = 80.0


def _gla_kernel(x_ref, gb_ref, ng_ref, uphi_ref, uplo_ref, o_ref, state_s, q_s, k_s, la_s, o_s):
    tb = GLA_BLOCK
    c = CHUNK
    heads = GLA_VAL_WIDTH // HEAD_DIM

    @pl.when(pl.program_id(1) == 0)
    def _():
        state_s[...] = jnp.zeros_like(state_s)

    kw, vw = GLA_KEY_WIDTH, GLA_VAL_WIDTH
    code = x_ref[:, 2 * kw + 2 * vw:]
    pre = _dot_x3(code, uphi_ref[...], uplo_ref[...]) + gb_ref[...]
    la_s[...] = -(jnp.maximum(-pre, 0.0) + jnp.log(1.0 + jnp.exp(-jnp.abs(pre)))) * (1.0 / GLA_GATE_NORMALIZER)
    q_s[...] = x_ref[:, 0:kw] * (GLA_DK ** -0.5)
    k_s[...] = x_ref[:, kw:2 * kw]

    tr = lax.broadcasted_iota(jnp.int32, (c, c), 0)
    tc = lax.broadcasted_iota(jnp.int32, (c, c), 1)
    tri = (tc <= tr).astype(BF16)
    sub = GLA_SUB
    n_sub = c // sub
    srow = lax.broadcasted_iota(jnp.int32, (heads * c, c), 0)
    scol = lax.broadcasted_iota(jnp.int32, (heads * c, c), 1)
    causal = scol <= (srow // (heads * sub)) * sub + srow % sub
    klane = lax.broadcasted_iota(jnp.int32, (1, kw), 1) // GLA_DK
    vlane = lax.broadcasted_iota(jnp.int32, (1, vw), 1) // HEAD_DIM
    st_r = lax.broadcasted_iota(jnp.int32, (vw, kw), 0) // HEAD_DIM
    st_c = lax.broadcasted_iota(jnp.int32, (vw, kw), 1) // GLA_DK
    st_mask = st_r == st_c

    n_chunk = tb // c
    pre = []
    for ci in range(n_chunk):
        rows = slice(ci * c, (ci + 1) * c)
        bc = _dot_exact_lhs(tri, la_s[rows, :])
        e_pos = jnp.exp(bc)
        qc = q_s[rows, :]
        kc = k_s[rows, :]
        k2 = (kc * jnp.exp(bc[c - 1:c, :] - bc)).astype(BF16)
        vb = x_ref[rows, 2 * kw:2 * kw + vw].astype(BF16)
        qsts, kes = [], []
        for i in range(n_sub):
            srows = slice(i * sub, (i + 1) * sub)
            ref = bc[i * sub - 1:i * sub, :] if i > 0 else jnp.zeros((1, kw), F32)
            qi = qc[srows, :] * jnp.exp(bc[srows, :] - ref)
            qsts.append(jnp.concatenate([jnp.where(klane == h, qi, 0.0) for h in range(heads)],
                                        axis=0).astype(BF16))
            kes.append((kc * jnp.exp(jnp.minimum(ref - bc, GLA_EXP_CAP))).astype(BF16))
        pre.append(((qc * e_pos).astype(BF16), k2, vb, qsts, kes, e_pos[c - 1:c, :]))
    scores = [jnp.where(causal, jnp.concatenate([_dot_nt(qs, ke) for qs, ke in zip(t[3], t[4])], axis=0),
                        0.0).astype(BF16) for t in pre]
    o_sts = [_dot(sc, t[2]) for sc, t in zip(scores, pre)]
    incs = [jnp.where(st_mask, _dot_tn(t[2], t[1]), 0.0) for t in pre]
    state = state_s[...]
    for ci in range(n_chunk):
        intra = []
        for i in range(n_sub):
            acc = None
            for h in range(heads):
                r0 = (i * heads + h) * sub
                part = jnp.where(vlane == h, o_sts[ci][r0:r0 + sub, :], 0.0)
                acc = part if acc is None else acc + part
            intra.append(acc)
        o_s[ci * c:(ci + 1) * c, :] = _dot_nt(pre[ci][0], state.astype(BF16)) + jnp.concatenate(intra, axis=0)
        state = state * pre[ci][5] + incs[ci]
    state_s[...] = state

    lane = lax.broadcasted_iota(jnp.int32, (1, LANES), 1)
    low_half = lane < HEAD_DIM
    outs = []
    for p in range(vw // LANES):
        o = o_s[:, p * LANES:(p + 1) * LANES]
        ms = _pair_head_sum(o * o, low_half) * (1.0 / HEAD_DIM)
        outs.append(o * lax.rsqrt(ms + EPS))
    g = x_ref[:, 2 * kw + vw:2 * kw + 2 * vw]
    on = jnp.concatenate(outs, axis=1) * ng_ref[...]
    o_ref[...] = (on * (g * jax.nn.sigmoid(g))).astype(o_ref.dtype)


def _gla(gl, prm, batch, seq):
    n = gl.shape[0]
    nb = seq // GLA_BLOCK
    return pl.pallas_call(
        _gla_kernel,
        grid=(batch, nb),
        in_specs=[pl.BlockSpec((GLA_BLOCK, GLA_PAD_WIDTH), lambda b, i: (b * nb + i, 0)),
                  pl.BlockSpec((1, GLA_KEY_WIDTH), lambda b, i: (0, 0)),
                  pl.BlockSpec((1, GLA_VAL_WIDTH), lambda b, i: (0, 0)),
                  pl.BlockSpec((LANES, GLA_KEY_WIDTH), lambda b, i: (0, 0)),
                  pl.BlockSpec((LANES, GLA_KEY_WIDTH), lambda b, i: (0, 0))],
        out_specs=pl.BlockSpec((GLA_BLOCK, GLA_VAL_WIDTH), lambda b, i: (b * nb + i, 0)),
        out_shape=jax.ShapeDtypeStruct((n, GLA_VAL_WIDTH), BF16),
        scratch_shapes=[pltpu.VMEM((GLA_VAL_WIDTH, GLA_KEY_WIDTH), F32),
                        pltpu.VMEM((GLA_BLOCK, GLA_KEY_WIDTH), F32),
                        pltpu.VMEM((GLA_BLOCK, GLA_KEY_WIDTH), F32),
                        pltpu.VMEM((GLA_BLOCK, GLA_KEY_WIDTH), F32),
                        pltpu.VMEM((GLA_BLOCK, GLA_VAL_WIDTH), F32)],
        compiler_params=_cparams(("parallel", "arbitrary")),
        name="gla",
    )(gl, prm["gate_b"], prm["norm_g"], prm["up_hi"], prm["up_lo"])


MLP_ROWS = 512
MLP_FF_TILE = 1024


def _mlp_kernel(x_ref, sb_ref, rw_ref, gl_ref, wsb_ref, wrw_ref, wgl_ref, gmix_ref,
                gpre_ref, w1_ref, w2_ref, gpost_ref, o_ref):
    mixed = _dot(sb_ref[...], wsb_ref[...]) + _dot(rw_ref[...], wrw_ref[...]) + _dot(gl_ref[...], wgl_ref[...])
    x1 = x_ref[...] + _rms(mixed, gmix_ref[...])
    h = _rms(x1, gpre_ref[...]).astype(BF16)
    ff = None
    for j in range(D_FF // MLP_FF_TILE):
        cols = slice(j * MLP_FF_TILE, (j + 1) * MLP_FF_TILE)
        a = jnp.maximum(_dot(h, w1_ref[:, cols]), 0.0)
        part = _dot((a * a).astype(BF16), w2_ref[cols, :])
        ff = part if ff is None else ff + part
    o_ref[...] = x1 + _rms(ff, gpost_ref[...])


def _mlp(x2, o_sb, o_rw, o_gl, w_sb, w_rw, w_gl, g_mix, g_pre, w1, w2, g_post):
    n = x2.shape[0]
    tm = MLP_ROWS
    full = lambda w: pl.BlockSpec(w.shape, lambda i: (0, 0), pipeline_mode=pl.Buffered(1))
    row = lambda width: pl.BlockSpec((tm, width), lambda i: (i, 0))
    return pl.pallas_call(
        _mlp_kernel,
        grid=(n // tm,),
        in_specs=[row(D_MODEL), row(SB_WIDTH), row(RW_WIDTH), row(GLA_VAL_WIDTH),
                  full(w_sb), full(w_rw), full(w_gl), full(g_mix),
                  full(g_pre), full(w1), full(w2), full(g_post)],
        out_specs=row(D_MODEL),
        out_shape=jax.ShapeDtypeStruct((n, D_MODEL), F32),
        compiler_params=_cparams(("parallel",)),
        name="out_proj_mlp",
    )(x2, o_sb, o_rw, o_gl, w_sb, w_rw, w_gl, g_mix, g_pre, w1, w2, g_post)


def _hi_lo(w):
    hi = w.astype(BF16)
    return hi, (w - hi.astype(F32)).astype(BF16)


def _prep_layer(l, w_in, rw_w_up, rw_a_up, rw_g_up, gla_gate_up, w_out, w_ff1, w_ff2):
    w = w_in[l]
    sb_end = 3 * SB_WIDTH
    rw_end = sb_end + RW_IN_WIDTH
    w_sb = w[:, :sb_end].astype(BF16)
    w_rw = w[:, sb_end:rw_end].astype(BF16)
    g0 = rw_end
    kw, vw, gr = GLA_KEY_WIDTH, GLA_VAL_WIDTH, GLA_GATE_RANK
    w_gl = jnp.concatenate([
        w[:, g0:g0 + 2 * kw + vw],
        w[:, g0 + 2 * kw + vw + gr:g0 + GLA_IN_WIDTH],
        w[:, g0 + 2 * kw + vw:g0 + 2 * kw + vw + gr],
        jnp.zeros((D_MODEL, GLA_PAD_WIDTH - GLA_IN_WIDTH), F32)], axis=1).astype(BF16)
    up = jnp.zeros((RW_CODE, 3 * RW_WIDTH), F32)
    up = up.at[0:32, 0:RW_WIDTH].set(rw_w_up[l])
    up = up.at[32:64, RW_WIDTH:2 * RW_WIDTH].set(rw_a_up[l])
    up = up.at[64:128, 2 * RW_WIDTH:].set(rw_g_up[l])
    rw_up_hi, rw_up_lo = _hi_lo(up)
    gup = jnp.zeros((LANES, GLA_KEY_WIDTH), F32).at[0:gr, :].set(gla_gate_up[l])
    gl_up_hi, gl_up_lo = _hi_lo(gup)
    wo = w_out[l].astype(BF16)
    return dict(w_sb=w_sb, w_rw=w_rw, w_gl=w_gl, rw_up_hi=rw_up_hi, rw_up_lo=rw_up_lo,
                gl_up_hi=gl_up_hi, gl_up_lo=gl_up_lo,
                wo_sb=wo[:SB_WIDTH], wo_rw=wo[SB_WIDTH:SB_WIDTH + RW_WIDTH], wo_gl=wo[SB_WIDTH + RW_WIDTH:],
                w1=w_ff1[l].astype(BF16), w2=w_ff2[l].astype(BF16))


def kernel(x, pre_mix_g, w_in, sb_norm_g, rw_mu, rw_w0, rw_w_up, rw_a0, rw_a_up, rw_g_up, rw_k_k, rw_k_a, rw_r_k, rw_gn_g, rw_gn_b, gla_gate_up, gla_gate_b, gla_norm_g, w_out, post_mix_g, pre_ffn_g, w_ff1, w_ff2, post_ffn_g):
    batch, seq, d = x.shape
    depth = w_in.shape[0]
    x2 = x.reshape(batch * seq, d)
    row = lambda t, l: t[l][None, :]
    for l in range(depth):
        wp = _prep_layer(l, w_in, rw_w_up, rw_a_up, rw_g_up, gla_gate_up, w_out, w_ff1, w_ff2)
        sb, rw, gl = _inproj(x2, row(pre_mix_g, l), wp["w_sb"], wp["w_rw"], wp["w_gl"])
        o_sb = _sb_attention(sb, row(sb_norm_g, l), batch, seq)
        o_rw = _rwkv7(rw, dict(mu=row(rw_mu, l), w0=row(rw_w0, l), a0=row(rw_a0, l), k_k=row(rw_k_k, l),
                               k_a=row(rw_k_a, l), r_k=row(rw_r_k, l), gn_g=row(rw_gn_g, l),
                               gn_b=row(rw_gn_b, l), up_hi=wp["rw_up_hi"], up_lo=wp["rw_up_lo"]), batch, seq)
        o_gl = _gla(gl, dict(gate_b=row(gla_gate_b, l), norm_g=row(gla_norm_g, l),
                             up_hi=wp["gl_up_hi"], up_lo=wp["gl_up_lo"]), batch, seq)
        x2 = _mlp(x2, o_sb, o_rw, o_gl, wp["wo_sb"], wp["wo_rw"], wp["wo_gl"], row(post_mix_g, l),
                  row(pre_ffn_g, l), wp["w1"], wp["w2"], row(post_ffn_g, l))
    return x2.reshape(batch, seq, d)
```

```python
import functools

import jax
import jax.numpy as jnp
from jax import lax
from jax.experimental import pallas as pl
from jax.experimental.pallas import tpu as pltpu

F32 = jnp.float32
BF16 = jnp.bfloat16

D_MODEL = 1024
HEAD_DIM = 64
LANES = 128
SB_WIDTH = 384
RW_WIDTH = 384
RW_IN_WIDTH = 1280
RW_CODE = 128
GLA_KEY_WIDTH = 128
GLA_VAL_WIDTH = 256
GLA_DK = 32
GLA_GATE_RANK = 16
GLA_IN_WIDTH = 784
GLA_PAD_WIDTH = 896
GLA_GATE_NORMALIZER = 16.0
D_FF = 4096
EPS = 1e-6
RW_GN_EPS = 64e-5
CHUNK = 64
VMEM_LIMIT = 58 * 1024 * 1024

SB_BLOCK = 256
LOG2E = 1.4426950408889634
SB_DEAD = 110.0 * LOG2E


def _cparams(sem):
    return pltpu.CompilerParams(dimension_semantics=sem, vmem_limit_bytes=VMEM_LIMIT)


def _dot(a, b):
    return jnp.dot(a, b, preferred_element_type=F32)


def _dot_nt(a, b):
    return lax.dot_general(a, b, (((1,), (1,)), ((), ())), preferred_element_type=F32)


def _dot_tn(a, b):
    return lax.dot_general(a, b, (((0,), (0,)), ((), ())), preferred_element_type=F32)


def _split2(x):
    hi = x.astype(BF16)
    lo = (x - hi.astype(F32)).astype(BF16)
    return hi, lo


def _split3(x):
    hi = x.astype(BF16)
    r1 = x - hi.astype(F32)
    mid = r1.astype(BF16)
    lo = (r1 - mid.astype(F32)).astype(BF16)
    return hi, mid, lo


def _dot_x3(a, w_hi, w_lo):
    a_hi, a_lo = _split2(a)
    return _dot(a_hi, w_hi) + (_dot(a_lo, w_hi) + _dot(a_hi, w_lo))


def _dot_exact_lhs(m_bf16, x):
    hi, mid, lo = _split3(x)
    return _dot(m_bf16, hi) + (_dot(m_bf16, mid) + _dot(m_bf16, lo))


def _rms(x, gain):
    return x * lax.rsqrt(jnp.mean(x * x, axis=-1, keepdims=True) + EPS) * gain


def _pair_head_sum(x, low_half):
    s_all = jnp.sum(x, axis=-1, keepdims=True)
    s_low = jnp.sum(jnp.where(low_half, x, 0.0), axis=-1, keepdims=True)
    return jnp.where(low_half, s_low, s_all - s_low)


def _inproj_kernel(x_ref, g_ref, wsb_ref, wrw_ref, wgl_ref, sb_ref, rw_ref, gl_ref):
    h = _rms(x_ref[...], g_ref[...]).astype(BF16)
    col = lax.broadcasted_iota(jnp.int32, (1, 3 * SB_WIDTH), 1)
    qscale = jnp.where(col < SB_WIDTH, LOG2E * HEAD_DIM ** -0.5, 1.0)
    sb_ref[...] = (_dot(h, wsb_ref[...]) * qscale).astype(BF16)
    rw_ref[...] = _dot(h, wrw_ref[...])
    gl_ref[...] = _dot(h, wgl_ref[...])


INPROJ_ROWS = 1024


def _inproj(x2, gain, w_sb, w_rw, w_gl, tm=INPROJ_ROWS):
    n = x2.shape[0]
    full = lambda w: pl.BlockSpec(w.shape, lambda i: (0, 0), pipeline_mode=pl.Buffered(1))
    row = lambda width: pl.BlockSpec((tm, width), lambda i: (i, 0))
    return pl.pallas_call(
        _inproj_kernel,
        grid=(n // tm,),
        in_specs=[row(D_MODEL), full(gain), full(w_sb), full(w_rw), full(w_gl)],
        out_specs=[row(3 * SB_WIDTH), row(RW_IN_WIDTH), row(GLA_PAD_WIDTH)],
        out_shape=[jax.ShapeDtypeStruct((n, 3 * SB_WIDTH), BF16),
                   jax.ShapeDtypeStruct((n, RW_IN_WIDTH), F32),
                   jax.ShapeDtypeStruct((n, GLA_PAD_WIDTH), F32)],
        compiler_params=_cparams(("parallel",)),
        name="in_proj",
    )(x2, gain, w_sb, w_rw, w_gl)


def _sb_kernel(q_ref, k_ref, v_ref, g_ref, o_ref):
    qb = SB_BLOCK
    qi = pl.program_id(1)
    lane = lax.broadcasted_iota(jnp.int32, (1, LANES), 1)
    low_half = lane < HEAD_DIM
    row = lax.broadcasted_iota(jnp.int32, (qb, qb), 0)
    col = lax.broadcasted_iota(jnp.int32, (qb, qb), 1)
    causal = col < row
    suffix = (row > col).astype(BF16)
    half = qb // 2
    pairs = SB_WIDTH // LANES
    zero = jnp.zeros((), BF16)
    vmasks = (low_half, jnp.logical_not(low_half))
    chains = [(h, r) for r in range(2) for h in range(2)]
    lanes_of = [slice(pr * LANES, (pr + 1) * LANES) for pr in range(pairs)]
    qcs = []
    for sl in lanes_of:
        q = q_ref[:, sl]
        qhs = [jnp.where(m, q, zero) for m in vmasks]
        qcs.append([qhs[h][r * half:(r + 1) * half, :] for h, r in chains])

    def front(qc, kb, n, mask):
        z = _dot_nt(qc, kb[0:n, :])
        p = jnp.maximum(z, 0.0) + jnp.log2(1.0 + jnp.exp2(-jnp.abs(z)))
        if mask is not None:
            p = jnp.where(mask, p, 0.0)
        later = _dot(p.astype(BF16), suffix[0:n, 0:n])
        return z, p, later

    def back(z, p, later, spent, mask):
        w = jnp.exp2((z - p) - (later + spent))
        if mask is not None:
            w = jnp.where(mask, w, 0.0)
        return w.astype(BF16), spent + (later[:, 0:1] + p[:, 0:1])

    def values(vb, n):
        return [jnp.where(m, vb[0:n, :], zero) for m in vmasks]

    d_start = pl.multiple_of(qi * qb, qb)
    p_start = pl.multiple_of(jnp.maximum(qi - 1, 0) * qb, qb)
    d_keys = [half if r == 0 else qb for _, r in chains]
    d_masks = [causal[0:half, 0:half] if r == 0 else causal[half:, :] for _, r in chains]
    kds = [k_ref[pl.ds(d_start, qb), sl] for sl in lanes_of]
    kps = [k_ref[pl.ds(p_start, qb), sl] for sl in lanes_of]
    fd = [[front(qc, kd, n, m) for qc, n, m in zip(qcp, d_keys, d_masks)] for qcp, kd in zip(qcs, kds)]
    fp = [[front(qc, kp, qb, None) for qc in qcp] for qcp, kp in zip(qcs, kps)]
    bd = [[back(*f, jnp.zeros((half, 1), F32), m) for f, m in zip(fdp, d_masks)] for fdp in fd]
    bp = [[back(*f, b[1], None) for f, b in zip(fpp, bdp)] for fpp, bdp in zip(fp, bd)]
    accs_all = []
    for pr, sl in enumerate(lanes_of):
        vd = v_ref[pl.ds(d_start, qb), sl]
        vp = jnp.where(qi > 0, v_ref[pl.ds(p_start, qb), sl], zero)
        accs = []
        for r in range(2):
            n = d_keys[2 * r]
            wcat = jnp.concatenate([bd[pr][2 * r][0], bd[pr][2 * r + 1][0],
                                    bp[pr][2 * r][0], bp[pr][2 * r + 1][0]], axis=1)
            vcat = jnp.concatenate(values(vd, n) + values(vp, qb), axis=0)
            accs.append(_dot(wcat, vcat))
        accs_all.append(accs)

    outs = []
    for pr, sl in enumerate(lanes_of):
        accs = accs_all[pr]

        def cond(s):
            j, spents, _ = s
            alive = jnp.minimum(jnp.minimum(jnp.min(spents[0]), jnp.min(spents[1])),
                                jnp.minimum(jnp.min(spents[2]), jnp.min(spents[3])))
            return jnp.logical_and(j >= 0, alive < SB_DEAD)

        def body(s, pr=pr, sl=sl):
            j, spents, accs = s
            start = pl.multiple_of(j * qb, qb)
            kb, vb = k_ref[pl.ds(start, qb), sl], v_ref[pl.ds(start, qb), sl]
            fs = [front(qc, kb, qb, None) for qc in qcs[pr]]
            bs = [back(*f, sp, None) for f, sp in zip(fs, spents)]
            vh = jnp.concatenate(values(vb, qb), axis=0)
            accs = [acc + _dot(jnp.concatenate([bs[2 * r][0], bs[2 * r + 1][0]], axis=1), vh)
                    for r, acc in enumerate(accs)]
            return j - 1, [b[1] for b in bs], accs

        _, _, accs = lax.while_loop(cond, body, (qi - 2, [b[1] for b in bp[pr]], accs))
        acc = jnp.concatenate(accs, axis=0)
        ms = _pair_head_sum(acc * acc, low_half) * (1.0 / HEAD_DIM)
        outs.append(acc * lax.rsqrt(ms + EPS))
    o_ref[...] = (jnp.concatenate(outs, axis=1) * g_ref[...]).astype(o_ref.dtype)


def _sb_attention(sb, gain, batch, seq):
    nq = seq // SB_BLOCK
    n = sb.shape[0]
    pairs = SB_WIDTH // LANES
    return pl.pallas_call(
        _sb_kernel,
        grid=(batch, nq),
        in_specs=[
            pl.BlockSpec((SB_BLOCK, SB_WIDTH), lambda b, i: (b * nq + i, 0)),
            pl.BlockSpec((seq, SB_WIDTH), lambda b, i: (b, 1)),
            pl.BlockSpec((seq, SB_WIDTH), lambda b, i: (b, 2)),
            pl.BlockSpec((1, SB_WIDTH), lambda b, i: (0, 0)),
        ],
        out_specs=pl.BlockSpec((SB_BLOCK, SB_WIDTH), lambda b, i: (b * nq + i, 0)),
        out_shape=jax.ShapeDtypeStruct((n, SB_WIDTH), BF16),
        compiler_params=_cparams(("parallel", "arbitrary")),
        name="sb_attention",
    )(sb, sb, sb, gain)


RW_BLOCK = 256
RW_GROUP = 4


def _rw_body(x_ref, mu_ref, w0_ref, a0_ref, kk_ref, ka_ref, rk_ref, gng_ref, gnb_ref,
             uphi_ref, uplo_ref, o_ref,
             state_s, prev_s, r_s, k_s, v_s, lw_s, a_s, b_s, y_s, lc_s,
             rp_s, yp_s, p_s, q_s, ge_s):
    tb = RW_BLOCK
    c = CHUNK
    pairs = RW_WIDTH // LANES

    lane = lax.broadcasted_iota(jnp.int32, (1, LANES), 1)
    low_half = lane < HEAD_DIM

    x = x_ref[...]
    first = lax.broadcasted_iota(jnp.int32, (tb, 1), 0) == 0
    prev = jnp.where(first, prev_s[...], pltpu.roll(x, 1, 0))
    prev_s[...] = x[tb - 1:tb, :]
    xs = x + (prev - x) * mu_ref[...]

    r = xs[:, 0:RW_WIDTH]
    k = xs[:, RW_WIDTH:2 * RW_WIDTH]
    v = xs[:, 2 * RW_WIDTH:3 * RW_WIDTH]
    code = xs[:, 3 * RW_WIDTH:]
    act = jnp.where(lane < 32, jnp.tanh(code), jnp.where(lane < 64, code, jax.nn.sigmoid(code)))
    up = jnp.concatenate(
        [_dot_x3(act, uphi_ref[:, 0:RW_WIDTH], uplo_ref[:, 0:RW_WIDTH]),
         _dot(act.astype(BF16), uphi_ref[:, RW_WIDTH:])], axis=1)
    wpre = w0_ref[...] + up[:, 0:RW_WIDTH]
    log_w = -(jnp.maximum(-wpre, 0.0) + jnp.log(1.0 + jnp.exp(-jnp.abs(wpre)))) - 0.5
    lw = -jnp.exp(log_w)
    a = jax.nn.sigmoid(a0_ref[...] + up[:, RW_WIDTH:2 * RW_WIDTH])
    gate = up[:, 2 * RW_WIDTH:]
    kmod = k * (1.0 + (a - 1.0) * ka_ref[...])
    kk = k * kk_ref[...]
    bonus_in = r * kmod * rk_ref[...]
    kkn, bon = [], []
    for p in range(pairs):
        sl = slice(p * LANES, (p + 1) * LANES)
        ss = _pair_head_sum(kk[:, sl] * kk[:, sl], low_half)
        kkn.append(kk[:, sl] * lax.rsqrt(jnp.maximum(ss, 1e-12)))
        bon.append(_pair_head_sum(bonus_in[:, sl], low_half))
    kkn = jnp.concatenate(kkn, axis=1)
    bonus = jnp.concatenate(bon, axis=1) * v
    r_s[...] = r
    k_s[...] = kmod
    v_s[...] = v
    lw_s[...] = lw
    a_s[...] = -kkn
    b_s[...] = kkn * a
    tr = lax.broadcasted_iota(jnp.int32, (tb, tb), 0)
    tc = lax.broadcasted_iota(jnp.int32, (tb, tb), 1)
    tri = jnp.logical_and(tr // c == tc // c, tc <= tr).astype(BF16)
    lc_s[...] = _dot_exact_lhs(tri, lw)

    rr = lax.broadcasted_iota(jnp.int32, (2 * c, 2 * c), 0)
    cc = lax.broadcasted_iota(jnp.int32, (2 * c, 2 * c), 1)
    same_head = (rr // c) == (cc // c)
    strict = jnp.logical_and(same_head, cc < rr)
    incl = jnp.logical_and(same_head, cc <= rr)
    m_lo = low_half
    m_hi = jnp.logical_not(low_half)

    def stack(xp):
        return jnp.concatenate([jnp.where(m_lo, xp, 0.0), jnp.where(m_hi, xp, 0.0)], axis=0)

    def prepare(gi, carry):
        chains = [(gi * RW_GROUP + cj, p) for cj in range(RW_GROUP) for p in range(pairs)]
        pre = []
        for ci, p in chains:
            rows = pl.ds(pl.multiple_of(ci * c, c), c)
            sl = slice(p * LANES, (p + 1) * LANES)
            lwc = lw_s[rows, sl]
            lc = lc_s[rows, sl]
            e_incl = jnp.exp(lc)
            e_excl = jnp.exp(lc - lwc)
            e_inv = jnp.exp(-lc)
            g_end = e_incl[c - 1:c, :]
            ag = stack(a_s[rows, sl] * e_excl).astype(BF16)
            rg = stack(r_s[rows, sl] * e_incl).astype(BF16)
            bd = b_s[rows, sl] * e_inv
            kd = k_s[rows, sl] * e_inv
            vst = stack(v_s[rows, sl]).astype(BF16)
            lhs = jnp.concatenate([ag, rg], axis=0)
            rhs = jnp.concatenate([bd, bd, kd, kd], axis=0).astype(BF16)
            bde = stack(bd * g_end).astype(BF16)
            kde = stack(kd * g_end).astype(BF16)
            idx = ci * pairs + p
            ge_s[idx] = g_end
            pre.append((idx, ag, rg, vst, lhs, rhs, kde, bde))
        gs = [_dot_nt(t[4], t[5]) for t in pre]
        npows, xs_, avs, arbs = [], [], [], []
        for t, g in zip(pre, gs):
            a_ab = jnp.where(strict, g[0:2 * c, 0:2 * c], 0.0)
            a_ak = jnp.where(strict, g[0:2 * c, 2 * c:], 0.0)
            a_rb = jnp.where(incl, g[2 * c:, 0:2 * c], 0.0)
            a_rk = jnp.where(incl, g[2 * c:, 2 * c:], 0.0)
            arbs.append(a_rb.astype(BF16))
            npows.append(a_ab)
            avs.append(jnp.concatenate([a_ak, a_rk], axis=0).astype(BF16))
        avs = [_dot(av, t[3]) for av, t in zip(avs, pre)]
        svs = [_dot_tn(t[3], t[6]) for t in pre]
        for t, av in zip(pre, avs):
            xs_.append(jnp.concatenate([t[1].astype(F32), av[0:2 * c, :]], axis=1))
        for it in range(6):
            nbs = [n.astype(BF16) for n in npows]
            xbs = [xc.astype(BF16) for xc in xs_]
            if it < 5:
                prods = [_dot(nb, jnp.concatenate([nb, xb], axis=1)) for nb, xb in zip(nbs, xbs)]
                npows = [pr[:, 0:LANES] for pr in prods]
                xs_ = [xc + pr[:, LANES:] for xc, pr in zip(xs_, prods)]
            else:
                xs_ = [xc + _dot(nb, xb) for xc, nb, xb in zip(xs_, nbs, xbs)]
        xbs = [xc.astype(BF16) for xc in xs_]
        ras = [_dot(arb, xb) for arb, xb in zip(arbs, xbs)]
        pqs = [_dot_tn(xb, t[7]) for xb, t in zip(xbs, pre)]
        for t, av, sv, ra, pq in zip(pre, avs, svs, ras, pqs):
            idx = t[0]
            rp_s[idx] = (t[2].astype(F32) + ra[:, 0:LANES]).astype(BF16)
            yp_s[idx] = ra[:, LANES:] + av[2 * c:, :]
            p_s[idx] = pq[0:LANES, :].astype(BF16)
            q_s[idx] = pq[LANES:, :] + sv
        return carry

    lax.fori_loop(0, tb // (c * RW_GROUP), prepare, 0)

    states = [state_s[p] for p in range(pairs)]
    for ci in range(tb // c):
        for p in range(pairs):
            idx = ci * pairs + p
            sb = states[p].astype(BF16)
            y = _dot_nt(rp_s[idx], sb) + yp_s[idx]
            y_s[ci * c:(ci + 1) * c, p * LANES:(p + 1) * LANES] = y[0:c, :] + y[c:2 * c, :]
            states[p] = states[p] * ge_s[idx] + _dot(sb, p_s[idx]) + q_s[idx]
    for p in range(pairs):
        state_s[p] = states[p]

    outs = []
    for p in range(pairs):
        sl = slice(p * LANES, (p + 1) * LANES)
        y = y_s[:, sl]
        mean = _pair_head_sum(y, low_half) * (1.0 / HEAD_DIM)
        d = y - mean
        var = _pair_head_sum(d * d, low_half) * (1.0 / HEAD_DIM)
        outs.append(d * lax.rsqrt(var + RW_GN_EPS))
    yn = jnp.concatenate(outs, axis=1) * gng_ref[...] + gnb_ref[...]
    o_ref[...] = ((yn + bonus) * gate).astype(o_ref.dtype)


RW_N_IN, RW_N_SCRATCH = 11, 15
GLA_N_IN, GLA_N_SCRATCH = 5, 5


def _recurrent_kernel(*refs):
    n_in = RW_N_IN + GLA_N_IN
    rw_in, gl_in = refs[:RW_N_IN], refs[RW_N_IN:n_in]
    o_rw, o_gl = refs[n_in], refs[n_in + 1]
    rw_scr = refs[n_in + 2:n_in + 2 + RW_N_SCRATCH]
    gl_scr = refs[n_in + 2 + RW_N_SCRATCH:]

    @pl.when(pl.program_id(1) == 0)
    def _():
        for state in (rw_scr[0], rw_scr[1], gl_scr[0]):
            state[...] = jnp.zeros_like(state)

    _rw_body(*rw_in, o_rw, *rw_scr)
    _gla_body(*gl_in, o_gl, *gl_scr)


def _recurrent_mixers(rw, gl, rprm, gprm, batch, seq):
    n = rw.shape[0]
    tb = RW_BLOCK
    nb = seq // tb
    const = lambda shape: pl.BlockSpec(shape, lambda b, i: (0,) * len(shape))
    rows = lambda width: pl.BlockSpec((tb, width), lambda b, i: (b * nb + i, 0))
    vec = const((1, RW_WIDTH))
    up = const((RW_CODE, 3 * RW_WIDTH))
    blk = lambda: pltpu.VMEM((tb, RW_WIDTH), F32)
    n_chain = (tb // CHUNK) * (RW_WIDTH // LANES)
    tile = lambda r, dt: pltpu.VMEM((n_chain, r, LANES), dt)
    rw_scratch = [pltpu.VMEM((RW_WIDTH // LANES, LANES, LANES), F32),
                  pltpu.VMEM((1, RW_IN_WIDTH), F32),
                  blk(), blk(), blk(), blk(), blk(), blk(), blk(), blk(),
                  tile(LANES, BF16), tile(LANES, F32), tile(LANES, BF16), tile(LANES, F32),
                  tile(1, F32)]
    gl_scratch = [pltpu.VMEM((GLA_VAL_WIDTH, GLA_KEY_WIDTH), F32),
                  pltpu.VMEM((tb, GLA_KEY_WIDTH), F32),
                  pltpu.VMEM((tb, GLA_KEY_WIDTH), F32),
                  pltpu.VMEM((tb, GLA_KEY_WIDTH), F32),
                  pltpu.VMEM((tb, GLA_VAL_WIDTH), F32)]
    assert len(rw_scratch) == RW_N_SCRATCH and len(gl_scratch) == GLA_N_SCRATCH
    return pl.pallas_call(
        _recurrent_kernel,
        grid=(batch, nb),
        in_specs=[rows(RW_IN_WIDTH), const((1, RW_IN_WIDTH)),
                  vec, vec, vec, vec, vec, vec, vec, up, up,
                  rows(GLA_PAD_WIDTH), const((1, GLA_KEY_WIDTH)), const((1, GLA_VAL_WIDTH)),
                  const((LANES, GLA_KEY_WIDTH)), const((LANES, GLA_KEY_WIDTH))],
        out_specs=[rows(RW_WIDTH), rows(GLA_VAL_WIDTH)],
        out_shape=[jax.ShapeDtypeStruct((n, RW_WIDTH), BF16),
                   jax.ShapeDtypeStruct((n, GLA_VAL_WIDTH), BF16)],
        scratch_shapes=rw_scratch + gl_scratch,
        compiler_params=_cparams(("parallel", "arbitrary")),
        name="rwkv7_gla",
    )(rw, rprm["mu"], rprm["w0"], rprm["a0"], rprm["k_k"], rprm["k_a"], rprm["r_k"],
      rprm["gn_g"], rprm["gn_b"], rprm["up_hi"], rprm["up_lo"],
      gl, gprm["gate_b"], gprm["norm_g"], gprm["up_hi"], gprm["up_lo"])


GLA_BLOCK = 256
GLA_SUB = 16
GLA_EXP_CAP = 80.0


def _gla_body(x_ref, gb_ref, ng_ref, uphi_ref, uplo_ref, o_ref, state_s, q_s, k_s, la_s, o_s):
    tb = GLA_BLOCK
    c = CHUNK
    heads = GLA_VAL_WIDTH // HEAD_DIM

    kw, vw = GLA_KEY_WIDTH, GLA_VAL_WIDTH
    code = x_ref[:, 2 * kw + 2 * vw:]
    pre = _dot_x3(code, uphi_ref[...], uplo_ref[...]) + gb_ref[...]
    la_s[...] = -(jnp.maximum(-pre, 0.0) + jnp.log(1.0 + jnp.exp(-jnp.abs(pre)))) * (1.0 / GLA_GATE_NORMALIZER)
    q_s[...] = x_ref[:, 0:kw] * (GLA_DK ** -0.5)
    k_s[...] = x_ref[:, kw:2 * kw]

    tr = lax.broadcasted_iota(jnp.int32, (c, c), 0)
    tc = lax.broadcasted_iota(jnp.int32, (c, c), 1)
    tri = (tc <= tr).astype(BF16)
    sub = GLA_SUB
    n_sub = c // sub
    srow = lax.broadcasted_iota(jnp.int32, (heads * c, c), 0)
    scol = lax.broadcasted_iota(jnp.int32, (heads * c, c), 1)
    causal = scol <= (srow // (heads * sub)) * sub + srow % sub
    klane = lax.broadcasted_iota(jnp.int32, (1, kw), 1) // GLA_DK
    vlane = lax.broadcasted_iota(jnp.int32, (1, vw), 1) // HEAD_DIM
    st_r = lax.broadcasted_iota(jnp.int32, (vw, kw), 0) // HEAD_DIM
    st_c = lax.broadcasted_iota(jnp.int32, (vw, kw), 1) // GLA_DK
    st_mask = st_r == st_c

    n_chunk = tb // c
    pre = []
    for ci in range(n_chunk):
        rows = slice(ci * c, (ci + 1) * c)
        bc = _dot_exact_lhs(tri, la_s[rows, :])
        e_pos = jnp.exp(bc)
        qc = q_s[rows, :]
        kc = k_s[rows, :]
        k2 = (kc * jnp.exp(bc[c - 1:c, :] - bc)).astype(BF16)
        vb = x_ref[rows, 2 * kw:2 * kw + vw].astype(BF16)
        qsts, kes = [], []
        for i in range(n_sub):
            srows = slice(i * sub, (i + 1) * sub)
            ref = bc[i * sub - 1:i * sub, :] if i > 0 else jnp.zeros((1, kw), F32)
            qi = qc[srows, :] * jnp.exp(bc[srows, :] - ref)
            qsts.append(jnp.concatenate([jnp.where(klane == h, qi, 0.0) for h in range(heads)],
                                        axis=0).astype(BF16))
            kes.append((kc * jnp.exp(jnp.minimum(ref - bc, GLA_EXP_CAP))).astype(BF16))
        pre.append(((qc * e_pos).astype(BF16), k2, vb, qsts, kes, e_pos[c - 1:c, :]))
    scores = [jnp.where(causal, jnp.concatenate([_dot_nt(qs, ke) for qs, ke in zip(t[3], t[4])], axis=0),
                        0.0).astype(BF16) for t in pre]
    o_sts = [_dot(sc, t[2]) for sc, t in zip(scores, pre)]
    incs = [jnp.where(st_mask, _dot_tn(t[2], t[1]), 0.0) for t in pre]
    state = state_s[...]
    for ci in range(n_chunk):
        intra = []
        for i in range(n_sub):
            acc = None
            for h in range(heads):
                r0 = (i * heads + h) * sub
                part = jnp.where(vlane == h, o_sts[ci][r0:r0 + sub, :], 0.0)
                acc = part if acc is None else acc + part
            intra.append(acc)
        o_s[ci * c:(ci + 1) * c, :] = _dot_nt(pre[ci][0], state.astype(BF16)) + jnp.concatenate(intra, axis=0)
        state = state * pre[ci][5] + incs[ci]
    state_s[...] = state

    lane = lax.broadcasted_iota(jnp.int32, (1, LANES), 1)
    low_half = lane < HEAD_DIM
    outs = []
    for p in range(vw // LANES):
        o = o_s[:, p * LANES:(p + 1) * LANES]
        ms = _pair_head_sum(o * o, low_half) * (1.0 / HEAD_DIM)
        outs.append(o * lax.rsqrt(ms + EPS))
    g = x_ref[:, 2 * kw + vw:2 * kw + 2 * vw]
    on = jnp.concatenate(outs, axis=1) * ng_ref[...]
    o_ref[...] = (on * (g * jax.nn.sigmoid(g))).astype(o_ref.dtype)


MLP_ROWS = 1024
MLP_FF_TILE = 1024


def _mlp_kernel(x_ref, sb_ref, rw_ref, gl_ref, wsb_ref, wrw_ref, wgl_ref, gmix_ref,
                gpre_ref, w1_ref, w2_ref, gpost_ref, o_ref):
    mixed = _dot(sb_ref[...], wsb_ref[...]) + _dot(rw_ref[...], wrw_ref[...]) + _dot(gl_ref[...], wgl_ref[...])
    x1 = x_ref[...] + _rms(mixed, gmix_ref[...])
    h = _rms(x1, gpre_ref[...]).astype(BF16)
    ff = None
    for j in range(D_FF // MLP_FF_TILE):
        cols = slice(j * MLP_FF_TILE, (j + 1) * MLP_FF_TILE)
        a = jnp.maximum(_dot(h, w1_ref[:, cols]), 0.0)
        part = _dot((a * a).astype(BF16), w2_ref[cols, :])
        ff = part if ff is None else ff + part
    o_ref[...] = x1 + _rms(ff, gpost_ref[...])


def _mlp(x2, o_sb, o_rw, o_gl, w_sb, w_rw, w_gl, g_mix, g_pre, w1, w2, g_post):
    n = x2.shape[0]
    tm = MLP_ROWS
    full = lambda w: pl.BlockSpec(w.shape, lambda i: (0, 0), pipeline_mode=pl.Buffered(1))
    row = lambda width: pl.BlockSpec((tm, width), lambda i: (i, 0))
    return pl.pallas_call(
        _mlp_kernel,
        grid=(n // tm,),
        in_specs=[row(D_MODEL), row(SB_WIDTH), row(RW_WIDTH), row(GLA_VAL_WIDTH),
                  full(w_sb), full(w_rw), full(w_gl), full(g_mix),
                  full(g_pre), full(w1), full(w2), full(g_post)],
        out_specs=row(D_MODEL),
        out_shape=jax.ShapeDtypeStruct((n, D_MODEL), F32),
        compiler_params=_cparams(("parallel",)),
        name="out_proj_mlp",
    )(x2, o_sb, o_rw, o_gl, w_sb, w_rw, w_gl, g_mix, g_pre, w1, w2, g_post)


def _hi_lo(w):
    hi = w.astype(BF16)
    return hi, (w - hi.astype(F32)).astype(BF16)


def _prep_layer(l, w_in, rw_w_up, rw_a_up, rw_g_up, gla_gate_up, w_out, w_ff1, w_ff2):
    w = w_in[l]
    sb_end = 3 * SB_WIDTH
    rw_end = sb_end + RW_IN_WIDTH
    w_sb = w[:, :sb_end].astype(BF16)
    w_rw = w[:, sb_end:rw_end].astype(BF16)
    g0 = rw_end
    kw, vw, gr = GLA_KEY_WIDTH, GLA_VAL_WIDTH, GLA_GATE_RANK
    w_gl = jnp.concatenate([
        w[:, g0:g0 + 2 * kw + vw],
        w[:, g0 + 2 * kw + vw + gr:g0 + GLA_IN_WIDTH],
        w[:, g0 + 2 * kw + vw:g0 + 2 * kw + vw + gr],
        jnp.zeros((D_MODEL, GLA_PAD_WIDTH - GLA_IN_WIDTH), F32)], axis=1).astype(BF16)
    up = jnp.zeros((RW_CODE, 3 * RW_WIDTH), F32)
    up = up.at[0:32, 0:RW_WIDTH].set(rw_w_up[l])
    up = up.at[32:64, RW_WIDTH:2 * RW_WIDTH].set(rw_a_up[l])
    up = up.at[64:128, 2 * RW_WIDTH:].set(rw_g_up[l])
    rw_up_hi, rw_up_lo = _hi_lo(up)
    gup = jnp.zeros((LANES, GLA_KEY_WIDTH), F32).at[0:gr, :].set(gla_gate_up[l])
    gl_up_hi, gl_up_lo = _hi_lo(gup)
    wo = w_out[l].astype(BF16)
    return dict(w_sb=w_sb, w_rw=w_rw, w_gl=w_gl, rw_up_hi=rw_up_hi, rw_up_lo=rw_up_lo,
                gl_up_hi=gl_up_hi, gl_up_lo=gl_up_lo,
                wo_sb=wo[:SB_WIDTH], wo_rw=wo[SB_WIDTH:SB_WIDTH + RW_WIDTH], wo_gl=wo[SB_WIDTH + RW_WIDTH:],
                w1=w_ff1[l].astype(BF16), w2=w_ff2[l].astype(BF16))


def kernel(x, pre_mix_g, w_in, sb_norm_g, rw_mu, rw_w0, rw_w_up, rw_a0, rw_a_up, rw_g_up, rw_k_k, rw_k_a, rw_r_k, rw_gn_g, rw_gn_b, gla_gate_up, gla_gate_b, gla_norm_g, w_out, post_mix_g, pre_ffn_g, w_ff1, w_ff2, post_ffn_g):
    batch, seq, d = x.shape
    depth = w_in.shape[0]
    x2 = x.reshape(batch * seq, d)
    row = lambda t, l: t[l][None, :]
    for l in range(depth):
        wp = _prep_layer(l, w_in, rw_w_up, rw_a_up, rw_g_up, gla_gate_up, w_out, w_ff1, w_ff2)
        sb, rw, gl = _inproj(x2, row(pre_mix_g, l), wp["w_sb"], wp["w_rw"], wp["w_gl"])
        o_sb = _sb_attention(sb, row(sb_norm_g, l), batch, seq)
        o_rw, o_gl = _recurrent_mixers(
            rw, gl,
            dict(mu=row(rw_mu, l), w0=row(rw_w0, l), a0=row(rw_a0, l), k_k=row(rw_k_k, l),
                 k_a=row(rw_k_a, l), r_k=row(rw_r_k, l), gn_g=row(rw_gn_g, l),
                 gn_b=row(rw_gn_b, l), up_hi=wp["rw_up_hi"], up_lo=wp["rw_up_lo"]),
            dict(gate_b=row(gla_gate_b, l), norm_g=row(gla_norm_g, l),
                 up_hi=wp["gl_up_hi"], up_lo=wp["gl_up_lo"]), batch, seq)
        x2 = _mlp(x2, o_sb, o_rw, o_gl, wp["wo_sb"], wp["wo_rw"], wp["wo_gl"], row(post_mix_g, l),
                  row(pre_ffn_g, l), wp["w1"], wp["w2"], row(post_ffn_g, l))
    return x2.reshape(batch, seq, d)
```

```python
import functools

import jax
import jax.numpy as jnp
from jax import lax
from jax.experimental import pallas as pl
from jax.experimental.pallas import tpu as pltpu

F32 = jnp.float32
BF16 = jnp.bfloat16

D_MODEL = 1024
HEAD_DIM = 64
LANES = 128
SB_WIDTH = 384
RW_WIDTH = 384
RW_IN_WIDTH = 1280
RW_CODE = 128
GLA_KEY_WIDTH = 128
GLA_VAL_WIDTH = 256
GLA_DK = 32
GLA_GATE_RANK = 16
GLA_IN_WIDTH = 784
GLA_PAD_WIDTH = 896
GLA_GATE_NORMALIZER = 16.0
D_FF = 4096
EPS = 1e-6
RW_GN_EPS = 64e-5
CHUNK = 64
VMEM_LIMIT = 58 * 1024 * 1024

SB_BLOCK = 256
LOG2E = 1.4426950408889634
SB_DEAD = 110.0 * LOG2E
SB_CLAMP = 120.0
SB_LAG = 4


def _cparams(sem):
    return pltpu.CompilerParams(dimension_semantics=sem, vmem_limit_bytes=VMEM_LIMIT)


def _dot(a, b):
    return jnp.dot(a, b, preferred_element_type=F32)


def _dot_nt(a, b):
    return lax.dot_general(a, b, (((1,), (1,)), ((), ())), preferred_element_type=F32)


def _dot_tn(a, b):
    return lax.dot_general(a, b, (((0,), (0,)), ((), ())), preferred_element_type=F32)


def _split2(x):
    hi = x.astype(BF16)
    lo = (x - hi.astype(F32)).astype(BF16)
    return hi, lo


def _split3(x):
    hi = x.astype(BF16)
    r1 = x - hi.astype(F32)
    mid = r1.astype(BF16)
    lo = (r1 - mid.astype(F32)).astype(BF16)
    return hi, mid, lo


def _dot_x3(a, w_hi, w_lo):
    a_hi, a_lo = _split2(a)
    return _dot(a_hi, w_hi) + (_dot(a_lo, w_hi) + _dot(a_hi, w_lo))


def _dot_exact_lhs(m_bf16, x):
    hi, mid, lo = _split3(x)
    return _dot(m_bf16, hi) + (_dot(m_bf16, mid) + _dot(m_bf16, lo))


def _rms(x, gain):
    return x * lax.rsqrt(jnp.mean(x * x, axis=-1, keepdims=True) + EPS) * gain


def _pair_head_sum(x, low_half):
    s_all = jnp.sum(x, axis=-1, keepdims=True)
    s_low = jnp.sum(jnp.where(low_half, x, 0.0), axis=-1, keepdims=True)
    return jnp.where(low_half, s_low, s_all - s_low)


def _inproj_kernel(x_ref, g_ref, wsb_ref, wrw_ref, wgl_ref, sb_ref, rw_ref, gl_ref):
    h = _rms(x_ref[...], g_ref[...]).astype(BF16)
    col = lax.broadcasted_iota(jnp.int32, (1, 3 * SB_WIDTH), 1)
    qscale = jnp.where(col < SB_WIDTH, LOG2E * HEAD_DIM ** -0.5, 1.0)
    sb_ref[...] = (_dot(h, wsb_ref[...]) * qscale).astype(BF16)
    rw_ref[...] = _dot(h, wrw_ref[...])
    gl_ref[...] = _dot(h, wgl_ref[...])


INPROJ_ROWS = 1024


def _inproj(x2, gain, w_sb, w_rw, w_gl, tm=INPROJ_ROWS):
    n = x2.shape[0]
    full = lambda w: pl.BlockSpec(w.shape, lambda i: (0, 0), pipeline_mode=pl.Buffered(1))
    row = lambda width: pl.BlockSpec((tm, width), lambda i: (i, 0))
    return pl.pallas_call(
        _inproj_kernel,
        grid=(n // tm,),
        in_specs=[row(D_MODEL), full(gain), full(w_sb), full(w_rw), full(w_gl)],
        out_specs=[row(3 * SB_WIDTH), row(RW_IN_WIDTH), row(GLA_PAD_WIDTH)],
        out_shape=[jax.ShapeDtypeStruct((n, 3 * SB_WIDTH), BF16),
                   jax.ShapeDtypeStruct((n, RW_IN_WIDTH), F32),
                   jax.ShapeDtypeStruct((n, GLA_PAD_WIDTH), F32)],
        compiler_params=_cparams(("parallel",)),
        name="in_proj",
    )(x2, gain, w_sb, w_rw, w_gl)


def _sb_kernel(q_ref, k_ref, v_ref, g_ref, o_ref):
    qb = SB_BLOCK
    qi = pl.program_id(1)
    lane = lax.broadcasted_iota(jnp.int32, (1, LANES), 1)
    low_half = lane < HEAD_DIM
    row = lax.broadcasted_iota(jnp.int32, (qb, qb), 0)
    col = lax.broadcasted_iota(jnp.int32, (qb, qb), 1)
    causal = col < row
    suffix = (row > col).astype(BF16)
    half = qb // 2
    pairs = SB_WIDTH // LANES
    zero = jnp.zeros((), BF16)
    vmasks = (low_half, jnp.logical_not(low_half))
    chains = [(h, r) for r in range(2) for h in range(2)]
    lanes_of = [slice(pr * LANES, (pr + 1) * LANES) for pr in range(pairs)]
    qcs = []
    for sl in lanes_of:
        q = q_ref[:, sl]
        qhs = [jnp.where(m, q, zero) for m in vmasks]
        qcs.append([qhs[h][r * half:(r + 1) * half, :] for h, r in chains])

    def front(qc, kb, n, mask):
        return front_tail(_dot_nt(qc, kb[0:n, :]), n, mask)

    def front_tail(z, n, mask):
        p = jnp.maximum(z, jnp.log2(1.0 + jnp.exp2(jnp.minimum(z, SB_CLAMP))))
        log_beta = z - p
        if mask is not None:
            p = jnp.where(mask, p, 0.0)
        later = _dot(p.astype(BF16), suffix[0:n, 0:n])
        return log_beta, p[:, 0:1], later

    def back(log_beta, p_first, later, spent, mask):
        w = jnp.exp2(log_beta - (later + spent))
        if mask is not None:
            w = jnp.where(mask, w, 0.0)
        return w.astype(BF16), spent + (later[:, 0:1] + p_first)

    def values(vb, n):
        return [jnp.where(m, vb[0:n, :], zero) for m in vmasks]

    d_start = pl.multiple_of(qi * qb, qb)
    p_start = pl.multiple_of(jnp.maximum(qi - 1, 0) * qb, qb)
    d_keys = [half if r == 0 else qb for _, r in chains]
    d_masks = [causal[0:half, 0:half] if r == 0 else causal[half:, :] for _, r in chains]
    seq_chains = []
    for pr in range(pairs):
        for prev in (False, True):
            for ci, (h, r) in enumerate(chains):
                seq_chains.append((pr, prev, ci, h, r))
    n_seq = len(seq_chains)
    p_vmasks = [jnp.logical_and(m, qi > 0) for m in vmasks]

    logits, fronts, spent_of = [None] * n_seq, [None] * n_seq, {}
    acc_of = {}
    def span(prev, r):
        if prev:
            return (0, qb) if r == 0 else (half, half)
        return (0, half) if r == 0 else (0, qb)

    for t in range(n_seq + 2 * SB_LAG):
        c2, c1 = t - 2 * SB_LAG, t - SB_LAG
        if 0 <= c2 < n_seq:
            pr, prev, ci, h, r = seq_chains[c2]
            k0, n = span(prev, r)
            mask = None if prev else d_masks[ci]
            spent = spent_of[(pr, ci)] if prev else jnp.zeros((half, 1), F32)
            log_beta, p_first, later = fronts[c2]
            fronts[c2] = None
            w, spent_of[(pr, ci)] = back(log_beta, p_first, later, spent, mask)
            start = p_start if prev else d_start
            vb = v_ref[pl.ds(start, qb), lanes_of[pr]]
            vh = jnp.where(p_vmasks[h] if prev else vmasks[h], vb[k0:k0 + n, :], zero)
            part = _dot(w, vh)
            acc_of[(pr, r)] = part if (pr, r) not in acc_of else acc_of[(pr, r)] + part
        if 0 <= c1 < n_seq:
            pr, prev, ci, h, r = seq_chains[c1]
            mask = None if prev else d_masks[ci]
            fronts[c1] = front_tail(logits[c1], span(prev, r)[1], mask)
            logits[c1] = None
        if t < n_seq:
            pr, prev, ci, h, r = seq_chains[t]
            k0, n = span(prev, r)
            start = p_start if prev else d_start
            kb = k_ref[pl.ds(start, qb), lanes_of[pr]]
            logits[t] = _dot_nt(qcs[pr][ci], kb[k0:k0 + n, :])
    accs_all = [[acc_of[(pr, r)] for r in range(2)] for pr in range(pairs)]
    bp_spent = [[spent_of[(pr, ci)] for ci in range(len(chains))] for pr in range(pairs)]

    outs = []
    late = [ci for ci, (_, r) in enumerate(chains) if r == 1]
    for pr, sl in enumerate(lanes_of):
        accs = accs_all[pr]

        def far_half(args, pr=pr, sl=sl):
            acc, spents = args
            kb, vb = k_ref[pl.ds(p_start, half), sl], v_ref[pl.ds(p_start, half), sl]
            fs = [front(qcs[pr][ci], kb, half, None) for ci in late]
            bs = [back(*f, sp, None) for f, sp in zip(fs, spents)]
            vh = jnp.concatenate(values(vb, half), axis=0)
            return acc + _dot(jnp.concatenate([b[0] for b in bs], axis=1), vh), [b[1] for b in bs]

        late_spent = [bp_spent[pr][ci] for ci in late]
        needed = jnp.logical_and(
            qi > 0, jnp.minimum(jnp.min(late_spent[0]), jnp.min(late_spent[1])) < SB_DEAD)
        acc_late, late_spent = lax.cond(needed, far_half, lambda args: args, (accs[1], late_spent))
        accs = [accs[0], acc_late]
        for ci, sp in zip(late, late_spent):
            bp_spent[pr][ci] = sp

        def cond(s):
            j, spents, _ = s
            alive = jnp.minimum(jnp.minimum(jnp.min(spents[0]), jnp.min(spents[1])),
                                jnp.minimum(jnp.min(spents[2]), jnp.min(spents[3])))
            return jnp.logical_and(j >= 0, alive < SB_DEAD)

        def body(s, pr=pr, sl=sl):
            j, spents, accs = s
            start = pl.multiple_of(j * qb, qb)
            kb, vb = k_ref[pl.ds(start, qb), sl], v_ref[pl.ds(start, qb), sl]
            fs = [front(qc, kb, qb, None) for qc in qcs[pr]]
            bs = [back(*f, sp, None) for f, sp in zip(fs, spents)]
            vh = jnp.concatenate(values(vb, qb), axis=0)
            accs = [acc + _dot(jnp.concatenate([bs[2 * r][0], bs[2 * r + 1][0]], axis=1), vh)
                    for r, acc in enumerate(accs)]
            return j - 1, [b[1] for b in bs], accs

        _, _, accs = lax.while_loop(cond, body, (qi - 2, bp_spent[pr], accs))
        acc = jnp.concatenate(accs, axis=0)
        ms = _pair_head_sum(acc * acc, low_half) * (1.0 / HEAD_DIM)
        outs.append(acc * lax.rsqrt(ms + EPS))
    o_ref[...] = (jnp.concatenate(outs, axis=1) * g_ref[...]).astype(o_ref.dtype)


def _sb_attention(sb, gain, batch, seq):
    nq = seq // SB_BLOCK
    n = sb.shape[0]
    pairs = SB_WIDTH // LANES
    return pl.pallas_call(
        _sb_kernel,
        grid=(batch, nq),
        in_specs=[
            pl.BlockSpec((SB_BLOCK, SB_WIDTH), lambda b, i: (b * nq + i, 0)),
            pl.BlockSpec((seq, SB_WIDTH), lambda b, i: (b, 1)),
            pl.BlockSpec((seq, SB_WIDTH), lambda b, i: (b, 2)),
            pl.BlockSpec((1, SB_WIDTH), lambda b, i: (0, 0)),
        ],
        out_specs=pl.BlockSpec((SB_BLOCK, SB_WIDTH), lambda b, i: (b * nq + i, 0)),
        out_shape=jax.ShapeDtypeStruct((n, SB_WIDTH), BF16),
        compiler_params=_cparams(("parallel", "arbitrary")),
        name="sb_attention",
    )(sb, sb, sb, gain)


RW_BLOCK = 256
RW_GROUP = 4


def _rw_body(x_ref, mu_ref, w0_ref, a0_ref, kk_ref, ka_ref, rk_ref, gng_ref, gnb_ref,
             uphi_ref, uplo_ref, o_ref,
             state_s, prev_s, r_s, k_s, v_s, lw_s, a_s, b_s, y_s, lc_s,
             rp_s, yp_s, p_s, q_s, ge_s):
    tb = RW_BLOCK
    c = CHUNK
    pairs = RW_WIDTH // LANES

    lane = lax.broadcasted_iota(jnp.int32, (1, LANES), 1)
    low_half = lane < HEAD_DIM

    x = x_ref[...]
    first = lax.broadcasted_iota(jnp.int32, (tb, 1), 0) == 0
    prev = jnp.where(first, prev_s[...], pltpu.roll(x, 1, 0))
    prev_s[...] = x[tb - 1:tb, :]
    xs = x + (prev - x) * mu_ref[...]

    r = xs[:, 0:RW_WIDTH]
    k = xs[:, RW_WIDTH:2 * RW_WIDTH]
    v = xs[:, 2 * RW_WIDTH:3 * RW_WIDTH]
    code = xs[:, 3 * RW_WIDTH:]
    act = jnp.where(lane < 32, jnp.tanh(code), jnp.where(lane < 64, code, jax.nn.sigmoid(code)))
    up = jnp.concatenate(
        [_dot_x3(act, uphi_ref[:, 0:RW_WIDTH], uplo_ref[:, 0:RW_WIDTH]),
         _dot(act.astype(BF16), uphi_ref[:, RW_WIDTH:])], axis=1)
    wpre = w0_ref[...] + up[:, 0:RW_WIDTH]
    log_w = -(jnp.maximum(-wpre, 0.0) + jnp.log(1.0 + jnp.exp(-jnp.abs(wpre)))) - 0.5
    lw = -jnp.exp(log_w)
    a = jax.nn.sigmoid(a0_ref[...] + up[:, RW_WIDTH:2 * RW_WIDTH])
    gate = up[:, 2 * RW_WIDTH:]
    kmod = k * (1.0 + (a - 1.0) * ka_ref[...])
    kk = k * kk_ref[...]
    bonus_in = r * kmod * rk_ref[...]
    kkn, bon = [], []
    for p in range(pairs):
        sl = slice(p * LANES, (p + 1) * LANES)
        ss = _pair_head_sum(kk[:, sl] * kk[:, sl], low_half)
        kkn.append(kk[:, sl] * lax.rsqrt(jnp.maximum(ss, 1e-12)))
        bon.append(_pair_head_sum(bonus_in[:, sl], low_half))
    kkn = jnp.concatenate(kkn, axis=1)
    bonus = jnp.concatenate(bon, axis=1) * v
    r_s[...] = r
    k_s[...] = kmod
    v_s[...] = v
    lw_s[...] = lw
    a_s[...] = -kkn
    b_s[...] = kkn * a
    tr = lax.broadcasted_iota(jnp.int32, (tb, tb), 0)
    tc = lax.broadcasted_iota(jnp.int32, (tb, tb), 1)
    tri = jnp.logical_and(tr // c == tc // c, tc <= tr).astype(BF16)
    lc_s[...] = _dot_exact_lhs(tri, lw)

    rr = lax.broadcasted_iota(jnp.int32, (2 * c, 2 * c), 0)
    cc = lax.broadcasted_iota(jnp.int32, (2 * c, 2 * c), 1)
    same_head = (rr // c) == (cc // c)
    strict = jnp.logical_and(same_head, cc < rr)
    incl = jnp.logical_and(same_head, cc <= rr)
    m_lo = low_half
    m_hi = jnp.logical_not(low_half)

    def stack(xp):
        return jnp.concatenate([jnp.where(m_lo, xp, 0.0), jnp.where(m_hi, xp, 0.0)], axis=0)

    def prepare(gi, carry):
        chains = [(gi * RW_GROUP + cj, p) for cj in range(RW_GROUP) for p in range(pairs)]
        pre = []
        for ci, p in chains:
            rows = pl.ds(pl.multiple_of(ci * c, c), c)
            sl = slice(p * LANES, (p + 1) * LANES)
            lwc = lw_s[rows, sl]
            lc = lc_s[rows, sl]
            e_incl = jnp.exp(lc)
            e_excl = jnp.exp(lc - lwc)
            e_inv = jnp.exp(-lc)
            g_end = e_incl[c - 1:c, :]
            ag = stack(a_s[rows, sl] * e_excl).astype(BF16)
            rg = stack(r_s[rows, sl] * e_incl).astype(BF16)
            bd = b_s[rows, sl] * e_inv
            kd = k_s[rows, sl] * e_inv
            vst = stack(v_s[rows, sl]).astype(BF16)
            lhs = jnp.concatenate([ag, rg], axis=0)
            rhs = jnp.concatenate([bd, bd, kd, kd], axis=0).astype(BF16)
            bde = stack(bd * g_end).astype(BF16)
            kde = stack(kd * g_end).astype(BF16)
            idx = ci * pairs + p
            ge_s[idx] = g_end
            pre.append((idx, ag, rg, vst, lhs, rhs, kde, bde))
        gs = [_dot_nt(t[4], t[5]) for t in pre]
        npows, xs_, avs, arbs = [], [], [], []
        for t, g in zip(pre, gs):
            a_ab = jnp.where(strict, g[0:2 * c, 0:2 * c], 0.0)
            a_ak = jnp.where(strict, g[0:2 * c, 2 * c:], 0.0)
            a_rb = jnp.where(incl, g[2 * c:, 0:2 * c], 0.0)
            a_rk = jnp.where(incl, g[2 * c:, 2 * c:], 0.0)
            arbs.append(a_rb.astype(BF16))
            npows.append(a_ab)
            avs.append(jnp.concatenate([a_ak, a_rk], axis=0).astype(BF16))
        avs = [_dot(av, t[3]) for av, t in zip(avs, pre)]
        svs = [_dot_tn(t[3], t[6]) for t in pre]
        for t, av in zip(pre, avs):
            xs_.append(jnp.concatenate([t[1].astype(F32), av[0:2 * c, :]], axis=1))
        for it in range(6):
            nbs = [n.astype(BF16) for n in npows]
            xbs = [xc.astype(BF16) for xc in xs_]
            if it < 5:
                prods = [_dot(nb, jnp.concatenate([nb, xb], axis=1)) for nb, xb in zip(nbs, xbs)]
                npows = [pr[:, 0:LANES] for pr in prods]
                xs_ = [xc + pr[:, LANES:] for xc, pr in zip(xs_, prods)]
            else:
                xs_ = [xc + _dot(nb, xb) for xc, nb, xb in zip(xs_, nbs, xbs)]
        xbs = [xc.astype(BF16) for xc in xs_]
        ras = [_dot(arb, xb) for arb, xb in zip(arbs, xbs)]
        pqs = [_dot_tn(xb, t[7]) for xb, t in zip(xbs, pre)]
        for t, av, sv, ra, pq in zip(pre, avs, svs, ras, pqs):
            idx = t[0]
            rp_s[idx] = (t[2].astype(F32) + ra[:, 0:LANES]).astype(BF16)
            yp_s[idx] = ra[:, LANES:] + av[2 * c:, :]
            p_s[idx] = pq[0:LANES, :].astype(BF16)
            q_s[idx] = pq[LANES:, :] + sv
        return carry

    lax.fori_loop(0, tb // (c * RW_GROUP), prepare, 0)

    states = [state_s[p] for p in range(pairs)]
    for ci in range(tb // c):
        for p in range(pairs):
            idx = ci * pairs + p
            sb = states[p].astype(BF16)
            y = _dot_nt(rp_s[idx], sb) + yp_s[idx]
            y_s[ci * c:(ci + 1) * c, p * LANES:(p + 1) * LANES] = y[0:c, :] + y[c:2 * c, :]
            states[p] = states[p] * ge_s[idx] + _dot(sb, p_s[idx]) + q_s[idx]
    for p in range(pairs):
        state_s[p] = states[p]

    outs = []
    for p in range(pairs):
        sl = slice(p * LANES, (p + 1) * LANES)
        y = y_s[:, sl]
        mean = _pair_head_sum(y, low_half) * (1.0 / HEAD_DIM)
        d = y - mean
        var = _pair_head_sum(d * d, low_half) * (1.0 / HEAD_DIM)
        outs.append(d * lax.rsqrt(var + RW_GN_EPS))
    yn = jnp.concatenate(outs, axis=1) * gng_ref[...] + gnb_ref[...]
    o_ref[...] = ((yn + bonus) * gate).astype(o_ref.dtype)


RW_N_IN, RW_N_SCRATCH = 11, 15
GLA_N_IN, GLA_N_SCRATCH = 5, 5


def _recurrent_kernel(*refs):
    n_in = RW_N_IN + GLA_N_IN
    rw_in, gl_in = refs[:RW_N_IN], refs[RW_N_IN:n_in]
    o_rw, o_gl = refs[n_in], refs[n_in + 1]
    rw_scr = refs[n_in + 2:n_in + 2 + RW_N_SCRATCH]
    gl_scr = refs[n_in + 2 + RW_N_SCRATCH:]

    @pl.when(pl.program_id(1) == 0)
    def _():
        for state in (rw_scr[0], rw_scr[1], gl_scr[0]):
            state[...] = jnp.zeros_like(state)

    _rw_body(*rw_in, o_rw, *rw_scr)
    _gla_body(*gl_in, o_gl, *gl_scr)


def _recurrent_mixers(rw, gl, rprm, gprm, batch, seq):
    n = rw.shape[0]
    tb = RW_BLOCK
    nb = seq // tb
    const = lambda shape: pl.BlockSpec(shape, lambda b, i: (0,) * len(shape))
    rows = lambda width: pl.BlockSpec((tb, width), lambda b, i: (b * nb + i, 0))
    vec = const((1, RW_WIDTH))
    up = const((RW_CODE, 3 * RW_WIDTH))
    blk = lambda: pltpu.VMEM((tb, RW_WIDTH), F32)
    n_chain = (tb // CHUNK) * (RW_WIDTH // LANES)
    tile = lambda r, dt: pltpu.VMEM((n_chain, r, LANES), dt)
    rw_scratch = [pltpu.VMEM((RW_WIDTH // LANES, LANES, LANES), F32),
                  pltpu.VMEM((1, RW_IN_WIDTH), F32),
                  blk(), blk(), blk(), blk(), blk(), blk(), blk(), blk(),
                  tile(LANES, BF16), tile(LANES, F32), tile(LANES, BF16), tile(LANES, F32),
                  tile(1, F32)]
    gl_scratch = [pltpu.VMEM((GLA_VAL_WIDTH, GLA_KEY_WIDTH), F32),
                  pltpu.VMEM((tb, GLA_KEY_WIDTH), F32),
                  pltpu.VMEM((tb, GLA_KEY_WIDTH), F32),
                  pltpu.VMEM((tb, GLA_KEY_WIDTH), F32),
                  pltpu.VMEM((tb, GLA_VAL_WIDTH), F32)]
    assert len(rw_scratch) == RW_N_SCRATCH and len(gl_scratch) == GLA_N_SCRATCH
    return pl.pallas_call(
        _recurrent_kernel,
        grid=(batch, nb),
        in_specs=[rows(RW_IN_WIDTH), const((1, RW_IN_WIDTH)),
                  vec, vec, vec, vec, vec, vec, vec, up, up,
                  rows(GLA_PAD_WIDTH), const((1, GLA_KEY_WIDTH)), const((1, GLA_VAL_WIDTH)),
                  const((LANES, GLA_KEY_WIDTH)), const((LANES, GLA_KEY_WIDTH))],
        out_specs=[rows(RW_WIDTH), rows(GLA_VAL_WIDTH)],
        out_shape=[jax.ShapeDtypeStruct((n, RW_WIDTH), BF16),
                   jax.ShapeDtypeStruct((n, GLA_VAL_WIDTH), BF16)],
        scratch_shapes=rw_scratch + gl_scratch,
        compiler_params=_cparams(("parallel", "arbitrary")),
        name="rwkv7_gla",
    )(rw, rprm["mu"], rprm["w0"], rprm["a0"], rprm["k_k"], rprm["k_a"], rprm["r_k"],
      rprm["gn_g"], rprm["gn_b"], rprm["up_hi"], rprm["up_lo"],
      gl, gprm["gate_b"], gprm["norm_g"], gprm["up_hi"], gprm["up_lo"])


GLA_BLOCK = 256
GLA_SUB = 16
GLA_EXP_CAP = 80.0


def _gla_body(x_ref, gb_ref, ng_ref, uphi_ref, uplo_ref, o_ref, state_s, q_s, k_s, la_s, o_s):
    tb = GLA_BLOCK
    c = CHUNK
    heads = GLA_VAL_WIDTH // HEAD_DIM

    kw, vw = GLA_KEY_WIDTH, GLA_VAL_WIDTH
    code = x_ref[:, 2 * kw + 2 * vw:]
    pre = _dot_x3(code, uphi_ref[...], uplo_ref[...]) + gb_ref[...]
    la_s[...] = -(jnp.maximum(-pre, 0.0) + jnp.log(1.0 + jnp.exp(-jnp.abs(pre)))) * (1.0 / GLA_GATE_NORMALIZER)
    q_s[...] = x_ref[:, 0:kw] * (GLA_DK ** -0.5)
    k_s[...] = x_ref[:, kw:2 * kw]

    tr = lax.broadcasted_iota(jnp.int32, (c, c), 0)
    tc = lax.broadcasted_iota(jnp.int32, (c, c), 1)
    tri = (tc <= tr).astype(BF16)
    sub = GLA_SUB
    n_sub = c // sub
    srow = lax.broadcasted_iota(jnp.int32, (heads * c, c), 0)
    scol = lax.broadcasted_iota(jnp.int32, (heads * c, c), 1)
    causal = scol <= (srow // (heads * sub)) * sub + srow % sub
    klane = lax.broadcasted_iota(jnp.int32, (1, kw), 1) // GLA_DK
    vlane = lax.broadcasted_iota(jnp.int32, (1, vw), 1) // HEAD_DIM
    st_r = lax.broadcasted_iota(jnp.int32, (vw, kw), 0) // HEAD_DIM
    st_c = lax.broadcasted_iota(jnp.int32, (vw, kw), 1) // GLA_DK
    st_mask = st_r == st_c

    n_chunk = tb // c
    pre = []
    for ci in range(n_chunk):
        rows = slice(ci * c, (ci + 1) * c)
        bc = _dot_exact_lhs(tri, la_s[rows, :])
        e_pos = jnp.exp(bc)
        qc = q_s[rows, :]
        kc = k_s[rows, :]
        k2 = (kc * jnp.exp(bc[c - 1:c, :] - bc)).astype(BF16)
        vb = x_ref[rows, 2 * kw:2 * kw + vw].astype(BF16)
        qsts, kes = [], []
        for i in range(n_sub):
            srows = slice(i * sub, (i + 1) * sub)
            ref = bc[i * sub - 1:i * sub, :] if i > 0 else jnp.zeros((1, kw), F32)
            qi = qc[srows, :] * jnp.exp(bc[srows, :] - ref)
            qsts.append(jnp.concatenate([jnp.where(klane == h, qi, 0.0) for h in range(heads)],
                                        axis=0).astype(BF16))
            kes.append((kc * jnp.exp(jnp.minimum(ref - bc, GLA_EXP_CAP))).astype(BF16))
        pre.append(((qc * e_pos).astype(BF16), k2, vb, qsts, kes, e_pos[c - 1:c, :]))
    scores = [jnp.where(causal, jnp.concatenate([_dot_nt(qs, ke) for qs, ke in zip(t[3], t[4])], axis=0),
                        0.0).astype(BF16) for t in pre]
    o_sts = [_dot(sc, t[2]) for sc, t in zip(scores, pre)]
    incs = [jnp.where(st_mask, _dot_tn(t[2], t[1]), 0.0) for t in pre]
    state = state_s[...]
    for ci in range(n_chunk):
        intra = []
        for i in range(n_sub):
            acc = None
            for h in range(heads):
                r0 = (i * heads + h) * sub
                part = jnp.where(vlane == h, o_sts[ci][r0:r0 + sub, :], 0.0)
                acc = part if acc is None else acc + part
            intra.append(acc)
        o_s[ci * c:(ci + 1) * c, :] = _dot_nt(pre[ci][0], state.astype(BF16)) + jnp.concatenate(intra, axis=0)
        state = state * pre[ci][5] + incs[ci]
    state_s[...] = state

    lane = lax.broadcasted_iota(jnp.int32, (1, LANES), 1)
    low_half = lane < HEAD_DIM
    outs = []
    for p in range(vw // LANES):
        o = o_s[:, p * LANES:(p + 1) * LANES]
        ms = _pair_head_sum(o * o, low_half) * (1.0 / HEAD_DIM)
        outs.append(o * lax.rsqrt(ms + EPS))
    g = x_ref[:, 2 * kw + vw:2 * kw + 2 * vw]
    on = jnp.concatenate(outs, axis=1) * ng_ref[...]
    o_ref[...] = (on * (g * jax.nn.sigmoid(g))).astype(o_ref.dtype)


MLP_ROWS = 1024
MLP_FF_TILE = 1024


def _mlp_kernel(x_ref, sb_ref, rw_ref, gl_ref, wsb_ref, wrw_ref, wgl_ref, gmix_ref,
                gpre_ref, w1_ref, w2_ref, gpost_ref, o_ref):
    mixed = _dot(sb_ref[...], wsb_ref[...]) + _dot(rw_ref[...], wrw_ref[...]) + _dot(gl_ref[...], wgl_ref[...])
    x1 = x_ref[...] + _rms(mixed, gmix_ref[...])
    h = _rms(x1, gpre_ref[...]).astype(BF16)
    ff = None
    for j in range(D_FF // MLP_FF_TILE):
        cols = slice(j * MLP_FF_TILE, (j + 1) * MLP_FF_TILE)
        a = jnp.maximum(_dot(h, w1_ref[:, cols]), 0.0)
        part = _dot((a * a).astype(BF16), w2_ref[cols, :])
        ff = part if ff is None else ff + part
    o_ref[...] = x1 + _rms(ff, gpost_ref[...])


def _mlp(x2, o_sb, o_rw, o_gl, w_sb, w_rw, w_gl, g_mix, g_pre, w1, w2, g_post):
    n = x2.shape[0]
    tm = MLP_ROWS
    full = lambda w: pl.BlockSpec(w.shape, lambda i: (0, 0), pipeline_mode=pl.Buffered(1))
    row = lambda width: pl.BlockSpec((tm, width), lambda i: (i, 0))
    return pl.pallas_call(
        _mlp_kernel,
        grid=(n // tm,),
        in_specs=[row(D_MODEL), row(SB_WIDTH), row(RW_WIDTH), row(GLA_VAL_WIDTH),
                  full(w_sb), full(w_rw), full(w_gl), full(g_mix),
                  full(g_pre), full(w1), full(w2), full(g_post)],
        out_specs=row(D_MODEL),
        out_shape=jax.ShapeDtypeStruct((n, D_MODEL), F32),
        compiler_params=_cparams(("parallel",)),
        name="out_proj_mlp",
    )(x2, o_sb, o_rw, o_gl, w_sb, w_rw, w_gl, g_mix, g_pre, w1, w2, g_post)


def _hi_lo(w):
    hi = w.astype(BF16)
    return hi, (w - hi.astype(F32)).astype(BF16)


def _prep_layer(l, w_in, rw_w_up, rw_a_up, rw_g_up, gla_gate_up, w_out, w_ff1, w_ff2):
    w = w_in[l]
    sb_end = 3 * SB_WIDTH
    rw_end = sb_end + RW_IN_WIDTH
    w_sb = w[:, :sb_end].astype(BF16)
    w_rw = w[:, sb_end:rw_end].astype(BF16)
    g0 = rw_end
    kw, vw, gr = GLA_KEY_WIDTH, GLA_VAL_WIDTH, GLA_GATE_RANK
    w_gl = jnp.concatenate([
        w[:, g0:g0 + 2 * kw + vw],
        w[:, g0 + 2 * kw + vw + gr:g0 + GLA_IN_WIDTH],
        w[:, g0 + 2 * kw + vw:g0 + 2 * kw + vw + gr],
        jnp.zeros((D_MODEL, GLA_PAD_WIDTH - GLA_IN_WIDTH), F32)], axis=1).astype(BF16)
    up = jnp.zeros((RW_CODE, 3 * RW_WIDTH), F32)
    up = up.at[0:32, 0:RW_WIDTH].set(rw_w_up[l])
    up = up.at[32:64, RW_WIDTH:2 * RW_WIDTH].set(rw_a_up[l])
    up = up.at[64:128, 2 * RW_WIDTH:].set(rw_g_up[l])
    rw_up_hi, rw_up_lo = _hi_lo(up)
    gup = jnp.zeros((LANES, GLA_KEY_WIDTH), F32).at[0:gr, :].set(gla_gate_up[l])
    gl_up_hi, gl_up_lo = _hi_lo(gup)
    wo = w_out[l].astype(BF16)
    return dict(w_sb=w_sb, w_rw=w_rw, w_gl=w_gl, rw_up_hi=rw_up_hi, rw_up_lo=rw_up_lo,
                gl_up_hi=gl_up_hi, gl_up_lo=gl_up_lo,
                wo_sb=wo[:SB_WIDTH], wo_rw=wo[SB_WIDTH:SB_WIDTH + RW_WIDTH], wo_gl=wo[SB_WIDTH + RW_WIDTH:],
                w1=w_ff1[l].astype(BF16), w2=w_ff2[l].astype(BF16))


def kernel(x, pre_mix_g, w_in, sb_norm_g, rw_mu, rw_w0, rw_w_up, rw_a0, rw_a_up, rw_g_up, rw_k_k, rw_k_a, rw_r_k, rw_gn_g, rw_gn_b, gla_gate_up, gla_gate_b, gla_norm_g, w_out, post_mix_g, pre_ffn_g, w_ff1, w_ff2, post_ffn_g):
    batch, seq, d = x.shape
    depth = w_in.shape[0]
    x2 = x.reshape(batch * seq, d)
    row = lambda t, l: t[l][None, :]
    for l in range(depth):
        wp = _prep_layer(l, w_in, rw_w_up, rw_a_up, rw_g_up, gla_gate_up, w_out, w_ff1, w_ff2)
        sb, rw, gl = _inproj(x2, row(pre_mix_g, l), wp["w_sb"], wp["w_rw"], wp["w_gl"])
        o_sb = _sb_attention(sb, row(sb_norm_g, l), batch, seq)
        o_rw, o_gl = _recurrent_mixers(
            rw, gl,
            dict(mu=row(rw_mu, l), w0=row(rw_w0, l), a0=row(rw_a0, l), k_k=row(rw_k_k, l),
                 k_a=row(rw_k_a, l), r_k=row(rw_r_k, l), gn_g=row(rw_gn_g, l),
                 gn_b=row(rw_gn_b, l), up_hi=wp["rw_up_hi"], up_lo=wp["rw_up_lo"]),
            dict(gate_b=row(gla_gate_b, l), norm_g=row(gla_norm_g, l),
                 up_hi=wp["gl_up_hi"], up_lo=wp["gl_up_lo"]), batch, seq)
        x2 = _mlp(x2, o_sb, o_rw, o_gl, wp["wo_sb"], wp["wo_rw"], wp["wo_gl"], row(post_mix_g, l),
                  row(pre_ffn_g, l), wp["w1"], wp["w2"], row(post_ffn_g, l))
    return x2.reshape(batch, seq, d)
```

```python
import functools

import jax
import jax.numpy as jnp
from jax import lax
from jax.experimental import pallas as pl
from jax.experimental.pallas import tpu as pltpu

F32 = jnp.float32
BF16 = jnp.bfloat16

D_MODEL = 1024
HEAD_DIM = 64
LANES = 128
SB_WIDTH = 384
RW_WIDTH = 384
RW_IN_WIDTH = 1280
RW_CODE = 128
GLA_KEY_WIDTH = 128
GLA_VAL_WIDTH = 256
GLA_DK = 32
GLA_GATE_RANK = 16
GLA_IN_WIDTH = 784
GLA_PAD_WIDTH = 896
GLA_GATE_NORMALIZER = 16.0
D_FF = 4096
EPS = 1e-6
RW_GN_EPS = 64e-5
CHUNK = 64
VMEM_LIMIT = 58 * 1024 * 1024

SB_BLOCK = 256
LOG2E = 1.4426950408889634
SB_DEAD = 110.0 * LOG2E
SB_CLAMP = 120.0
SB_LAG = 4


def _cparams(sem):
    return pltpu.CompilerParams(dimension_semantics=sem, vmem_limit_bytes=VMEM_LIMIT)


def _dot(a, b):
    return jnp.dot(a, b, preferred_element_type=F32)


def _dot_nt(a, b):
    return lax.dot_general(a, b, (((1,), (1,)), ((), ())), preferred_element_type=F32)


def _dot_tn(a, b):
    return lax.dot_general(a, b, (((0,), (0,)), ((), ())), preferred_element_type=F32)


def _split2(x):
    hi = x.astype(BF16)
    lo = (x - hi.astype(F32)).astype(BF16)
    return hi, lo


def _split3(x):
    hi = x.astype(BF16)
    r1 = x - hi.astype(F32)
    mid = r1.astype(BF16)
    lo = (r1 - mid.astype(F32)).astype(BF16)
    return hi, mid, lo


def _dot_x3(a, w_hi, w_lo):
    a_hi, a_lo = _split2(a)
    return _dot(a_hi, w_hi) + (_dot(a_lo, w_hi) + _dot(a_hi, w_lo))


def _dot_exact_lhs(m_bf16, x):
    hi, mid, lo = _split3(x)
    return _dot(m_bf16, hi) + (_dot(m_bf16, mid) + _dot(m_bf16, lo))


def _rms(x, gain):
    return x * lax.rsqrt(jnp.mean(x * x, axis=-1, keepdims=True) + EPS) * gain


def _pair_head_sum(x, low_half):
    s_all = jnp.sum(x, axis=-1, keepdims=True)
    s_low = jnp.sum(jnp.where(low_half, x, 0.0), axis=-1, keepdims=True)
    return jnp.where(low_half, s_low, s_all - s_low)


def _inproj_kernel(x_ref, g_ref, wsb_ref, wrw_ref, wgl_ref, sb_ref, rw_ref, gl_ref):
    h = _rms(x_ref[...], g_ref[...]).astype(BF16)
    col = lax.broadcasted_iota(jnp.int32, (1, 3 * SB_WIDTH), 1)
    qscale = jnp.where(col < SB_WIDTH, LOG2E * HEAD_DIM ** -0.5, 1.0)
    sb_ref[...] = (_dot(h, wsb_ref[...]) * qscale).astype(BF16)
    rw_ref[...] = _dot(h, wrw_ref[...])
    gl_ref[...] = _dot(h, wgl_ref[...])


INPROJ_ROWS = 1024


def _inproj(x2, gain, w_sb, w_rw, w_gl, tm=INPROJ_ROWS):
    n = x2.shape[0]
    full = lambda w: pl.BlockSpec(w.shape, lambda i: (0, 0), pipeline_mode=pl.Buffered(1))
    row = lambda width: pl.BlockSpec((tm, width), lambda i: (i, 0))
    return pl.pallas_call(
        _inproj_kernel,
        grid=(n // tm,),
        in_specs=[row(D_MODEL), full(gain), full(w_sb), full(w_rw), full(w_gl)],
        out_specs=[row(3 * SB_WIDTH), row(RW_IN_WIDTH), row(GLA_PAD_WIDTH)],
        out_shape=[jax.ShapeDtypeStruct((n, 3 * SB_WIDTH), BF16),
                   jax.ShapeDtypeStruct((n, RW_IN_WIDTH), F32),
                   jax.ShapeDtypeStruct((n, GLA_PAD_WIDTH), F32)],
        compiler_params=_cparams(("parallel",)),
        name="in_proj",
    )(x2, gain, w_sb, w_rw, w_gl)


def _sb_kernel(q_ref, k_ref, v_ref, g_ref, o_ref):
    qb = SB_BLOCK
    qi = pl.program_id(1)
    lane = lax.broadcasted_iota(jnp.int32, (1, LANES), 1)
    low_half = lane < HEAD_DIM
    row = lax.broadcasted_iota(jnp.int32, (qb, qb), 0)
    col = lax.broadcasted_iota(jnp.int32, (qb, qb), 1)
    causal = col < row
    suffix = (row > col).astype(BF16)
    half = qb // 2
    pairs = SB_WIDTH // LANES
    zero = jnp.zeros((), BF16)
    vmasks = (low_half, jnp.logical_not(low_half))
    chains = [(h, r) for r in range(2) for h in range(2)]
    lanes_of = [slice(pr * LANES, (pr + 1) * LANES) for pr in range(pairs)]
    qcs = []
    for sl in lanes_of:
        q = q_ref[:, sl]
        qhs = [jnp.where(m, q, zero) for m in vmasks]
        qcs.append([qhs[h][r * half:(r + 1) * half, :] for h, r in chains])

    def front(qc, kb, n, mask):
        return front_tail(_dot_nt(qc, kb[0:n, :]), n, mask)

    def front_tail(z, n, mask):
        p = jnp.maximum(z, jnp.log2(1.0 + jnp.exp2(jnp.minimum(z, SB_CLAMP))))
        log_beta = z - p
        if mask is not None:
            p = jnp.where(mask, p, 0.0)
        later = _dot(p.astype(BF16), suffix[0:n, 0:n])
        return log_beta, p[:, 0:1], later

    def back(log_beta, p_first, later, spent, mask):
        w = jnp.exp2(log_beta - (later + spent))
        if mask is not None:
            w = jnp.where(mask, w, 0.0)
        return w.astype(BF16), spent + (later[:, 0:1] + p_first)

    def values(vb, n):
        return [jnp.where(m, vb[0:n, :], zero) for m in vmasks]

    d_start = pl.multiple_of(qi * qb, qb)
    p_start = pl.multiple_of(jnp.maximum(qi - 1, 0) * qb, qb)
    d_keys = [half if r == 0 else qb for _, r in chains]
    d_masks = [causal[0:half, 0:half] if r == 0 else causal[half:, :] for _, r in chains]
    seq_chains = []
    for pr in range(pairs):
        for prev in (False, True):
            for ci, (h, r) in enumerate(chains):
                seq_chains.append((pr, prev, ci, h, r))
    n_seq = len(seq_chains)
    p_vmasks = [jnp.logical_and(m, qi > 0) for m in vmasks]

    logits, fronts, spent_of = [None] * n_seq, [None] * n_seq, {}
    acc_of = {}
    def span(prev, r):
        if prev:
            return (0, qb) if r == 0 else (half, half)
        return (0, half) if r == 0 else (0, qb)

    for t in range(n_seq + 2 * SB_LAG):
        c2, c1 = t - 2 * SB_LAG, t - SB_LAG
        if 0 <= c2 < n_seq:
            pr, prev, ci, h, r = seq_chains[c2]
            k0, n = span(prev, r)
            mask = None if prev else d_masks[ci]
            spent = spent_of[(pr, ci)] if prev else jnp.zeros((half, 1), F32)
            log_beta, p_first, later = fronts[c2]
            fronts[c2] = None
            w, spent_of[(pr, ci)] = back(log_beta, p_first, later, spent, mask)
            start = p_start if prev else d_start
            vb = v_ref[pl.ds(start, qb), lanes_of[pr]]
            vh = jnp.where(p_vmasks[h] if prev else vmasks[h], vb[k0:k0 + n, :], zero)
            part = _dot(w, vh)
            acc_of[(pr, r)] = part if (pr, r) not in acc_of else acc_of[(pr, r)] + part
        if 0 <= c1 < n_seq:
            pr, prev, ci, h, r = seq_chains[c1]
            mask = None if prev else d_masks[ci]
            fronts[c1] = front_tail(logits[c1], span(prev, r)[1], mask)
            logits[c1] = None
        if t < n_seq:
            pr, prev, ci, h, r = seq_chains[t]
            k0, n = span(prev, r)
            start = p_start if prev else d_start
            kb = k_ref[pl.ds(start, qb), lanes_of[pr]]
            logits[t] = _dot_nt(qcs[pr][ci], kb[k0:k0 + n, :])
    accs_all = [[acc_of[(pr, r)] for r in range(2)] for pr in range(pairs)]
    bp_spent = [[spent_of[(pr, ci)] for ci in range(len(chains))] for pr in range(pairs)]

    outs = []
    late = [ci for ci, (_, r) in enumerate(chains) if r == 1]
    for pr, sl in enumerate(lanes_of):
        accs = accs_all[pr]

        def far_half(args, pr=pr, sl=sl):
            acc, spents = args
            kb, vb = k_ref[pl.ds(p_start, half), sl], v_ref[pl.ds(p_start, half), sl]
            fs = [front(qcs[pr][ci], kb, half, None) for ci in late]
            bs = [back(*f, sp, None) for f, sp in zip(fs, spents)]
            vh = jnp.concatenate(values(vb, half), axis=0)
            return acc + _dot(jnp.concatenate([b[0] for b in bs], axis=1), vh), [b[1] for b in bs]

        late_spent = [bp_spent[pr][ci] for ci in late]
        needed = jnp.logical_and(
            qi > 0, jnp.minimum(jnp.min(late_spent[0]), jnp.min(late_spent[1])) < SB_DEAD)
        acc_late, late_spent = lax.cond(needed, far_half, lambda args: args, (accs[1], late_spent))
        accs = [accs[0], acc_late]
        for ci, sp in zip(late, late_spent):
            bp_spent[pr][ci] = sp

        def cond(s):
            j, spents, _ = s
            alive = jnp.minimum(jnp.minimum(jnp.min(spents[0]), jnp.min(spents[1])),
                                jnp.minimum(jnp.min(spents[2]), jnp.min(spents[3])))
            return jnp.logical_and(j >= 0, alive < SB_DEAD)

        def body(s, pr=pr, sl=sl):
            j, spents, accs = s
            start = pl.multiple_of(j * qb, qb)
            kb, vb = k_ref[pl.ds(start, qb), sl], v_ref[pl.ds(start, qb), sl]
            fs = [front(qc, kb, qb, None) for qc in qcs[pr]]
            bs = [back(*f, sp, None) for f, sp in zip(fs, spents)]
            vh = jnp.concatenate(values(vb, qb), axis=0)
            accs = [acc + _dot(jnp.concatenate([bs[2 * r][0], bs[2 * r + 1][0]], axis=1), vh)
                    for r, acc in enumerate(accs)]
            return j - 1, [b[1] for b in bs], accs

        _, _, accs = lax.while_loop(cond, body, (qi - 2, bp_spent[pr], accs))
        acc = jnp.concatenate(accs, axis=0)
        ms = _pair_head_sum(acc * acc, low_half) * (1.0 / HEAD_DIM)
        outs.append(acc * lax.rsqrt(ms + EPS))
    o_ref[...] = (jnp.concatenate(outs, axis=1) * g_ref[...]).astype(o_ref.dtype)


def _sb_attention(sb, gain, batch, seq):
    nq = seq // SB_BLOCK
    n = sb.shape[0]
    pairs = SB_WIDTH // LANES
    return pl.pallas_call(
        _sb_kernel,
        grid=(batch, nq),
        in_specs=[
            pl.BlockSpec((SB_BLOCK, SB_WIDTH), lambda b, i: (b * nq + i, 0)),
            pl.BlockSpec((seq, SB_WIDTH), lambda b, i: (b, 1)),
            pl.BlockSpec((seq, SB_WIDTH), lambda b, i: (b, 2)),
            pl.BlockSpec((1, SB_WIDTH), lambda b, i: (0, 0)),
        ],
        out_specs=pl.BlockSpec((SB_BLOCK, SB_WIDTH), lambda b, i: (b * nq + i, 0)),
        out_shape=jax.ShapeDtypeStruct((n, SB_WIDTH), BF16),
        compiler_params=_cparams(("parallel", "arbitrary")),
        name="sb_attention",
    )(sb, sb, sb, gain)


RW_BLOCK = 256
RW_GROUP = 4


def _rw_body(x_ref, mu_ref, w0_ref, a0_ref, kk_ref, ka_ref, rk_ref, gng_ref, gnb_ref,
             uphi_ref, uplo_ref, o_ref,
             state_s, prev_s, r_s, k_s, v_s, lw_s, a_s, b_s, y_s, lc_s,
             rp_s, yp_s, p_s, q_s, ge_s, bonus_s, gate_s, filler=None):
    tb = RW_BLOCK
    hb = tb // 2
    c = CHUNK
    pairs = RW_WIDTH // LANES

    lane = lax.broadcasted_iota(jnp.int32, (1, LANES), 1)
    low_half = lane < HEAD_DIM
    tr = lax.broadcasted_iota(jnp.int32, (hb, hb), 0)
    tc = lax.broadcasted_iota(jnp.int32, (hb, hb), 1)
    tri = jnp.logical_and(tr // c == tc // c, tc <= tr).astype(BF16)
    first = lax.broadcasted_iota(jnp.int32, (hb, 1), 0) == 0

    def prologue(part):
        rows = slice(part * hb, (part + 1) * hb)
        x = x_ref[rows, :]
        before = prev_s[...] if part == 0 else x_ref[part * hb - 1:part * hb, :]
        prev = jnp.where(first, before, pltpu.roll(x, 1, 0))
        xs = x + (prev - x) * mu_ref[...]
        yield
        r = xs[:, 0:RW_WIDTH]
        k = xs[:, RW_WIDTH:2 * RW_WIDTH]
        v = xs[:, 2 * RW_WIDTH:3 * RW_WIDTH]
        code = xs[:, 3 * RW_WIDTH:]
        act = jnp.where(lane < 32, jnp.tanh(code), jnp.where(lane < 64, code, jax.nn.sigmoid(code)))
        up = jnp.concatenate(
            [_dot_x3(act, uphi_ref[:, 0:RW_WIDTH], uplo_ref[:, 0:RW_WIDTH]),
             _dot(act.astype(BF16), uphi_ref[:, RW_WIDTH:])], axis=1)
        r_s[rows, :] = r
        v_s[rows, :] = v
        gate_s[rows, :] = up[:, 2 * RW_WIDTH:]
        yield
        wpre = w0_ref[...] + up[:, 0:RW_WIDTH]
        log_w = -(jnp.maximum(-wpre, 0.0) + jnp.log(1.0 + jnp.exp(-jnp.abs(wpre)))) - 0.5
        lw = -jnp.exp(log_w)
        lw_s[rows, :] = lw
        yield
        lc_s[rows, :] = _dot_exact_lhs(tri, lw)
        yield
        a = jax.nn.sigmoid(a0_ref[...] + up[:, RW_WIDTH:2 * RW_WIDTH])
        kmod = k * (1.0 + (a - 1.0) * ka_ref[...])
        k_s[rows, :] = kmod
        yield
        kk = k * kk_ref[...]
        bonus_in = r * kmod * rk_ref[...]
        for p in range(pairs):
            sl = slice(p * LANES, (p + 1) * LANES)
            ss = _pair_head_sum(kk[:, sl] * kk[:, sl], low_half)
            kkn = kk[:, sl] * lax.rsqrt(jnp.maximum(ss, 1e-12))
            a_s[rows, sl] = -kkn
            b_s[rows, sl] = kkn * a[:, sl]
            bonus_s[rows, sl] = _pair_head_sum(bonus_in[:, sl], low_half) * v[:, sl]
            yield

    rr = lax.broadcasted_iota(jnp.int32, (2 * c, 2 * c), 0)
    cc = lax.broadcasted_iota(jnp.int32, (2 * c, 2 * c), 1)
    same_head = (rr // c) == (cc // c)
    strict = jnp.logical_and(same_head, cc < rr)
    incl = jnp.logical_and(same_head, cc <= rr)
    m_lo = low_half
    m_hi = jnp.logical_not(low_half)

    def stack(xp):
        return jnp.concatenate([jnp.where(m_lo, xp, 0.0), jnp.where(m_hi, xp, 0.0)], axis=0)

    def prepare(chunks, tick):
        chains = [(ci, p) for ci in chunks for p in range(pairs)]
        pre = []
        for ci, p in chains:
            rows = slice(ci * c, (ci + 1) * c)
            sl = slice(p * LANES, (p + 1) * LANES)
            lwc = lw_s[rows, sl]
            lc = lc_s[rows, sl]
            e_incl = jnp.exp(lc)
            e_excl = jnp.exp(lc - lwc)
            e_inv = jnp.exp(-lc)
            g_end = e_incl[c - 1:c, :]
            ag = stack(a_s[rows, sl] * e_excl).astype(BF16)
            rg = stack(r_s[rows, sl] * e_incl).astype(BF16)
            bd = b_s[rows, sl] * e_inv
            kd = k_s[rows, sl] * e_inv
            vst = stack(v_s[rows, sl]).astype(BF16)
            lhs = jnp.concatenate([ag, rg], axis=0)
            rhs = jnp.concatenate([bd, bd, kd, kd], axis=0).astype(BF16)
            bde = stack(bd * g_end).astype(BF16)
            kde = stack(kd * g_end).astype(BF16)
            idx = ci * pairs + p
            ge_s[idx] = g_end
            pre.append((idx, ag, rg, vst, lhs, rhs, kde, bde))
        tick()
        gs = [_dot_nt(t[4], t[5]) for t in pre]
        tick()
        npows, xs_, avs, arbs = [], [], [], []
        for t, g in zip(pre, gs):
            a_ab = jnp.where(strict, g[0:2 * c, 0:2 * c], 0.0)
            a_ak = jnp.where(strict, g[0:2 * c, 2 * c:], 0.0)
            a_rb = jnp.where(incl, g[2 * c:, 0:2 * c], 0.0)
            a_rk = jnp.where(incl, g[2 * c:, 2 * c:], 0.0)
            arbs.append(a_rb.astype(BF16))
            npows.append(a_ab)
            avs.append(jnp.concatenate([a_ak, a_rk], axis=0).astype(BF16))
        avs = [_dot(av, t[3]) for av, t in zip(avs, pre)]
        svs = [_dot_tn(t[3], t[6]) for t in pre]
        tick()
        for t, av in zip(pre, avs):
            xs_.append(jnp.concatenate([t[1].astype(F32), av[0:2 * c, :]], axis=1))
        for it in range(6):
            nbs = [n.astype(BF16) for n in npows]
            xbs = [xc.astype(BF16) for xc in xs_]
            if it < 5:
                prods = [_dot(nb, jnp.concatenate([nb, xb], axis=1)) for nb, xb in zip(nbs, xbs)]
                npows = [pr[:, 0:LANES] for pr in prods]
                xs_ = [xc + pr[:, LANES:] for xc, pr in zip(xs_, prods)]
            else:
                xs_ = [xc + _dot(nb, xb) for xc, nb, xb in zip(xs_, nbs, xbs)]
            tick()
        xbs = [xc.astype(BF16) for xc in xs_]
        ras = [_dot(arb, xb) for arb, xb in zip(arbs, xbs)]
        pqs = [_dot_tn(xb, t[7]) for xb, t in zip(xbs, pre)]
        for t, av, sv, ra, pq in zip(pre, avs, svs, ras, pqs):
            idx = t[0]
            rp_s[idx] = (t[2].astype(F32) + ra[:, 0:LANES]).astype(BF16)
            yp_s[idx] = ra[:, LANES:] + av[2 * c:, :]
            p_s[idx] = pq[0:LANES, :].astype(BF16)
            q_s[idx] = pq[LANES:, :] + sv
        tick()

    half_chunks = hb // c
    states = [state_s[p] for p in range(pairs)]

    def finish(part):
        for ci in range(part * half_chunks, (part + 1) * half_chunks):
            for p in range(pairs):
                idx = ci * pairs + p
                sb = states[p].astype(BF16)
                y = _dot_nt(rp_s[idx], sb) + yp_s[idx]
                y_s[ci * c:(ci + 1) * c, p * LANES:(p + 1) * LANES] = y[0:c, :] + y[c:2 * c, :]
                states[p] = states[p] * ge_s[idx] + _dot(sb, p_s[idx]) + q_s[idx]
            yield
        rows = slice(part * hb, (part + 1) * hb)
        for p in range(pairs):
            sl = slice(p * LANES, (p + 1) * LANES)
            y = y_s[rows, sl]
            mean = _pair_head_sum(y, low_half) * (1.0 / HEAD_DIM)
            d = y - mean
            var = _pair_head_sum(d * d, low_half) * (1.0 / HEAD_DIM)
            yn = d * lax.rsqrt(var + RW_GN_EPS) * gng_ref[:, sl] + gnb_ref[:, sl]
            o_ref[rows, sl] = ((yn + bonus_s[rows, sl]) * gate_s[rows, sl]).astype(o_ref.dtype)
            yield

    def advance(*gens):
        def tick():
            for gen in gens:
                if gen is not None:
                    next(gen, None)
        return tick

    def drain(gen):
        for _ in gen:
            pass

    drain(prologue(0))
    second = prologue(1)
    prepare(range(half_chunks), advance(second, filler))
    drain(second)
    prev_s[...] = x_ref[tb - 1:tb, :]
    first_half = finish(0)
    prepare(range(half_chunks, 2 * half_chunks), advance(first_half, filler))
    drain(first_half)
    drain(finish(1))
    for p in range(pairs):
        state_s[p] = states[p]


RW_N_IN, RW_N_SCRATCH = 11, 17
GLA_N_IN, GLA_N_SCRATCH = 5, 5


def _recurrent_kernel(*refs):
    n_in = RW_N_IN + GLA_N_IN
    rw_in, gl_in = refs[:RW_N_IN], refs[RW_N_IN:n_in]
    o_rw, o_gl = refs[n_in], refs[n_in + 1]
    rw_scr = refs[n_in + 2:n_in + 2 + RW_N_SCRATCH]
    gl_scr = refs[n_in + 2 + RW_N_SCRATCH:]

    @pl.when(pl.program_id(1) == 0)
    def _():
        for state in (rw_scr[0], rw_scr[1], gl_scr[0]):
            state[...] = jnp.zeros_like(state)

    gla = _gla_body(*gl_in, o_gl, *gl_scr)
    _rw_body(*rw_in, o_rw, *rw_scr, filler=gla)
    for _ in gla:
        pass


def _recurrent_mixers(rw, gl, rprm, gprm, batch, seq):
    n = rw.shape[0]
    tb = RW_BLOCK
    nb = seq // tb
    const = lambda shape: pl.BlockSpec(shape, lambda b, i: (0,) * len(shape))
    rows = lambda width: pl.BlockSpec((tb, width), lambda b, i: (b * nb + i, 0))
    vec = const((1, RW_WIDTH))
    up = const((RW_CODE, 3 * RW_WIDTH))
    blk = lambda: pltpu.VMEM((tb, RW_WIDTH), F32)
    n_chain = (tb // CHUNK) * (RW_WIDTH // LANES)
    tile = lambda r, dt: pltpu.VMEM((n_chain, r, LANES), dt)
    rw_scratch = [pltpu.VMEM((RW_WIDTH // LANES, LANES, LANES), F32),
                  pltpu.VMEM((1, RW_IN_WIDTH), F32),
                  blk(), blk(), blk(), blk(), blk(), blk(), blk(), blk(),
                  tile(LANES, BF16), tile(LANES, F32), tile(LANES, BF16), tile(LANES, F32),
                  tile(1, F32), blk(), blk()]
    gl_scratch = [pltpu.VMEM((GLA_VAL_WIDTH, GLA_KEY_WIDTH), F32),
                  pltpu.VMEM((tb, GLA_KEY_WIDTH), F32),
                  pltpu.VMEM((tb, GLA_KEY_WIDTH), F32),
                  pltpu.VMEM((tb, GLA_KEY_WIDTH), F32),
                  pltpu.VMEM((tb, GLA_VAL_WIDTH), F32)]
    assert len(rw_scratch) == RW_N_SCRATCH and len(gl_scratch) == GLA_N_SCRATCH
    return pl.pallas_call(
        _recurrent_kernel,
        grid=(batch, nb),
        in_specs=[rows(RW_IN_WIDTH), const((1, RW_IN_WIDTH)),
                  vec, vec, vec, vec, vec, vec, vec, up, up,
                  rows(GLA_PAD_WIDTH), const((1, GLA_KEY_WIDTH)), const((1, GLA_VAL_WIDTH)),
                  const((LANES, GLA_KEY_WIDTH)), const((LANES, GLA_KEY_WIDTH))],
        out_specs=[rows(RW_WIDTH), rows(GLA_VAL_WIDTH)],
        out_shape=[jax.ShapeDtypeStruct((n, RW_WIDTH), BF16),
                   jax.ShapeDtypeStruct((n, GLA_VAL_WIDTH), BF16)],
        scratch_shapes=rw_scratch + gl_scratch,
        compiler_params=_cparams(("parallel", "arbitrary")),
        name="rwkv7_gla",
    )(rw, rprm["mu"], rprm["w0"], rprm["a0"], rprm["k_k"], rprm["k_a"], rprm["r_k"],
      rprm["gn_g"], rprm["gn_b"], rprm["up_hi"], rprm["up_lo"],
      gl, gprm["gate_b"], gprm["norm_g"], gprm["up_hi"], gprm["up_lo"])


GLA_BLOCK = 256
GLA_SUB = 16
GLA_EXP_CAP = 80.0


def _gla_body(x_ref, gb_ref, ng_ref, uphi_ref, uplo_ref, o_ref, state_s, q_s, k_s, la_s, o_s):
    tb = GLA_BLOCK
    c = CHUNK
    heads = GLA_VAL_WIDTH // HEAD_DIM

    kw, vw = GLA_KEY_WIDTH, GLA_VAL_WIDTH
    code = x_ref[:, 2 * kw + 2 * vw:]
    pre = _dot_x3(code, uphi_ref[...], uplo_ref[...]) + gb_ref[...]
    la_s[...] = -(jnp.maximum(-pre, 0.0) + jnp.log(1.0 + jnp.exp(-jnp.abs(pre)))) * (1.0 / GLA_GATE_NORMALIZER)
    q_s[...] = x_ref[:, 0:kw] * (GLA_DK ** -0.5)
    k_s[...] = x_ref[:, kw:2 * kw]
    yield

    tr = lax.broadcasted_iota(jnp.int32, (c, c), 0)
    tc = lax.broadcasted_iota(jnp.int32, (c, c), 1)
    tri = (tc <= tr).astype(BF16)
    sub = GLA_SUB
    n_sub = c // sub
    srow = lax.broadcasted_iota(jnp.int32, (heads * c, c), 0)
    scol = lax.broadcasted_iota(jnp.int32, (heads * c, c), 1)
    causal = scol <= (srow // (heads * sub)) * sub + srow % sub
    klane = lax.broadcasted_iota(jnp.int32, (1, kw), 1) // GLA_DK
    vlane = lax.broadcasted_iota(jnp.int32, (1, vw), 1) // HEAD_DIM
    st_r = lax.broadcasted_iota(jnp.int32, (vw, kw), 0) // HEAD_DIM
    st_c = lax.broadcasted_iota(jnp.int32, (vw, kw), 1) // GLA_DK
    st_mask = st_r == st_c

    n_chunk = tb // c
    pre = []
    for ci in range(n_chunk):
        rows = slice(ci * c, (ci + 1) * c)
        bc = _dot_exact_lhs(tri, la_s[rows, :])
        e_pos = jnp.exp(bc)
        qc = q_s[rows, :]
        kc = k_s[rows, :]
        k2 = (kc * jnp.exp(bc[c - 1:c, :] - bc)).astype(BF16)
        vb = x_ref[rows, 2 * kw:2 * kw + vw].astype(BF16)
        qsts, kes = [], []
        for i in range(n_sub):
            srows = slice(i * sub, (i + 1) * sub)
            ref = bc[i * sub - 1:i * sub, :] if i > 0 else jnp.zeros((1, kw), F32)
            qi = qc[srows, :] * jnp.exp(bc[srows, :] - ref)
            qsts.append(jnp.concatenate([jnp.where(klane == h, qi, 0.0) for h in range(heads)],
                                        axis=0).astype(BF16))
            kes.append((kc * jnp.exp(jnp.minimum(ref - bc, GLA_EXP_CAP))).astype(BF16))
        pre.append(((qc * e_pos).astype(BF16), k2, vb, qsts, kes, e_pos[c - 1:c, :]))
        yield
    scores = []
    for t in pre:
        scores.append(jnp.where(
            causal, jnp.concatenate([_dot_nt(qs, ke) for qs, ke in zip(t[3], t[4])], axis=0),
            0.0).astype(BF16))
        yield
    o_sts = [_dot(sc, t[2]) for sc, t in zip(scores, pre)]
    incs = [jnp.where(st_mask, _dot_tn(t[2], t[1]), 0.0) for t in pre]
    yield
    state = state_s[...]
    for ci in range(n_chunk):
        intra = []
        for i in range(n_sub):
            acc = None
            for h in range(heads):
                r0 = (i * heads + h) * sub
                part = jnp.where(vlane == h, o_sts[ci][r0:r0 + sub, :], 0.0)
                acc = part if acc is None else acc + part
            intra.append(acc)
        o_s[ci * c:(ci + 1) * c, :] = _dot_nt(pre[ci][0], state.astype(BF16)) + jnp.concatenate(intra, axis=0)
        state = state * pre[ci][5] + incs[ci]
        yield
    state_s[...] = state

    lane = lax.broadcasted_iota(jnp.int32, (1, LANES), 1)
    low_half = lane < HEAD_DIM
    outs = []
    for p in range(vw // LANES):
        o = o_s[:, p * LANES:(p + 1) * LANES]
        ms = _pair_head_sum(o * o, low_half) * (1.0 / HEAD_DIM)
        outs.append(o * lax.rsqrt(ms + EPS))
    g = x_ref[:, 2 * kw + vw:2 * kw + 2 * vw]
    on = jnp.concatenate(outs, axis=1) * ng_ref[...]
    o_ref[...] = (on * (g * jax.nn.sigmoid(g))).astype(o_ref.dtype)


MLP_ROWS = 1024
MLP_FF_TILE = 1024


def _mlp_kernel(x_ref, sb_ref, rw_ref, gl_ref, wsb_ref, wrw_ref, wgl_ref, gmix_ref,
                gpre_ref, w1_ref, w2_ref, gpost_ref, o_ref):
    mixed = _dot(sb_ref[...], wsb_ref[...]) + _dot(rw_ref[...], wrw_ref[...]) + _dot(gl_ref[...], wgl_ref[...])
    x1 = x_ref[...] + _rms(mixed, gmix_ref[...])
    h = _rms(x1, gpre_ref[...]).astype(BF16)
    ff = None
    for j in range(D_FF // MLP_FF_TILE):
        cols = slice(j * MLP_FF_TILE, (j + 1) * MLP_FF_TILE)
        a = jnp.maximum(_dot(h, w1_ref[:, cols]), 0.0)
        part = _dot((a * a).astype(BF16), w2_ref[cols, :])
        ff = part if ff is None else ff + part
    o_ref[...] = x1 + _rms(ff, gpost_ref[...])


def _mlp(x2, o_sb, o_rw, o_gl, w_sb, w_rw, w_gl, g_mix, g_pre, w1, w2, g_post):
    n = x2.shape[0]
    tm = MLP_ROWS
    full = lambda w: pl.BlockSpec(w.shape, lambda i: (0, 0), pipeline_mode=pl.Buffered(1))
    row = lambda width: pl.BlockSpec((tm, width), lambda i: (i, 0))
    return pl.pallas_call(
        _mlp_kernel,
        grid=(n // tm,),
        in_specs=[row(D_MODEL), row(SB_WIDTH), row(RW_WIDTH), row(GLA_VAL_WIDTH),
                  full(w_sb), full(w_rw), full(w_gl), full(g_mix),
                  full(g_pre), full(w1), full(w2), full(g_post)],
        out_specs=row(D_MODEL),
        out_shape=jax.ShapeDtypeStruct((n, D_MODEL), F32),
        compiler_params=_cparams(("parallel",)),
        name="out_proj_mlp",
    )(x2, o_sb, o_rw, o_gl, w_sb, w_rw, w_gl, g_mix, g_pre, w1, w2, g_post)


def _hi_lo(w):
    hi = w.astype(BF16)
    return hi, (w - hi.astype(F32)).astype(BF16)


def _prep_layer(l, w_in, rw_w_up, rw_a_up, rw_g_up, gla_gate_up, w_out, w_ff1, w_ff2):
    w = w_in[l]
    sb_end = 3 * SB_WIDTH
    rw_end = sb_end + RW_IN_WIDTH
    w_sb = w[:, :sb_end].astype(BF16)
    w_rw = w[:, sb_end:rw_end].astype(BF16)
    g0 = rw_end
    kw, vw, gr = GLA_KEY_WIDTH, GLA_VAL_WIDTH, GLA_GATE_RANK
    w_gl = jnp.concatenate([
        w[:, g0:g0 + 2 * kw + vw],
        w[:, g0 + 2 * kw + vw + gr:g0 + GLA_IN_WIDTH],
        w[:, g0 + 2 * kw + vw:g0 + 2 * kw + vw + gr],
        jnp.zeros((D_MODEL, GLA_PAD_WIDTH - GLA_IN_WIDTH), F32)], axis=1).astype(BF16)
    up = jnp.zeros((RW_CODE, 3 * RW_WIDTH), F32)
    up = up.at[0:32, 0:RW_WIDTH].set(rw_w_up[l])
    up = up.at[32:64, RW_WIDTH:2 * RW_WIDTH].set(rw_a_up[l])
    up = up.at[64:128, 2 * RW_WIDTH:].set(rw_g_up[l])
    rw_up_hi, rw_up_lo = _hi_lo(up)
    gup = jnp.zeros((LANES, GLA_KEY_WIDTH), F32).at[0:gr, :].set(gla_gate_up[l])
    gl_up_hi, gl_up_lo = _hi_lo(gup)
    wo = w_out[l].astype(BF16)
    return dict(w_sb=w_sb, w_rw=w_rw, w_gl=w_gl, rw_up_hi=rw_up_hi, rw_up_lo=rw_up_lo,
                gl_up_hi=gl_up_hi, gl_up_lo=gl_up_lo,
                wo_sb=wo[:SB_WIDTH], wo_rw=wo[SB_WIDTH:SB_WIDTH + RW_WIDTH], wo_gl=wo[SB_WIDTH + RW_WIDTH:],
                w1=w_ff1[l].astype(BF16), w2=w_ff2[l].astype(BF16))


def kernel(x, pre_mix_g, w_in, sb_norm_g, rw_mu, rw_w0, rw_w_up, rw_a0, rw_a_up, rw_g_up, rw_k_k, rw_k_a, rw_r_k, rw_gn_g, rw_gn_b, gla_gate_up, gla_gate_b, gla_norm_g, w_out, post_mix_g, pre_ffn_g, w_ff1, w_ff2, post_ffn_g):
    batch, seq, d = x.shape
    depth = w_in.shape[0]
    x2 = x.reshape(batch * seq, d)
    row = lambda t, l: t[l][None, :]
    for l in range(depth):
        wp = _prep_layer(l, w_in, rw_w_up, rw_a_up, rw_g_up, gla_gate_up, w_out, w_ff1, w_ff2)
        sb, rw, gl = _inproj(x2, row(pre_mix_g, l), wp["w_sb"], wp["w_rw"], wp["w_gl"])
        o_sb = _sb_attention(sb, row(sb_norm_g, l), batch, seq)
        o_rw, o_gl = _recurrent_mixers(
            rw, gl,
            dict(mu=row(rw_mu, l), w0=row(rw_w0, l), a0=row(rw_a0, l), k_k=row(rw_k_k, l),
                 k_a=row(rw_k_a, l), r_k=row(rw_r_k, l), gn_g=row(rw_gn_g, l),
                 gn_b=row(rw_gn_b, l), up_hi=wp["rw_up_hi"], up_lo=wp["rw_up_lo"]),
            dict(gate_b=row(gla_gate_b, l), norm_g=row(gla_norm_g, l),
                 up_hi=wp["gl_up_hi"], up_lo=wp["gl_up_lo"]), batch, seq)
        x2 = _mlp(x2, o_sb, o_rw, o_gl, wp["wo_sb"], wp["wo_rw"], wp["wo_gl"], row(post_mix_g, l),
                  row(pre_ffn_g, l), wp["w1"], wp["w2"], row(post_ffn_g, l))
    return x2.reshape(batch, seq, d)
```

```python
import functools

import jax
import jax.numpy as jnp
from jax import lax
from jax.experimental import pallas as pl
from jax.experimental.pallas import tpu as pltpu

F32 = jnp.float32
BF16 = jnp.bfloat16

D_MODEL = 1024
HEAD_DIM = 64
LANES = 128
SB_WIDTH = 384
RW_WIDTH = 384
RW_IN_WIDTH = 1280
RW_CODE = 128
GLA_KEY_WIDTH = 128
GLA_VAL_WIDTH = 256
GLA_DK = 32
GLA_GATE_RANK = 16
GLA_IN_WIDTH = 784
GLA_PAD_WIDTH = 896
GLA_GATE_NORMALIZER = 16.0
D_FF = 4096
EPS = 1e-6
RW_GN_EPS = 64e-5
CHUNK = 64
VMEM_LIMIT = 58 * 1024 * 1024

SB_BLOCK = 256
LOG2E = 1.4426950408889634
SB_DEAD = 110.0 * LOG2E
SB_CLAMP = 120.0
SB_LAG = 4


def _cparams(sem):
    return pltpu.CompilerParams(dimension_semantics=sem, vmem_limit_bytes=VMEM_LIMIT)


def _dot(a, b):
    return jnp.dot(a, b, preferred_element_type=F32)


def _dot_nt(a, b):
    return lax.dot_general(a, b, (((1,), (1,)), ((), ())), preferred_element_type=F32)


def _dot_tn(a, b):
    return lax.dot_general(a, b, (((0,), (0,)), ((), ())), preferred_element_type=F32)


def _split2(x):
    hi = x.astype(BF16)
    lo = (x - hi.astype(F32)).astype(BF16)
    return hi, lo


def _split3(x):
    hi = x.astype(BF16)
    r1 = x - hi.astype(F32)
    mid = r1.astype(BF16)
    lo = (r1 - mid.astype(F32)).astype(BF16)
    return hi, mid, lo


def _dot_x3(a, w_hi, w_lo):
    a_hi, a_lo = _split2(a)
    return _dot(a_hi, w_hi) + (_dot(a_lo, w_hi) + _dot(a_hi, w_lo))


def _dot_exact_lhs(m_bf16, x):
    hi, mid, lo = _split3(x)
    return _dot(m_bf16, hi) + (_dot(m_bf16, mid) + _dot(m_bf16, lo))


def _rms(x, gain):
    return x * lax.rsqrt(jnp.mean(x * x, axis=-1, keepdims=True) + EPS) * gain


def _pair_head_sum(x, low_half):
    s_all = jnp.sum(x, axis=-1, keepdims=True)
    s_low = jnp.sum(jnp.where(low_half, x, 0.0), axis=-1, keepdims=True)
    return jnp.where(low_half, s_low, s_all - s_low)


def _inproj_kernel(x_ref, g_ref, wsb_ref, wrw_ref, wgl_ref, sb_ref, rw_ref, gl_ref):
    h = _rms(x_ref[...], g_ref[...]).astype(BF16)
    col = lax.broadcasted_iota(jnp.int32, (1, 3 * SB_WIDTH), 1)
    qscale = jnp.where(col < SB_WIDTH, LOG2E * HEAD_DIM ** -0.5, 1.0)
    sb_ref[...] = (_dot(h, wsb_ref[...]) * qscale).astype(BF16)
    rw_ref[...] = _dot(h, wrw_ref[...])
    gl_ref[...] = _dot(h, wgl_ref[...])


INPROJ_ROWS = 1024


def _layer_block(arr, layer, grid_rank, single_buffer=False):
    index = lambda *_: (layer, 0, 0)
    assert arr.ndim == 3 and grid_rank in (1, 2)
    if single_buffer:
        return pl.BlockSpec((None,) + arr.shape[1:], index, pipeline_mode=pl.Buffered(1))
    return pl.BlockSpec((None,) + arr.shape[1:], index)


def _inproj(x2, layer, gain, w_sb, w_rw, w_gl, tm=INPROJ_ROWS):
    n = x2.shape[0]
    full = lambda w: _layer_block(w, layer, 1, single_buffer=True)
    row = lambda width: pl.BlockSpec((tm, width), lambda i: (i, 0))
    return pl.pallas_call(
        _inproj_kernel,
        grid=(n // tm,),
        in_specs=[row(D_MODEL), full(gain), full(w_sb), full(w_rw), full(w_gl)],
        out_specs=[row(3 * SB_WIDTH), row(RW_IN_WIDTH), row(GLA_PAD_WIDTH)],
        out_shape=[jax.ShapeDtypeStruct((n, 3 * SB_WIDTH), BF16),
                   jax.ShapeDtypeStruct((n, RW_IN_WIDTH), F32),
                   jax.ShapeDtypeStruct((n, GLA_PAD_WIDTH), F32)],
        compiler_params=_cparams(("parallel",)),
        name="in_proj",
    )(x2, gain, w_sb, w_rw, w_gl)


def _sb_kernel(q_ref, k_ref, v_ref, g_ref, o_ref):
    qb = SB_BLOCK
    qi = pl.program_id(1)
    lane = lax.broadcasted_iota(jnp.int32, (1, LANES), 1)
    low_half = lane < HEAD_DIM
    row = lax.broadcasted_iota(jnp.int32, (qb, qb), 0)
    col = lax.broadcasted_iota(jnp.int32, (qb, qb), 1)
    causal = col < row
    suffix = (row > col).astype(BF16)
    half = qb // 2
    pairs = SB_WIDTH // LANES
    zero = jnp.zeros((), BF16)
    vmasks = (low_half, jnp.logical_not(low_half))
    chains = [(h, r) for r in range(2) for h in range(2)]
    lanes_of = [slice(pr * LANES, (pr + 1) * LANES) for pr in range(pairs)]
    qcs = []
    for sl in lanes_of:
        q = q_ref[:, sl]
        qhs = [jnp.where(m, q, zero) for m in vmasks]
        qcs.append([qhs[h][r * half:(r + 1) * half, :] for h, r in chains])

    def front(qc, kb, n, mask):
        return front_tail(_dot_nt(qc, kb[0:n, :]), n, mask)

    def front_tail(z, n, mask):
        p = jnp.maximum(z, jnp.log2(1.0 + jnp.exp2(jnp.minimum(z, SB_CLAMP))))
        log_beta = z - p
        if mask is not None:
            p = jnp.where(mask, p, 0.0)
        later = _dot(p.astype(BF16), suffix[0:n, 0:n])
        return log_beta, p[:, 0:1], later

    def back(log_beta, p_first, later, spent, mask):
        w = jnp.exp2(log_beta - (later + spent))
        if mask is not None:
            w = jnp.where(mask, w, 0.0)
        return w.astype(BF16), spent + (later[:, 0:1] + p_first)

    def values(vb, n):
        return [jnp.where(m, vb[0:n, :], zero) for m in vmasks]

    d_start = pl.multiple_of(qi * qb, qb)
    p_start = pl.multiple_of(jnp.maximum(qi - 1, 0) * qb, qb)
    d_keys = [half if r == 0 else qb for _, r in chains]
    d_masks = [causal[0:half, 0:half] if r == 0 else causal[half:, :] for _, r in chains]
    seq_chains = []
    for pr in range(pairs):
        for prev in (False, True):
            for ci, (h, r) in enumerate(chains):
                seq_chains.append((pr, prev, ci, h, r))
    n_seq = len(seq_chains)
    p_vmasks = [jnp.logical_and(m, qi > 0) for m in vmasks]

    logits, fronts, spent_of = [None] * n_seq, [None] * n_seq, {}
    acc_of = {}
    def span(prev, r):
        if prev:
            return (0, qb) if r == 0 else (half, half)
        return (0, half) if r == 0 else (0, qb)

    for t in range(n_seq + 2 * SB_LAG):
        c2, c1 = t - 2 * SB_LAG, t - SB_LAG
        if 0 <= c2 < n_seq:
            pr, prev, ci, h, r = seq_chains[c2]
            k0, n = span(prev, r)
            mask = None if prev else d_masks[ci]
            spent = spent_of[(pr, ci)] if prev else jnp.zeros((half, 1), F32)
            log_beta, p_first, later = fronts[c2]
            fronts[c2] = None
            w, spent_of[(pr, ci)] = back(log_beta, p_first, later, spent, mask)
            start = p_start if prev else d_start
            vb = v_ref[pl.ds(start, qb), lanes_of[pr]]
            vh = jnp.where(p_vmasks[h] if prev else vmasks[h], vb[k0:k0 + n, :], zero)
            part = _dot(w, vh)
            acc_of[(pr, r)] = part if (pr, r) not in acc_of else acc_of[(pr, r)] + part
        if 0 <= c1 < n_seq:
            pr, prev, ci, h, r = seq_chains[c1]
            mask = None if prev else d_masks[ci]
            fronts[c1] = front_tail(logits[c1], span(prev, r)[1], mask)
            logits[c1] = None
        if t < n_seq:
            pr, prev, ci, h, r = seq_chains[t]
            k0, n = span(prev, r)
            start = p_start if prev else d_start
            kb = k_ref[pl.ds(start, qb), lanes_of[pr]]
            logits[t] = _dot_nt(qcs[pr][ci], kb[k0:k0 + n, :])
    accs_all = [[acc_of[(pr, r)] for r in range(2)] for pr in range(pairs)]
    bp_spent = [[spent_of[(pr, ci)] for ci in range(len(chains))] for pr in range(pairs)]

    late = [ci for ci, (_, r) in enumerate(chains) if r == 1]

    def least(spents):
        m = spents[0]
        for sp in spents[1:]:
            m = jnp.minimum(m, sp)
        return jnp.min(m)

    far_needed = [jnp.logical_and(qi > 0, least([bp_spent[pr][ci] for ci in late]) < SB_DEAD)
                  for pr in range(pairs)]
    more_needed = [jnp.logical_and(qi > 1, least(bp_spent[pr]) < SB_DEAD) for pr in range(pairs)]

    finals = []
    for pr, sl in enumerate(lanes_of):
        def far_half(args, pr=pr, sl=sl):
            acc, spents = args
            kb, vb = k_ref[pl.ds(p_start, half), sl], v_ref[pl.ds(p_start, half), sl]
            fs = [front(qcs[pr][ci], kb, half, None) for ci in late]
            bs = [back(*f, sp, None) for f, sp in zip(fs, spents)]
            vh = jnp.concatenate(values(vb, half), axis=0)
            return acc + _dot(jnp.concatenate([b[0] for b in bs], axis=1), vh), [b[1] for b in bs]

        def cond(s):
            j, spents, _ = s
            return jnp.logical_and(j >= 0, least(spents) < SB_DEAD)

        def body(s, pr=pr, sl=sl):
            j, spents, accs = s
            start = pl.multiple_of(j * qb, qb)
            kb, vb = k_ref[pl.ds(start, qb), sl], v_ref[pl.ds(start, qb), sl]
            fs = [front(qc, kb, qb, None) for qc in qcs[pr]]
            bs = [back(*f, sp, None) for f, sp in zip(fs, spents)]
            vh = jnp.concatenate(values(vb, qb), axis=0)
            accs = [acc + _dot(jnp.concatenate([bs[2 * r][0], bs[2 * r + 1][0]], axis=1), vh)
                    for r, acc in enumerate(accs)]
            return j - 1, [b[1] for b in bs], accs

        def sweep_on(args, pr=pr, far_half=far_half, cond=cond, body=body):
            accs, spents = args
            late_spent = [spents[ci] for ci in late]
            acc_late, late_spent = lax.cond(far_needed[pr], far_half, lambda a: a, (accs[1], late_spent))
            spents = list(spents)
            for ci, sp in zip(late, late_spent):
                spents[ci] = sp
            _, _, accs = lax.while_loop(cond, body, (qi - 2, spents, [accs[0], acc_late]))
            return accs, spents

        accs, _ = lax.cond(jnp.logical_or(far_needed[pr], more_needed[pr]), sweep_on, lambda a: a,
                           (accs_all[pr], bp_spent[pr]))
        finals.append(jnp.concatenate(accs, axis=0))

    outs = []
    for acc in finals:
        ms = _pair_head_sum(acc * acc, low_half) * (1.0 / HEAD_DIM)
        outs.append(acc * lax.rsqrt(ms + EPS))
    o_ref[...] = (jnp.concatenate(outs, axis=1) * g_ref[...]).astype(o_ref.dtype)


def _sb_attention(sb, layer, gain, batch, seq):
    nq = seq // SB_BLOCK
    n = sb.shape[0]
    pairs = SB_WIDTH // LANES
    return pl.pallas_call(
        _sb_kernel,
        grid=(batch, nq),
        in_specs=[
            pl.BlockSpec((SB_BLOCK, SB_WIDTH), lambda b, i: (b * nq + i, 0)),
            pl.BlockSpec((seq, SB_WIDTH), lambda b, i: (b, 1)),
            pl.BlockSpec((seq, SB_WIDTH), lambda b, i: (b, 2)),
            _layer_block(gain, layer, 2),
        ],
        out_specs=pl.BlockSpec((SB_BLOCK, SB_WIDTH), lambda b, i: (b * nq + i, 0)),
        out_shape=jax.ShapeDtypeStruct((n, SB_WIDTH), BF16),
        compiler_params=_cparams(("parallel", "arbitrary")),
        name="sb_attention",
    )(sb, sb, sb, gain)


RW_BLOCK = 256
RW_GROUP = 4


def _rw_body(x_ref, mu_ref, w0_ref, a0_ref, kk_ref, ka_ref, rk_ref, gng_ref, gnb_ref,
             uphi_ref, uplo_ref, o_ref,
             state_s, prev_s, r_s, k_s, v_s, lw_s, a_s, b_s, y_s, lc_s,
             rp_s, yp_s, p_s, q_s, ge_s, bonus_s, gate_s, filler=None):
    tb = RW_BLOCK
    hb = tb // 2
    c = CHUNK
    pairs = RW_WIDTH // LANES

    lane = lax.broadcasted_iota(jnp.int32, (1, LANES), 1)
    low_half = lane < HEAD_DIM
    tr = lax.broadcasted_iota(jnp.int32, (hb, hb), 0)
    tc = lax.broadcasted_iota(jnp.int32, (hb, hb), 1)
    tri = jnp.logical_and(tr // c == tc // c, tc <= tr).astype(BF16)
    first = lax.broadcasted_iota(jnp.int32, (hb, 1), 0) == 0

    def prologue(part):
        rows = slice(part * hb, (part + 1) * hb)
        x = x_ref[rows, :]
        before = prev_s[...] if part == 0 else x_ref[part * hb - 1:part * hb, :]
        prev = jnp.where(first, before, pltpu.roll(x, 1, 0))
        xs = x + (prev - x) * mu_ref[...]
        yield
        r = xs[:, 0:RW_WIDTH]
        k = xs[:, RW_WIDTH:2 * RW_WIDTH]
        v = xs[:, 2 * RW_WIDTH:3 * RW_WIDTH]
        code = xs[:, 3 * RW_WIDTH:]
        act = jnp.where(lane < 32, jnp.tanh(code), jnp.where(lane < 64, code, jax.nn.sigmoid(code)))
        up = jnp.concatenate(
            [_dot_x3(act, uphi_ref[:, 0:RW_WIDTH], uplo_ref[:, 0:RW_WIDTH]),
             _dot(act.astype(BF16), uphi_ref[:, RW_WIDTH:])], axis=1)
        r_s[rows, :] = r
        v_s[rows, :] = v
        gate_s[rows, :] = up[:, 2 * RW_WIDTH:]
        yield
        wpre = w0_ref[...] + up[:, 0:RW_WIDTH]
        log_w = -(jnp.maximum(-wpre, 0.0) + jnp.log(1.0 + jnp.exp(-jnp.abs(wpre)))) - 0.5
        lw = -jnp.exp(log_w)
        lw_s[rows, :] = lw
        yield
        lc_s[rows, :] = _dot_exact_lhs(tri, lw)
        yield
        a = jax.nn.sigmoid(a0_ref[...] + up[:, RW_WIDTH:2 * RW_WIDTH])
        kmod = k * (1.0 + (a - 1.0) * ka_ref[...])
        k_s[rows, :] = kmod
        yield
        kk = k * kk_ref[...]
        bonus_in = r * kmod * rk_ref[...]
        for p in range(pairs):
            sl = slice(p * LANES, (p + 1) * LANES)
            ss = _pair_head_sum(kk[:, sl] * kk[:, sl], low_half)
            kkn = kk[:, sl] * lax.rsqrt(jnp.maximum(ss, 1e-12))
            a_s[rows, sl] = -kkn
            b_s[rows, sl] = kkn * a[:, sl]
            bonus_s[rows, sl] = _pair_head_sum(bonus_in[:, sl], low_half) * v[:, sl]
            yield

    rr = lax.broadcasted_iota(jnp.int32, (2 * c, 2 * c), 0)
    cc = lax.broadcasted_iota(jnp.int32, (2 * c, 2 * c), 1)
    same_head = (rr // c) == (cc // c)
    strict = jnp.logical_and(same_head, cc < rr)
    incl = jnp.logical_and(same_head, cc <= rr)
    m_lo = low_half
    m_hi = jnp.logical_not(low_half)

    def stack(xp):
        return jnp.concatenate([jnp.where(m_lo, xp, 0.0), jnp.where(m_hi, xp, 0.0)], axis=0)

    def prepare(chunks, tick):
        chains = [(ci, p) for ci in chunks for p in range(pairs)]
        pre = []
        for ci, p in chains:
            rows = slice(ci * c, (ci + 1) * c)
            sl = slice(p * LANES, (p + 1) * LANES)
            lwc = lw_s[rows, sl]
            lc = lc_s[rows, sl]
            e_incl = jnp.exp(lc)
            e_excl = jnp.exp(lc - lwc)
            e_inv = jnp.exp(-lc)
            g_end = e_incl[c - 1:c, :]
            ag = stack(a_s[rows, sl] * e_excl).astype(BF16)
            rg = stack(r_s[rows, sl] * e_incl).astype(BF16)
            bd = b_s[rows, sl] * e_inv
            kd = k_s[rows, sl] * e_inv
            vst = stack(v_s[rows, sl]).astype(BF16)
            lhs = jnp.concatenate([ag, rg], axis=0)
            rhs = jnp.concatenate([bd, bd, kd, kd], axis=0).astype(BF16)
            bde = stack(bd * g_end).astype(BF16)
            kde = stack(kd * g_end).astype(BF16)
            idx = ci * pairs + p
            ge_s[idx] = g_end
            pre.append((idx, ag, rg, vst, lhs, rhs, kde, bde))
        tick()
        gs = [_dot_nt(t[4], t[5]) for t in pre]
        tick()
        npows, xs_, avs, arbs = [], [], [], []
        for t, g in zip(pre, gs):
            a_ab = jnp.where(strict, g[0:2 * c, 0:2 * c], 0.0)
            a_ak = jnp.where(strict, g[0:2 * c, 2 * c:], 0.0)
            a_rb = jnp.where(incl, g[2 * c:, 0:2 * c], 0.0)
            a_rk = jnp.where(incl, g[2 * c:, 2 * c:], 0.0)
            arbs.append(a_rb.astype(BF16))
            npows.append(a_ab)
            avs.append(jnp.concatenate([a_ak, a_rk], axis=0).astype(BF16))
        avs = [_dot(av, t[3]) for av, t in zip(avs, pre)]
        svs = [_dot_tn(t[3], t[6]) for t in pre]
        tick()
        for t, av in zip(pre, avs):
            xs_.append(jnp.concatenate([t[1].astype(F32), av[0:2 * c, :]], axis=1))
        for it in range(6):
            nbs = [n.astype(BF16) for n in npows]
            xbs = [xc.astype(BF16) for xc in xs_]
            if it < 5:
                prods = [_dot(nb, jnp.concatenate([nb, xb], axis=1)) for nb, xb in zip(nbs, xbs)]
                npows = [pr[:, 0:LANES] for pr in prods]
                xs_ = [xc + pr[:, LANES:] for xc, pr in zip(xs_, prods)]
            else:
                xs_ = [xc + _dot(nb, xb) for xc, nb, xb in zip(xs_, nbs, xbs)]
            tick()
        xbs = [xc.astype(BF16) for xc in xs_]
        ras = [_dot(arb, xb) for arb, xb in zip(arbs, xbs)]
        pqs = [_dot_tn(xb, t[7]) for xb, t in zip(xbs, pre)]
        for t, av, sv, ra, pq in zip(pre, avs, svs, ras, pqs):
            idx = t[0]
            rp_s[idx] = (t[2].astype(F32) + ra[:, 0:LANES]).astype(BF16)
            yp_s[idx] = ra[:, LANES:] + av[2 * c:, :]
            p_s[idx] = pq[0:LANES, :].astype(BF16)
            q_s[idx] = pq[LANES:, :] + sv
        tick()

    half_chunks = hb // c
    states = [state_s[p] for p in range(pairs)]

    def finish(part):
        for ci in range(part * half_chunks, (part + 1) * half_chunks):
            for p in range(pairs):
                idx = ci * pairs + p
                sb = states[p].astype(BF16)
                y = _dot_nt(rp_s[idx], sb) + yp_s[idx]
                y_s[ci * c:(ci + 1) * c, p * LANES:(p + 1) * LANES] = y[0:c, :] + y[c:2 * c, :]
                states[p] = states[p] * ge_s[idx] + _dot(sb, p_s[idx]) + q_s[idx]
            yield
        rows = slice(part * hb, (part + 1) * hb)
        for p in range(pairs):
            sl = slice(p * LANES, (p + 1) * LANES)
            y = y_s[rows, sl]
            mean = _pair_head_sum(y, low_half) * (1.0 / HEAD_DIM)
            d = y - mean
            var = _pair_head_sum(d * d, low_half) * (1.0 / HEAD_DIM)
            yn = d * lax.rsqrt(var + RW_GN_EPS) * gng_ref[:, sl] + gnb_ref[:, sl]
            o_ref[rows, sl] = ((yn + bonus_s[rows, sl]) * gate_s[rows, sl]).astype(o_ref.dtype)
            yield

    def advance(*gens):
        def tick():
            for gen in gens:
                if gen is not None:
                    next(gen, None)
        return tick

    def drain(gen):
        for _ in gen:
            pass

    drain(prologue(0))
    second = prologue(1)
    prepare(range(half_chunks), advance(second, filler))
    drain(second)
    prev_s[...] = x_ref[tb - 1:tb, :]
    first_half = finish(0)
    prepare(range(half_chunks, 2 * half_chunks), advance(first_half, filler))
    drain(first_half)
    drain(finish(1))
    for p in range(pairs):
        state_s[p] = states[p]


RW_N_IN, RW_N_SCRATCH = 11, 17
GLA_N_IN, GLA_N_SCRATCH = 5, 5


def _recurrent_kernel(*refs):
    n_in = RW_N_IN + GLA_N_IN
    rw_in, gl_in = refs[:RW_N_IN], refs[RW_N_IN:n_in]
    o_rw, o_gl = refs[n_in], refs[n_in + 1]
    rw_scr = refs[n_in + 2:n_in + 2 + RW_N_SCRATCH]
    gl_scr = refs[n_in + 2 + RW_N_SCRATCH:]

    @pl.when(pl.program_id(1) == 0)
    def _():
        for state in (rw_scr[0], rw_scr[1], gl_scr[0]):
            state[...] = jnp.zeros_like(state)

    gla = _gla_body(*gl_in, o_gl, *gl_scr)
    _rw_body(*rw_in, o_rw, *rw_scr, filler=gla)
    for _ in gla:
        pass


def _recurrent_mixers(rw, gl, layer, rprm, gprm, batch, seq):
    n = rw.shape[0]
    tb = RW_BLOCK
    nb = seq // tb
    rows = lambda width: pl.BlockSpec((tb, width), lambda b, i: (b * nb + i, 0))
    rw_names = ("mu", "w0", "a0", "k_k", "k_a", "r_k", "gn_g", "gn_b", "up_hi", "up_lo")
    gl_names = ("gate_b", "norm_g", "up_hi", "up_lo")
    rw_params = [rprm[k] for k in rw_names]
    gl_params = [gprm[k] for k in gl_names]
    assert 1 + len(rw_params) == RW_N_IN and 1 + len(gl_params) == GLA_N_IN
    blk = lambda: pltpu.VMEM((tb, RW_WIDTH), F32)
    n_chain = (tb // CHUNK) * (RW_WIDTH // LANES)
    tile = lambda r, dt: pltpu.VMEM((n_chain, r, LANES), dt)
    rw_scratch = [pltpu.VMEM((RW_WIDTH // LANES, LANES, LANES), F32),
                  pltpu.VMEM((1, RW_IN_WIDTH), F32),
                  blk(), blk(), blk(), blk(), blk(), blk(), blk(), blk(),
                  tile(LANES, BF16), tile(LANES, F32), tile(LANES, BF16), tile(LANES, F32),
                  tile(1, F32), blk(), blk()]
    gl_scratch = [pltpu.VMEM((GLA_VAL_WIDTH, GLA_KEY_WIDTH), F32),
                  pltpu.VMEM((tb, GLA_KEY_WIDTH), F32),
                  pltpu.VMEM((tb, GLA_KEY_WIDTH), F32),
                  pltpu.VMEM((tb, GLA_KEY_WIDTH), F32),
                  pltpu.VMEM((tb, GLA_VAL_WIDTH), F32)]
    assert len(rw_scratch) == RW_N_SCRATCH and len(gl_scratch) == GLA_N_SCRATCH
    return pl.pallas_call(
        _recurrent_kernel,
        grid=(batch, nb),
        in_specs=([rows(RW_IN_WIDTH)] + [_layer_block(a, layer, 2) for a in rw_params]
                  + [rows(GLA_PAD_WIDTH)] + [_layer_block(a, layer, 2) for a in gl_params]),
        out_specs=[rows(RW_WIDTH), rows(GLA_VAL_WIDTH)],
        out_shape=[jax.ShapeDtypeStruct((n, RW_WIDTH), BF16),
                   jax.ShapeDtypeStruct((n, GLA_VAL_WIDTH), BF16)],
        scratch_shapes=rw_scratch + gl_scratch,
        compiler_params=_cparams(("parallel", "arbitrary")),
        name="rwkv7_gla",
    )(rw, *rw_params, gl, *gl_params)


GLA_BLOCK = 256
GLA_SUB = 16
GLA_EXP_CAP = 80.0


def _gla_body(x_ref, gb_ref, ng_ref, uphi_ref, uplo_ref, o_ref, state_s, q_s, k_s, la_s, o_s):
    tb = GLA_BLOCK
    c = CHUNK
    heads = GLA_VAL_WIDTH // HEAD_DIM

    kw, vw = GLA_KEY_WIDTH, GLA_VAL_WIDTH
    code = x_ref[:, 2 * kw + 2 * vw:]
    pre = _dot_x3(code, uphi_ref[...], uplo_ref[...]) + gb_ref[...]
    la_s[...] = -(jnp.maximum(-pre, 0.0) + jnp.log(1.0 + jnp.exp(-jnp.abs(pre)))) * (1.0 / GLA_GATE_NORMALIZER)
    q_s[...] = x_ref[:, 0:kw] * (GLA_DK ** -0.5)
    k_s[...] = x_ref[:, kw:2 * kw]
    yield

    tr = lax.broadcasted_iota(jnp.int32, (c, c), 0)
    tc = lax.broadcasted_iota(jnp.int32, (c, c), 1)
    tri = (tc <= tr).astype(BF16)
    sub = GLA_SUB
    n_sub = c // sub
    srow = lax.broadcasted_iota(jnp.int32, (heads * c, c), 0)
    scol = lax.broadcasted_iota(jnp.int32, (heads * c, c), 1)
    causal = scol <= (srow // (heads * sub)) * sub + srow % sub
    klane = lax.broadcasted_iota(jnp.int32, (1, kw), 1) // GLA_DK
    vlane = lax.broadcasted_iota(jnp.int32, (1, vw), 1) // HEAD_DIM
    st_r = lax.broadcasted_iota(jnp.int32, (vw, kw), 0) // HEAD_DIM
    st_c = lax.broadcasted_iota(jnp.int32, (vw, kw), 1) // GLA_DK
    st_mask = st_r == st_c

    n_chunk = tb // c
    pre = []
    for ci in range(n_chunk):
        rows = slice(ci * c, (ci + 1) * c)
        bc = _dot_exact_lhs(tri, la_s[rows, :])
        e_pos = jnp.exp(bc)
        qc = q_s[rows, :]
        kc = k_s[rows, :]
        k2 = (kc * jnp.exp(bc[c - 1:c, :] - bc)).astype(BF16)
        vb = x_ref[rows, 2 * kw:2 * kw + vw].astype(BF16)
        qsts, kes = [], []
        for i in range(n_sub):
            srows = slice(i * sub, (i + 1) * sub)
            ref = bc[i * sub - 1:i * sub, :] if i > 0 else jnp.zeros((1, kw), F32)
            qi = qc[srows, :] * jnp.exp(bc[srows, :] - ref)
            qsts.append(jnp.concatenate([jnp.where(klane == h, qi, 0.0) for h in range(heads)],
                                        axis=0).astype(BF16))
            kes.append((kc * jnp.exp(jnp.minimum(ref - bc, GLA_EXP_CAP))).astype(BF16))
        pre.append(((qc * e_pos).astype(BF16), k2, vb, qsts, kes, e_pos[c - 1:c, :]))
        yield
    scores = []
    for t in pre:
        scores.append(jnp.where(
            causal, jnp.concatenate([_dot_nt(qs, ke) for qs, ke in zip(t[3], t[4])], axis=0),
            0.0).astype(BF16))
        yield
    o_sts = [_dot(sc, t[2]) for sc, t in zip(scores, pre)]
    incs = [jnp.where(st_mask, _dot_tn(t[2], t[1]), 0.0) for t in pre]
    yield
    state = state_s[...]
    for ci in range(n_chunk):
        intra = []
        for i in range(n_sub):
            acc = None
            for h in range(heads):
                r0 = (i * heads + h) * sub
                part = jnp.where(vlane == h, o_sts[ci][r0:r0 + sub, :], 0.0)
                acc = part if acc is None else acc + part
            intra.append(acc)
        o_s[ci * c:(ci + 1) * c, :] = _dot_nt(pre[ci][0], state.astype(BF16)) + jnp.concatenate(intra, axis=0)
        state = state * pre[ci][5] + incs[ci]
        yield
    state_s[...] = state

    lane = lax.broadcasted_iota(jnp.int32, (1, LANES), 1)
    low_half = lane < HEAD_DIM
    outs = []
    for p in range(vw // LANES):
        o = o_s[:, p * LANES:(p + 1) * LANES]
        ms = _pair_head_sum(o * o, low_half) * (1.0 / HEAD_DIM)
        outs.append(o * lax.rsqrt(ms + EPS))
    g = x_ref[:, 2 * kw + vw:2 * kw + 2 * vw]
    on = jnp.concatenate(outs, axis=1) * ng_ref[...]
    o_ref[...] = (on * (g * jax.nn.sigmoid(g))).astype(o_ref.dtype)


MLP_ROWS = 1024
MLP_FF_TILE = 1024


def _mlp_kernel(x_ref, sb_ref, rw_ref, gl_ref, wsb_ref, wrw_ref, wgl_ref, gmix_ref,
                gpre_ref, w1_ref, w2_ref, gpost_ref, o_ref):
    mixed = _dot(sb_ref[...], wsb_ref[...]) + _dot(rw_ref[...], wrw_ref[...]) + _dot(gl_ref[...], wgl_ref[...])
    x1 = x_ref[...] + _rms(mixed, gmix_ref[...])
    h = _rms(x1, gpre_ref[...]).astype(BF16)
    ff = None
    for j in range(D_FF // MLP_FF_TILE):
        cols = slice(j * MLP_FF_TILE, (j + 1) * MLP_FF_TILE)
        a = jnp.maximum(_dot(h, w1_ref[:, cols]), 0.0)
        part = _dot((a * a).astype(BF16), w2_ref[cols, :])
        ff = part if ff is None else ff + part
    o_ref[...] = x1 + _rms(ff, gpost_ref[...])


def _mlp(x2, o_sb, o_rw, o_gl, layer, w_out, g_mix, g_pre, w1, w2, g_post):
    n = x2.shape[0]
    tm = MLP_ROWS
    full = lambda w: _layer_block(w, layer, 1, single_buffer=True)
    row = lambda width: pl.BlockSpec((tm, width), lambda i: (i, 0))

    def out_rows(width, first):
        assert first % width == 0
        return pl.BlockSpec((None, width, D_MODEL), lambda i: (layer, first // width, 0),
                            pipeline_mode=pl.Buffered(1))

    return pl.pallas_call(
        _mlp_kernel,
        grid=(n // tm,),
        in_specs=[row(D_MODEL), row(SB_WIDTH), row(RW_WIDTH), row(GLA_VAL_WIDTH),
                  out_rows(SB_WIDTH, 0), out_rows(RW_WIDTH, SB_WIDTH),
                  out_rows(GLA_VAL_WIDTH, SB_WIDTH + RW_WIDTH), full(g_mix),
                  full(g_pre), full(w1), full(w2), full(g_post)],
        out_specs=row(D_MODEL),
        out_shape=jax.ShapeDtypeStruct((n, D_MODEL), F32),
        compiler_params=_cparams(("parallel",)),
        name="out_proj_mlp",
    )(x2, o_sb, o_rw, o_gl, w_out, w_out, w_out, g_mix, g_pre, w1, w2, g_post)


def _hi_lo(w):
    hi = w.astype(BF16)
    return hi, (w - hi.astype(F32)).astype(BF16)


def _prep_weights(w_in, rw_w_up, rw_a_up, rw_g_up, gla_gate_up, w_out, w_ff1, w_ff2):
    depth = w_in.shape[0]
    sb_end = 3 * SB_WIDTH
    rw_end = sb_end + RW_IN_WIDTH
    w_sb = w_in[:, :, :sb_end].astype(BF16)
    w_rw = w_in[:, :, sb_end:rw_end].astype(BF16)
    g0 = rw_end
    kw, vw, gr = GLA_KEY_WIDTH, GLA_VAL_WIDTH, GLA_GATE_RANK
    w_gl = jnp.concatenate([
        w_in[:, :, g0:g0 + 2 * kw + vw],
        w_in[:, :, g0 + 2 * kw + vw + gr:g0 + GLA_IN_WIDTH],
        w_in[:, :, g0 + 2 * kw + vw:g0 + 2 * kw + vw + gr],
        jnp.zeros((depth, D_MODEL, GLA_PAD_WIDTH - GLA_IN_WIDTH), F32)], axis=2).astype(BF16)
    up = jnp.zeros((depth, RW_CODE, 3 * RW_WIDTH), F32)
    up = up.at[:, 0:32, 0:RW_WIDTH].set(rw_w_up)
    up = up.at[:, 32:64, RW_WIDTH:2 * RW_WIDTH].set(rw_a_up)
    up = up.at[:, 64:128, 2 * RW_WIDTH:].set(rw_g_up)
    rw_up_hi, rw_up_lo = _hi_lo(up)
    gup = jnp.zeros((depth, LANES, GLA_KEY_WIDTH), F32).at[:, 0:gr, :].set(gla_gate_up)
    gl_up_hi, gl_up_lo = _hi_lo(gup)
    return dict(w_sb=w_sb, w_rw=w_rw, w_gl=w_gl, rw_up_hi=rw_up_hi, rw_up_lo=rw_up_lo,
                gl_up_hi=gl_up_hi, gl_up_lo=gl_up_lo, wo=w_out.astype(BF16),
                w1=w_ff1.astype(BF16), w2=w_ff2.astype(BF16))


def kernel(x, pre_mix_g, w_in, sb_norm_g, rw_mu, rw_w0, rw_w_up, rw_a0, rw_a_up, rw_g_up, rw_k_k, rw_k_a, rw_r_k, rw_gn_g, rw_gn_b, gla_gate_up, gla_gate_b, gla_norm_g, w_out, post_mix_g, pre_ffn_g, w_ff1, w_ff2, post_ffn_g):
    batch, seq, d = x.shape
    depth = w_in.shape[0]
    x2 = x.reshape(batch * seq, d)
    vec = lambda t: t[:, None, :]
    wp = _prep_weights(w_in, rw_w_up, rw_a_up, rw_g_up, gla_gate_up, w_out, w_ff1, w_ff2)
    rw_prm = dict(mu=vec(rw_mu), w0=vec(rw_w0), a0=vec(rw_a0), k_k=vec(rw_k_k), k_a=vec(rw_k_a),
                  r_k=vec(rw_r_k), gn_g=vec(rw_gn_g), gn_b=vec(rw_gn_b),
                  up_hi=wp["rw_up_hi"], up_lo=wp["rw_up_lo"])
    gl_prm = dict(gate_b=vec(gla_gate_b), norm_g=vec(gla_norm_g),
                  up_hi=wp["gl_up_hi"], up_lo=wp["gl_up_lo"])
    for l in range(depth):
        sb, rw, gl = _inproj(x2, l, vec(pre_mix_g), wp["w_sb"], wp["w_rw"], wp["w_gl"])
        o_sb = _sb_attention(sb, l, vec(sb_norm_g), batch, seq)
        o_rw, o_gl = _recurrent_mixers(rw, gl, l, rw_prm, gl_prm, batch, seq)
        x2 = _mlp(x2, o_sb, o_rw, o_gl, l, wp["wo"], vec(post_mix_g), vec(pre_ffn_g),
                  wp["w1"], wp["w2"], vec(post_ffn_g))
    return x2.reshape(batch, seq, d)
```

```python
import functools

import jax
import jax.numpy as jnp
from jax import lax
from jax.experimental import pallas as pl
from jax.experimental.pallas import tpu as pltpu

F32 = jnp.float32
BF16 = jnp.bfloat16

D_MODEL = 1024
HEAD_DIM = 64
LANES = 128
SB_WIDTH = 384
RW_WIDTH = 384
RW_IN_WIDTH = 1280
RW_CODE = 128
GLA_KEY_WIDTH = 128
GLA_VAL_WIDTH = 256
GLA_DK = 32
GLA_GATE_RANK = 16
GLA_IN_WIDTH = 784
GLA_PAD_WIDTH = 896
GLA_GATE_NORMALIZER = 16.0
D_FF = 4096
EPS = 1e-6
RW_GN_EPS = 64e-5
CHUNK = 64
VMEM_LIMIT = 58 * 1024 * 1024

SB_BLOCK = 256
LOG2E = 1.4426950408889634
SB_DEAD = 110.0 * LOG2E
SB_CLAMP = 120.0
SB_LAG = 4


def _cparams(sem):
    return pltpu.CompilerParams(dimension_semantics=sem, vmem_limit_bytes=VMEM_LIMIT)


def _dot(a, b):
    return jnp.dot(a, b, preferred_element_type=F32)


def _dot_nt(a, b):
    return lax.dot_general(a, b, (((1,), (1,)), ((), ())), preferred_element_type=F32)


def _dot_tn(a, b):
    return lax.dot_general(a, b, (((0,), (0,)), ((), ())), preferred_element_type=F32)


def _split2(x):
    hi = x.astype(BF16)
    lo = (x - hi.astype(F32)).astype(BF16)
    return hi, lo


def _split3(x):
    hi = x.astype(BF16)
    r1 = x - hi.astype(F32)
    mid = r1.astype(BF16)
    lo = (r1 - mid.astype(F32)).astype(BF16)
    return hi, mid, lo


def _dot_x3(a, w_hi, w_lo):
    a_hi, a_lo = _split2(a)
    return _dot(a_hi, w_hi) + (_dot(a_lo, w_hi) + _dot(a_hi, w_lo))


def _dot_exact_lhs(m_bf16, x):
    hi, mid, lo = _split3(x)
    return _dot(m_bf16, hi) + (_dot(m_bf16, mid) + _dot(m_bf16, lo))


def _rms(x, gain):
    return x * lax.rsqrt(jnp.mean(x * x, axis=-1, keepdims=True) + EPS) * gain


def _pair_head_sum(x, low_half):
    s_all = jnp.sum(x, axis=-1, keepdims=True)
    s_low = jnp.sum(jnp.where(low_half, x, 0.0), axis=-1, keepdims=True)
    return jnp.where(low_half, s_low, s_all - s_low)


def _inproj_kernel(x_ref, g_ref, wsb_ref, wrw_ref, wgl_ref, sb_ref, rw_ref, gl_ref):
    h = _rms(x_ref[...], g_ref[...]).astype(BF16)
    col = lax.broadcasted_iota(jnp.int32, (1, 3 * SB_WIDTH), 1)
    qscale = jnp.where(col < SB_WIDTH, LOG2E * HEAD_DIM ** -0.5, 1.0)
    sb_ref[...] = (_dot(h, wsb_ref[...]) * qscale).astype(BF16)
    rw_ref[...] = _dot(h, wrw_ref[...])
    gl_ref[...] = _dot(h, wgl_ref[...])


INPROJ_ROWS = 1024


def _layer_block(arr, layer, grid_rank, single_buffer=False):
    index = lambda *_: (layer, 0, 0)
    assert arr.ndim == 3 and grid_rank in (1, 2)
    if single_buffer:
        return pl.BlockSpec((None,) + arr.shape[1:], index, pipeline_mode=pl.Buffered(1))
    return pl.BlockSpec((None,) + arr.shape[1:], index)


def _inproj(x2, layer, gain, w_sb, w_rw, w_gl, tm=INPROJ_ROWS):
    n = x2.shape[0]
    full = lambda w: _layer_block(w, layer, 1, single_buffer=True)
    row = lambda width: pl.BlockSpec((tm, width), lambda i: (i, 0))
    return pl.pallas_call(
        _inproj_kernel,
        grid=(n // tm,),
        in_specs=[row(D_MODEL), full(gain), full(w_sb), full(w_rw), full(w_gl)],
        out_specs=[row(3 * SB_WIDTH), row(RW_IN_WIDTH), row(GLA_PAD_WIDTH)],
        out_shape=[jax.ShapeDtypeStruct((n, 3 * SB_WIDTH), BF16),
                   jax.ShapeDtypeStruct((n, RW_IN_WIDTH), F32),
                   jax.ShapeDtypeStruct((n, GLA_PAD_WIDTH), F32)],
        compiler_params=_cparams(("parallel",)),
        name="in_proj",
    )(x2, gain, w_sb, w_rw, w_gl)


def _sb_kernel(q_ref, k_ref, v_ref, g_ref, o_ref):
    qb = SB_BLOCK
    qi = pl.program_id(1)
    lane = lax.broadcasted_iota(jnp.int32, (1, LANES), 1)
    low_half = lane < HEAD_DIM
    row = lax.broadcasted_iota(jnp.int32, (qb, qb), 0)
    col = lax.broadcasted_iota(jnp.int32, (qb, qb), 1)
    causal = col < row
    suffix = (row > col).astype(BF16)
    half = qb // 2
    pairs = SB_WIDTH // LANES
    zero = jnp.zeros((), BF16)
    vmasks = (low_half, jnp.logical_not(low_half))
    chains = [(h, r) for r in range(2) for h in range(2)]
    lanes_of = [slice(pr * LANES, (pr + 1) * LANES) for pr in range(pairs)]
    qcs = []
    for sl in lanes_of:
        q = q_ref[:, sl]
        qhs = [jnp.where(m, q, zero) for m in vmasks]
        qcs.append([qhs[h][r * half:(r + 1) * half, :] for h, r in chains])

    def front(qc, kb, n, mask):
        return front_tail(_dot_nt(qc, kb[0:n, :]), n, mask)

    def front_tail(z, n, mask):
        p = jnp.maximum(z, jnp.log2(1.0 + jnp.exp2(jnp.minimum(z, SB_CLAMP))))
        log_beta = z - p
        if mask is not None:
            p = jnp.where(mask, p, 0.0)
        later = _dot(p.astype(BF16), suffix[0:n, 0:n])
        return log_beta, p[:, 0:1], later

    def back(log_beta, p_first, later, spent, mask):
        w = jnp.exp2(log_beta - (later + spent))
        if mask is not None:
            w = jnp.where(mask, w, 0.0)
        return w.astype(BF16), spent + (later[:, 0:1] + p_first)

    def values(vb, n):
        return [jnp.where(m, vb[0:n, :], zero) for m in vmasks]

    d_start = pl.multiple_of(qi * qb, qb)
    p_start = pl.multiple_of(jnp.maximum(qi - 1, 0) * qb, qb)
    d_keys = [half if r == 0 else qb for _, r in chains]
    d_masks = [causal[0:half, 0:half] if r == 0 else causal[half:, :] for _, r in chains]
    seq_chains = []
    for pr in range(pairs):
        for prev in (False, True):
            for ci, (h, r) in enumerate(chains):
                seq_chains.append((pr, prev, ci, h, r))
    n_seq = len(seq_chains)
    p_vmasks = [jnp.logical_and(m, qi > 0) for m in vmasks]

    logits, fronts, spent_of = [None] * n_seq, [None] * n_seq, {}
    acc_of = {}
    def span(prev, r):
        if prev:
            return (0, qb) if r == 0 else (half, half)
        return (0, half) if r == 0 else (0, qb)

    for t in range(n_seq + 2 * SB_LAG):
        c2, c1 = t - 2 * SB_LAG, t - SB_LAG
        if 0 <= c2 < n_seq:
            pr, prev, ci, h, r = seq_chains[c2]
            k0, n = span(prev, r)
            mask = None if prev else d_masks[ci]
            spent = spent_of[(pr, ci)] if prev else jnp.zeros((half, 1), F32)
            log_beta, p_first, later = fronts[c2]
            fronts[c2] = None
            w, spent_of[(pr, ci)] = back(log_beta, p_first, later, spent, mask)
            start = p_start if prev else d_start
            vb = v_ref[pl.ds(start, qb), lanes_of[pr]]
            vh = jnp.where(p_vmasks[h] if prev else vmasks[h], vb[k0:k0 + n, :], zero)
            part = _dot(w, vh)
            acc_of[(pr, r)] = part if (pr, r) not in acc_of else acc_of[(pr, r)] + part
        if 0 <= c1 < n_seq:
            pr, prev, ci, h, r = seq_chains[c1]
            mask = None if prev else d_masks[ci]
            fronts[c1] = front_tail(logits[c1], span(prev, r)[1], mask)
            logits[c1] = None
        if t < n_seq:
            pr, prev, ci, h, r = seq_chains[t]
            k0, n = span(prev, r)
            start = p_start if prev else d_start
            kb = k_ref[pl.ds(start, qb), lanes_of[pr]]
            logits[t] = _dot_nt(qcs[pr][ci], kb[k0:k0 + n, :])
    accs_all = [[acc_of[(pr, r)] for r in range(2)] for pr in range(pairs)]
    bp_spent = [[spent_of[(pr, ci)] for ci in range(len(chains))] for pr in range(pairs)]

    late = [ci for ci, (_, r) in enumerate(chains) if r == 1]

    def least(spents):
        m = spents[0]
        for sp in spents[1:]:
            m = jnp.minimum(m, sp)
        return jnp.min(m)

    far_needed = [jnp.logical_and(qi > 0, least([bp_spent[pr][ci] for ci in late]) < SB_DEAD)
                  for pr in range(pairs)]
    more_needed = [jnp.logical_and(qi > 1, least(bp_spent[pr]) < SB_DEAD) for pr in range(pairs)]

    finals = []
    for pr, sl in enumerate(lanes_of):
        def far_half(args, pr=pr, sl=sl):
            acc, spents = args
            kb, vb = k_ref[pl.ds(p_start, half), sl], v_ref[pl.ds(p_start, half), sl]
            fs = [front(qcs[pr][ci], kb, half, None) for ci in late]
            bs = [back(*f, sp, None) for f, sp in zip(fs, spents)]
            vh = jnp.concatenate(values(vb, half), axis=0)
            return acc + _dot(jnp.concatenate([b[0] for b in bs], axis=1), vh), [b[1] for b in bs]

        def cond(s):
            j, spents, _ = s
            return jnp.logical_and(j >= 0, least(spents) < SB_DEAD)

        def body(s, pr=pr, sl=sl):
            j, spents, accs = s
            start = pl.multiple_of(j * qb, qb)
            kb, vb = k_ref[pl.ds(start, qb), sl], v_ref[pl.ds(start, qb), sl]
            fs = [front(qc, kb, qb, None) for qc in qcs[pr]]
            bs = [back(*f, sp, None) for f, sp in zip(fs, spents)]
            vh = jnp.concatenate(values(vb, qb), axis=0)
            accs = [acc + _dot(jnp.concatenate([bs[2 * r][0], bs[2 * r + 1][0]], axis=1), vh)
                    for r, acc in enumerate(accs)]
            return j - 1, [b[1] for b in bs], accs

        def sweep_on(args, pr=pr, far_half=far_half, cond=cond, body=body):
            accs, spents = args
            late_spent = [spents[ci] for ci in late]
            acc_late, late_spent = lax.cond(far_needed[pr], far_half, lambda a: a, (accs[1], late_spent))
            spents = list(spents)
            for ci, sp in zip(late, late_spent):
                spents[ci] = sp
            _, _, accs = lax.while_loop(cond, body, (qi - 2, spents, [accs[0], acc_late]))
            return accs, spents

        accs, _ = lax.cond(jnp.logical_or(far_needed[pr], more_needed[pr]), sweep_on, lambda a: a,
                           (accs_all[pr], bp_spent[pr]))
        finals.append(jnp.concatenate(accs, axis=0))

    outs = []
    for acc in finals:
        ms = _pair_head_sum(acc * acc, low_half) * (1.0 / HEAD_DIM)
        outs.append(acc * lax.rsqrt(ms + EPS))
    o_ref[...] = (jnp.concatenate(outs, axis=1) * g_ref[...]).astype(o_ref.dtype)


def _sb_attention(sb, layer, gain, batch, seq):
    nq = seq // SB_BLOCK
    n = sb.shape[0]
    pairs = SB_WIDTH // LANES
    return pl.pallas_call(
        _sb_kernel,
        grid=(batch, nq),
        in_specs=[
            pl.BlockSpec((SB_BLOCK, SB_WIDTH), lambda b, i: (b * nq + i, 0)),
            pl.BlockSpec((seq, SB_WIDTH), lambda b, i: (b, 1)),
            pl.BlockSpec((seq, SB_WIDTH), lambda b, i: (b, 2)),
            _layer_block(gain, layer, 2),
        ],
        out_specs=pl.BlockSpec((SB_BLOCK, SB_WIDTH), lambda b, i: (b * nq + i, 0)),
        out_shape=jax.ShapeDtypeStruct((n, SB_WIDTH), BF16),
        compiler_params=_cparams(("parallel", "arbitrary")),
        name="sb_attention",
    )(sb, sb, sb, gain)


RW_BLOCK = 512
RW_PART = 128


def _rw_body(x_ref, mu_ref, w0_ref, a0_ref, kk_ref, ka_ref, rk_ref, gng_ref, gnb_ref,
             uphi_ref, uplo_ref, o_ref,
             state_s, prev_s, r_s, k_s, v_s, lw_s, a_s, b_s, y_s, lc_s,
             rp_s, yp_s, p_s, q_s, ge_s, bonus_s, gate_s, filler=None):
    tb = RW_BLOCK
    hb = RW_PART
    c = CHUNK
    pairs = RW_WIDTH // LANES

    lane = lax.broadcasted_iota(jnp.int32, (1, LANES), 1)
    low_half = lane < HEAD_DIM
    tr = lax.broadcasted_iota(jnp.int32, (hb, hb), 0)
    tc = lax.broadcasted_iota(jnp.int32, (hb, hb), 1)
    tri = jnp.logical_and(tr // c == tc // c, tc <= tr).astype(BF16)
    first = lax.broadcasted_iota(jnp.int32, (hb, 1), 0) == 0

    def prologue(part):
        rows = slice(part * hb, (part + 1) * hb)
        x = x_ref[rows, :]
        before = prev_s[...] if part == 0 else x_ref[part * hb - 1:part * hb, :]
        prev = jnp.where(first, before, pltpu.roll(x, 1, 0))
        xs = x + (prev - x) * mu_ref[...]
        yield
        r = xs[:, 0:RW_WIDTH]
        k = xs[:, RW_WIDTH:2 * RW_WIDTH]
        v = xs[:, 2 * RW_WIDTH:3 * RW_WIDTH]
        code = xs[:, 3 * RW_WIDTH:]
        act = jnp.where(lane < 32, jnp.tanh(code), jnp.where(lane < 64, code, jax.nn.sigmoid(code)))
        up = jnp.concatenate(
            [_dot_x3(act, uphi_ref[:, 0:RW_WIDTH], uplo_ref[:, 0:RW_WIDTH]),
             _dot(act.astype(BF16), uphi_ref[:, RW_WIDTH:])], axis=1)
        r_s[rows, :] = r
        v_s[rows, :] = v
        gate_s[rows, :] = up[:, 2 * RW_WIDTH:]
        yield
        wpre = w0_ref[...] + up[:, 0:RW_WIDTH]
        log_w = -(jnp.maximum(-wpre, 0.0) + jnp.log(1.0 + jnp.exp(-jnp.abs(wpre)))) - 0.5
        lw = -jnp.exp(log_w)
        lw_s[rows, :] = lw
        yield
        lc_s[rows, :] = _dot_exact_lhs(tri, lw)
        yield
        a = jax.nn.sigmoid(a0_ref[...] + up[:, RW_WIDTH:2 * RW_WIDTH])
        kmod = k * (1.0 + (a - 1.0) * ka_ref[...])
        k_s[rows, :] = kmod
        yield
        kk = k * kk_ref[...]
        bonus_in = r * kmod * rk_ref[...]
        for p in range(pairs):
            sl = slice(p * LANES, (p + 1) * LANES)
            ss = _pair_head_sum(kk[:, sl] * kk[:, sl], low_half)
            kkn = kk[:, sl] * lax.rsqrt(jnp.maximum(ss, 1e-12))
            a_s[rows, sl] = -kkn
            b_s[rows, sl] = kkn * a[:, sl]
            bonus_s[rows, sl] = _pair_head_sum(bonus_in[:, sl], low_half) * v[:, sl]
            yield

    rr = lax.broadcasted_iota(jnp.int32, (2 * c, 2 * c), 0)
    cc = lax.broadcasted_iota(jnp.int32, (2 * c, 2 * c), 1)
    same_head = (rr // c) == (cc // c)
    strict = jnp.logical_and(same_head, cc < rr)
    incl = jnp.logical_and(same_head, cc <= rr)
    m_lo = low_half
    m_hi = jnp.logical_not(low_half)

    def stack(xp):
        return jnp.concatenate([jnp.where(m_lo, xp, 0.0), jnp.where(m_hi, xp, 0.0)], axis=0)

    def prepare(chunks, tick):
        chains = [(ci, p) for ci in chunks for p in range(pairs)]
        pre = []
        for ci, p in chains:
            rows = slice(ci * c, (ci + 1) * c)
            sl = slice(p * LANES, (p + 1) * LANES)
            lwc = lw_s[rows, sl]
            lc = lc_s[rows, sl]
            e_incl = jnp.exp(lc)
            e_excl = jnp.exp(lc - lwc)
            e_inv = jnp.exp(-lc)
            g_end = e_incl[c - 1:c, :]
            ag = stack(a_s[rows, sl] * e_excl).astype(BF16)
            rg = stack(r_s[rows, sl] * e_incl).astype(BF16)
            bd = b_s[rows, sl] * e_inv
            kd = k_s[rows, sl] * e_inv
            vst = stack(v_s[rows, sl]).astype(BF16)
            lhs = jnp.concatenate([ag, rg], axis=0)
            rhs = jnp.concatenate([bd, bd, kd, kd], axis=0).astype(BF16)
            bde = stack(bd * g_end).astype(BF16)
            kde = stack(kd * g_end).astype(BF16)
            idx = ci * pairs + p
            ge_s[idx] = jnp.broadcast_to(g_end, (LANES, LANES)).T
            pre.append((idx, ag, rg, vst, lhs, rhs, kde, bde))
        tick()
        gs = [_dot_nt(t[4], t[5]) for t in pre]
        tick()
        npows, xs_, avs, arbs = [], [], [], []
        for t, g in zip(pre, gs):
            a_ab = jnp.where(strict, g[0:2 * c, 0:2 * c], 0.0)
            a_ak = jnp.where(strict, g[0:2 * c, 2 * c:], 0.0)
            a_rb = jnp.where(incl, g[2 * c:, 0:2 * c], 0.0)
            a_rk = jnp.where(incl, g[2 * c:, 2 * c:], 0.0)
            arbs.append(a_rb.astype(BF16))
            npows.append(a_ab)
            avs.append(jnp.concatenate([a_ak, a_rk], axis=0).astype(BF16))
        avs = [_dot(av, t[3]) for av, t in zip(avs, pre)]
        svs = [_dot_tn(t[6], t[3]) for t in pre]
        tick()
        for t, av in zip(pre, avs):
            xs_.append(jnp.concatenate([t[1].astype(F32), av[0:2 * c, :]], axis=1))
        for it in range(6):
            nbs = [n.astype(BF16) for n in npows]
            xbs = [xc.astype(BF16) for xc in xs_]
            if it < 5:
                prods = [_dot(nb, jnp.concatenate([nb, xb], axis=1)) for nb, xb in zip(nbs, xbs)]
                npows = [pr[:, 0:LANES] for pr in prods]
                xs_ = [xc + pr[:, LANES:] for xc, pr in zip(xs_, prods)]
            else:
                xs_ = [xc + _dot(nb, xb) for xc, nb, xb in zip(xs_, nbs, xbs)]
            tick()
        xbs = [xc.astype(BF16) for xc in xs_]
        ras = [_dot(arb, xb) for arb, xb in zip(arbs, xbs)]
        pqs = [_dot_tn(t[7], xb) for xb, t in zip(xbs, pre)]
        for t, av, sv, ra, pq in zip(pre, avs, svs, ras, pqs):
            idx = t[0]
            rp_s[idx] = (t[2].astype(F32) + ra[:, 0:LANES]).astype(BF16)
            yp_s[idx] = ra[:, LANES:] + av[2 * c:, :]
            p_s[idx] = pq[:, 0:LANES].astype(BF16)
            q_s[idx] = pq[:, LANES:] + sv
        tick()

    half_chunks = hb // c
    states = [state_s[p] for p in range(pairs)]

    def finish(part):
        for ci in range(part * half_chunks, (part + 1) * half_chunks):
            for p in range(pairs):
                idx = ci * pairs + p
                sb = states[p].astype(BF16)
                y = _dot(rp_s[idx], sb) + yp_s[idx]
                y_s[ci * c:(ci + 1) * c, p * LANES:(p + 1) * LANES] = y[0:c, :] + y[c:2 * c, :]
                states[p] = states[p] * ge_s[idx] + _dot(p_s[idx], sb) + q_s[idx]
            yield
        rows = slice(part * hb, (part + 1) * hb)
        for p in range(pairs):
            sl = slice(p * LANES, (p + 1) * LANES)
            y = y_s[rows, sl]
            mean = _pair_head_sum(y, low_half) * (1.0 / HEAD_DIM)
            d = y - mean
            var = _pair_head_sum(d * d, low_half) * (1.0 / HEAD_DIM)
            yn = d * lax.rsqrt(var + RW_GN_EPS) * gng_ref[:, sl] + gnb_ref[:, sl]
            o_ref[rows, sl] = ((yn + bonus_s[rows, sl]) * gate_s[rows, sl]).astype(o_ref.dtype)
            yield

    def advance(*gens):
        def tick():
            for gen in gens:
                if gen is not None:
                    next(gen, None)
        return tick

    def drain(gen):
        for _ in gen:
            pass

    n_parts = tb // hb
    drain(prologue(0))
    for part in range(n_parts):
        ahead = prologue(part + 1) if part + 1 < n_parts else None
        behind = finish(part - 1) if part >= 1 else None
        prepare(range(part * half_chunks, (part + 1) * half_chunks), advance(ahead, behind, filler))
        for gen in (ahead, behind):
            if gen is not None:
                drain(gen)
    prev_s[...] = x_ref[tb - 1:tb, :]
    drain(finish(n_parts - 1))
    for p in range(pairs):
        state_s[p] = states[p]


RW_N_IN, RW_N_SCRATCH = 11, 17
GLA_N_IN, GLA_N_SCRATCH = 5, 5


def _recurrent_kernel(*refs):
    n_in = RW_N_IN + GLA_N_IN
    rw_in, gl_in = refs[:RW_N_IN], refs[RW_N_IN:n_in]
    o_rw, o_gl = refs[n_in], refs[n_in + 1]
    rw_scr = refs[n_in + 2:n_in + 2 + RW_N_SCRATCH]
    gl_scr = refs[n_in + 2 + RW_N_SCRATCH:]

    @pl.when(pl.program_id(1) == 0)
    def _():
        for state in (rw_scr[0], rw_scr[1], gl_scr[0]):
            state[...] = jnp.zeros_like(state)

    gla = _gla_body(*gl_in, o_gl, *gl_scr)
    _rw_body(*rw_in, o_rw, *rw_scr, filler=gla)
    for _ in gla:
        pass


def _recurrent_mixers(rw, gl, layer, rprm, gprm, batch, seq):
    n = rw.shape[0]
    tb = RW_BLOCK
    nb = seq // tb
    rows = lambda width: pl.BlockSpec((tb, width), lambda b, i: (b * nb + i, 0))
    rw_names = ("mu", "w0", "a0", "k_k", "k_a", "r_k", "gn_g", "gn_b", "up_hi", "up_lo")
    gl_names = ("gate_b", "norm_g", "up_hi", "up_lo")
    rw_params = [rprm[k] for k in rw_names]
    gl_params = [gprm[k] for k in gl_names]
    assert 1 + len(rw_params) == RW_N_IN and 1 + len(gl_params) == GLA_N_IN
    blk = lambda: pltpu.VMEM((tb, RW_WIDTH), F32)
    n_chain = (tb // CHUNK) * (RW_WIDTH // LANES)
    tile = lambda r, dt: pltpu.VMEM((n_chain, r, LANES), dt)
    rw_scratch = [pltpu.VMEM((RW_WIDTH // LANES, LANES, LANES), F32),
                  pltpu.VMEM((1, RW_IN_WIDTH), F32),
                  blk(), blk(), blk(), blk(), blk(), blk(), blk(), blk(),
                  tile(LANES, BF16), tile(LANES, F32), tile(LANES, BF16), tile(LANES, F32),
                  tile(LANES, F32), blk(), blk()]
    gl_scratch = [pltpu.VMEM((GLA_VAL_WIDTH, GLA_KEY_WIDTH), F32),
                  pltpu.VMEM((tb, GLA_KEY_WIDTH), F32),
                  pltpu.VMEM((tb, GLA_KEY_WIDTH), F32),
                  pltpu.VMEM((tb, GLA_KEY_WIDTH), F32),
                  pltpu.VMEM((tb, GLA_VAL_WIDTH), F32)]
    assert len(rw_scratch) == RW_N_SCRATCH and len(gl_scratch) == GLA_N_SCRATCH
    return pl.pallas_call(
        _recurrent_kernel,
        grid=(batch, nb),
        in_specs=([rows(RW_IN_WIDTH)] + [_layer_block(a, layer, 2) for a in rw_params]
                  + [rows(GLA_PAD_WIDTH)] + [_layer_block(a, layer, 2) for a in gl_params]),
        out_specs=[rows(RW_WIDTH), rows(GLA_VAL_WIDTH)],
        out_shape=[jax.ShapeDtypeStruct((n, RW_WIDTH), BF16),
                   jax.ShapeDtypeStruct((n, GLA_VAL_WIDTH), BF16)],
        scratch_shapes=rw_scratch + gl_scratch,
        compiler_params=_cparams(("parallel", "arbitrary")),
        name="rwkv7_gla",
    )(rw, *rw_params, gl, *gl_params)


GLA_BLOCK = RW_BLOCK
GLA_SUB = 16
GLA_EXP_CAP = 80.0


def _gla_body(x_ref, gb_ref, ng_ref, uphi_ref, uplo_ref, o_ref, state_s, q_s, k_s, la_s, o_s):
    tb = GLA_BLOCK
    c = CHUNK
    heads = GLA_VAL_WIDTH // HEAD_DIM

    kw, vw = GLA_KEY_WIDTH, GLA_VAL_WIDTH
    code = x_ref[:, 2 * kw + 2 * vw:]
    pre = _dot_x3(code, uphi_ref[...], uplo_ref[...]) + gb_ref[...]
    la_s[...] = -(jnp.maximum(-pre, 0.0) + jnp.log(1.0 + jnp.exp(-jnp.abs(pre)))) * (1.0 / GLA_GATE_NORMALIZER)
    q_s[...] = x_ref[:, 0:kw] * (GLA_DK ** -0.5)
    k_s[...] = x_ref[:, kw:2 * kw]
    yield

    tr = lax.broadcasted_iota(jnp.int32, (c, c), 0)
    tc = lax.broadcasted_iota(jnp.int32, (c, c), 1)
    tri = (tc <= tr).astype(BF16)
    sub = GLA_SUB
    n_sub = c // sub
    srow = lax.broadcasted_iota(jnp.int32, (heads * c, c), 0)
    scol = lax.broadcasted_iota(jnp.int32, (heads * c, c), 1)
    causal = scol <= (srow // (heads * sub)) * sub + srow % sub
    klane = lax.broadcasted_iota(jnp.int32, (1, kw), 1) // GLA_DK
    vlane = lax.broadcasted_iota(jnp.int32, (1, vw), 1) // HEAD_DIM
    st_r = lax.broadcasted_iota(jnp.int32, (vw, kw), 0) // HEAD_DIM
    st_c = lax.broadcasted_iota(jnp.int32, (vw, kw), 1) // GLA_DK
    st_mask = st_r == st_c

    n_chunk = tb // c
    pre = []
    for ci in range(n_chunk):
        rows = slice(ci * c, (ci + 1) * c)
        bc = _dot_exact_lhs(tri, la_s[rows, :])
        e_pos = jnp.exp(bc)
        qc = q_s[rows, :]
        kc = k_s[rows, :]
        k2 = (kc * jnp.exp(bc[c - 1:c, :] - bc)).astype(BF16)
        vb = x_ref[rows, 2 * kw:2 * kw + vw].astype(BF16)
        qsts, kes = [], []
        for i in range(n_sub):
            srows = slice(i * sub, (i + 1) * sub)
            ref = bc[i * sub - 1:i * sub, :] if i > 0 else jnp.zeros((1, kw), F32)
            qi = qc[srows, :] * jnp.exp(bc[srows, :] - ref)
            qsts.append(jnp.concatenate([jnp.where(klane == h, qi, 0.0) for h in range(heads)],
                                        axis=0).astype(BF16))
            kes.append((kc * jnp.exp(jnp.minimum(ref - bc, GLA_EXP_CAP))).astype(BF16))
        pre.append(((qc * e_pos).astype(BF16), k2, vb, qsts, kes, e_pos[c - 1:c, :]))
        yield
    scores = []
    for t in pre:
        scores.append(jnp.where(
            causal, jnp.concatenate([_dot_nt(qs, ke) for qs, ke in zip(t[3], t[4])], axis=0),
            0.0).astype(BF16))
        yield
    o_sts = [_dot(sc, t[2]) for sc, t in zip(scores, pre)]
    incs = [jnp.where(st_mask, _dot_tn(t[2], t[1]), 0.0) for t in pre]
    yield
    state = state_s[...]
    for ci in range(n_chunk):
        intra = []
        for i in range(n_sub):
            acc = None
            for h in range(heads):
                r0 = (i * heads + h) * sub
                part = jnp.where(vlane == h, o_sts[ci][r0:r0 + sub, :], 0.0)
                acc = part if acc is None else acc + part
            intra.append(acc)
        o_s[ci * c:(ci + 1) * c, :] = _dot_nt(pre[ci][0], state.astype(BF16)) + jnp.concatenate(intra, axis=0)
        state = state * pre[ci][5] + incs[ci]
        yield
    state_s[...] = state

    lane = lax.broadcasted_iota(jnp.int32, (1, LANES), 1)
    low_half = lane < HEAD_DIM
    outs = []
    for p in range(vw // LANES):
        o = o_s[:, p * LANES:(p + 1) * LANES]
        ms = _pair_head_sum(o * o, low_half) * (1.0 / HEAD_DIM)
        outs.append(o * lax.rsqrt(ms + EPS))
    g = x_ref[:, 2 * kw + vw:2 * kw + 2 * vw]
    on = jnp.concatenate(outs, axis=1) * ng_ref[...]
    o_ref[...] = (on * (g * jax.nn.sigmoid(g))).astype(o_ref.dtype)


MLP_ROWS = 1024
MLP_FF_TILE = 1024


def _mlp_kernel(x_ref, sb_ref, rw_ref, gl_ref, wsb_ref, wrw_ref, wgl_ref, gmix_ref,
                gpre_ref, w1_ref, w2_ref, gpost_ref, o_ref):
    mixed = _dot(sb_ref[...], wsb_ref[...]) + _dot(rw_ref[...], wrw_ref[...]) + _dot(gl_ref[...], wgl_ref[...])
    x1 = x_ref[...] + _rms(mixed, gmix_ref[...])
    h = _rms(x1, gpre_ref[...]).astype(BF16)
    ff = None
    for j in range(D_FF // MLP_FF_TILE):
        cols = slice(j * MLP_FF_TILE, (j + 1) * MLP_FF_TILE)
        a = jnp.maximum(_dot(h, w1_ref[:, cols]), 0.0)
        part = _dot((a * a).astype(BF16), w2_ref[cols, :])
        ff = part if ff is None else ff + part
    o_ref[...] = x1 + _rms(ff, gpost_ref[...])


def _mlp(x2, o_sb, o_rw, o_gl, layer, w_out, g_mix, g_pre, w1, w2, g_post):
    n = x2.shape[0]
    tm = MLP_ROWS
    full = lambda w: _layer_block(w, layer, 1, single_buffer=True)
    row = lambda width: pl.BlockSpec((tm, width), lambda i: (i, 0))

    def out_rows(width, first):
        assert first % width == 0
        return pl.BlockSpec((None, width, D_MODEL), lambda i: (layer, first // width, 0),
                            pipeline_mode=pl.Buffered(1))

    return pl.pallas_call(
        _mlp_kernel,
        grid=(n // tm,),
        in_specs=[row(D_MODEL), row(SB_WIDTH), row(RW_WIDTH), row(GLA_VAL_WIDTH),
                  out_rows(SB_WIDTH, 0), out_rows(RW_WIDTH, SB_WIDTH),
                  out_rows(GLA_VAL_WIDTH, SB_WIDTH + RW_WIDTH), full(g_mix),
                  full(g_pre), full(w1), full(w2), full(g_post)],
        out_specs=row(D_MODEL),
        out_shape=jax.ShapeDtypeStruct((n, D_MODEL), F32),
        compiler_params=_cparams(("parallel",)),
        name="out_proj_mlp",
    )(x2, o_sb, o_rw, o_gl, w_out, w_out, w_out, g_mix, g_pre, w1, w2, g_post)


def _hi_lo(w):
    hi = w.astype(BF16)
    return hi, (w - hi.astype(F32)).astype(BF16)


def _prep_weights(w_in, rw_w_up, rw_a_up, rw_g_up, gla_gate_up, w_out, w_ff1, w_ff2):
    depth = w_in.shape[0]
    sb_end = 3 * SB_WIDTH
    rw_end = sb_end + RW_IN_WIDTH
    w_sb = w_in[:, :, :sb_end].astype(BF16)
    w_rw = w_in[:, :, sb_end:rw_end].astype(BF16)
    g0 = rw_end
    kw, vw, gr = GLA_KEY_WIDTH, GLA_VAL_WIDTH, GLA_GATE_RANK
    w_gl = jnp.concatenate([
        w_in[:, :, g0:g0 + 2 * kw + vw],
        w_in[:, :, g0 + 2 * kw + vw + gr:g0 + GLA_IN_WIDTH],
        w_in[:, :, g0 + 2 * kw + vw:g0 + 2 * kw + vw + gr],
        jnp.zeros((depth, D_MODEL, GLA_PAD_WIDTH - GLA_IN_WIDTH), F32)], axis=2).astype(BF16)
    up = jnp.zeros((depth, RW_CODE, 3 * RW_WIDTH), F32)
    up = up.at[:, 0:32, 0:RW_WIDTH].set(rw_w_up)
    up = up.at[:, 32:64, RW_WIDTH:2 * RW_WIDTH].set(rw_a_up)
    up = up.at[:, 64:128, 2 * RW_WIDTH:].set(rw_g_up)
    rw_up_hi, rw_up_lo = _hi_lo(up)
    gup = jnp.zeros((depth, LANES, GLA_KEY_WIDTH), F32).at[:, 0:gr, :].set(gla_gate_up)
    gl_up_hi, gl_up_lo = _hi_lo(gup)
    return dict(w_sb=w_sb, w_rw=w_rw, w_gl=w_gl, rw_up_hi=rw_up_hi, rw_up_lo=rw_up_lo,
                gl_up_hi=gl_up_hi, gl_up_lo=gl_up_lo, wo=w_out.astype(BF16),
                w1=w_ff1.astype(BF16), w2=w_ff2.astype(BF16))


def kernel(x, pre_mix_g, w_in, sb_norm_g, rw_mu, rw_w0, rw_w_up, rw_a0, rw_a_up, rw_g_up, rw_k_k, rw_k_a, rw_r_k, rw_gn_g, rw_gn_b, gla_gate_up, gla_gate_b, gla_norm_g, w_out, post_mix_g, pre_ffn_g, w_ff1, w_ff2, post_ffn_g):
    batch, seq, d = x.shape
    depth = w_in.shape[0]
    x2 = x.reshape(batch * seq, d)
    vec = lambda t: t[:, None, :]
    wp = _prep_weights(w_in, rw_w_up, rw_a_up, rw_g_up, gla_gate_up, w_out, w_ff1, w_ff2)
    rw_prm = dict(mu=vec(rw_mu), w0=vec(rw_w0), a0=vec(rw_a0), k_k=vec(rw_k_k), k_a=vec(rw_k_a),
                  r_k=vec(rw_r_k), gn_g=vec(rw_gn_g), gn_b=vec(rw_gn_b),
                  up_hi=wp["rw_up_hi"], up_lo=wp["rw_up_lo"])
    gl_prm = dict(gate_b=vec(gla_gate_b), norm_g=vec(gla_norm_g),
                  up_hi=wp["gl_up_hi"], up_lo=wp["gl_up_lo"])
    for l in range(depth):
        sb, rw, gl = _inproj(x2, l, vec(pre_mix_g), wp["w_sb"], wp["w_rw"], wp["w_gl"])
        o_sb = _sb_attention(sb, l, vec(sb_norm_g), batch, seq)
        o_rw, o_gl = _recurrent_mixers(rw, gl, l, rw_prm, gl_prm, batch, seq)
        x2 = _mlp(x2, o_sb, o_rw, o_gl, l, wp["wo"], vec(post_mix_g), vec(pre_ffn_g),
                  wp["w1"], wp["w2"], vec(post_ffn_g))
    return x2.reshape(batch, seq, d)
```

```python
import functools

import jax
import jax.numpy as jnp
from jax import lax
from jax.experimental import pallas as pl
from jax.experimental.pallas import tpu as pltpu

F32 = jnp.float32
BF16 = jnp.bfloat16

D_MODEL = 1024
HEAD_DIM = 64
LANES = 128
SB_WIDTH = 384
RW_WIDTH = 384
RW_IN_WIDTH = 1280
RW_CODE = 128
GLA_KEY_WIDTH = 128
GLA_VAL_WIDTH = 256
GLA_DK = 32
GLA_GATE_RANK = 16
GLA_IN_WIDTH = 784
GLA_PAD_WIDTH = 896
GLA_GATE_NORMALIZER = 16.0
D_FF = 4096
EPS = 1e-6
RW_GN_EPS = 64e-5
CHUNK = 64
VMEM_LIMIT = 58 * 1024 * 1024

SB_BLOCK = 256
LOG2E = 1.4426950408889634
SB_DEAD = 110.0 * LOG2E
SB_CLAMP = 120.0
SB_LAG = 4


def _cparams(sem):
    return pltpu.CompilerParams(dimension_semantics=sem, vmem_limit_bytes=VMEM_LIMIT)


def _dot(a, b):
    return jnp.dot(a, b, preferred_element_type=F32)


def _dot_nt(a, b):
    return lax.dot_general(a, b, (((1,), (1,)), ((), ())), preferred_element_type=F32)


def _dot_tn(a, b):
    return lax.dot_general(a, b, (((0,), (0,)), ((), ())), preferred_element_type=F32)


def _split2(x):
    hi = x.astype(BF16)
    lo = (x - hi.astype(F32)).astype(BF16)
    return hi, lo


def _split3(x):
    hi = x.astype(BF16)
    r1 = x - hi.astype(F32)
    mid = r1.astype(BF16)
    lo = (r1 - mid.astype(F32)).astype(BF16)
    return hi, mid, lo


def _dot_x3(a, w_hi, w_lo):
    a_hi, a_lo = _split2(a)
    return _dot(a_hi, w_hi) + (_dot(a_lo, w_hi) + _dot(a_hi, w_lo))


def _dot_exact_lhs(m_bf16, x):
    hi, mid, lo = _split3(x)
    return _dot(m_bf16, hi) + (_dot(m_bf16, mid) + _dot(m_bf16, lo))


def _rms(x, gain):
    return x * lax.rsqrt(jnp.mean(x * x, axis=-1, keepdims=True) + EPS) * gain


def _pair_head_sum(x, low_half):
    s_all = jnp.sum(x, axis=-1, keepdims=True)
    s_low = jnp.sum(jnp.where(low_half, x, 0.0), axis=-1, keepdims=True)
    return jnp.where(low_half, s_low, s_all - s_low)


def _inproj_kernel(x_ref, g_ref, wsb_ref, wrw_ref, wgl_ref, sb_ref, rw_ref, gl_ref):
    h = _rms(x_ref[...], g_ref[...]).astype(BF16)
    col = lax.broadcasted_iota(jnp.int32, (1, 3 * SB_WIDTH), 1)
    qscale = jnp.where(col < SB_WIDTH, LOG2E * HEAD_DIM ** -0.5, 1.0)
    sb_ref[...] = (_dot(h, wsb_ref[...]) * qscale).astype(BF16)
    rw_ref[...] = _dot(h, wrw_ref[...])
    gl_ref[...] = _dot(h, wgl_ref[...])


INPROJ_ROWS = 1024


def _layer_block(arr, layer, grid_rank, single_buffer=False):
    index = lambda *_: (layer, 0, 0)
    assert arr.ndim == 3 and grid_rank in (1, 2)
    if single_buffer:
        return pl.BlockSpec((None,) + arr.shape[1:], index, pipeline_mode=pl.Buffered(1))
    return pl.BlockSpec((None,) + arr.shape[1:], index)


def _inproj(x2, layer, gain, w_sb, w_rw, w_gl, tm=INPROJ_ROWS):
    n = x2.shape[0]
    full = lambda w: _layer_block(w, layer, 1, single_buffer=True)
    row = lambda width: pl.BlockSpec((tm, width), lambda i: (i, 0))
    return pl.pallas_call(
        _inproj_kernel,
        grid=(n // tm,),
        in_specs=[row(D_MODEL), full(gain), full(w_sb), full(w_rw), full(w_gl)],
        out_specs=[row(3 * SB_WIDTH), row(RW_IN_WIDTH), row(GLA_PAD_WIDTH)],
        out_shape=[jax.ShapeDtypeStruct((n, 3 * SB_WIDTH), BF16),
                   jax.ShapeDtypeStruct((n, RW_IN_WIDTH), F32),
                   jax.ShapeDtypeStruct((n, GLA_PAD_WIDTH), F32)],
        compiler_params=_cparams(("parallel",)),
        name="in_proj",
    )(x2, gain, w_sb, w_rw, w_gl)


def _sb_kernel(q_ref, k_ref, v_ref, g_ref, o_ref):
    qb = SB_BLOCK
    qi = pl.program_id(1)
    lane = lax.broadcasted_iota(jnp.int32, (1, LANES), 1)
    low_half = lane < HEAD_DIM
    row = lax.broadcasted_iota(jnp.int32, (qb, qb), 0)
    col = lax.broadcasted_iota(jnp.int32, (qb, qb), 1)
    causal = col < row
    suffix = (row > col).astype(BF16)
    half = qb // 2
    pairs = SB_WIDTH // LANES
    zero = jnp.zeros((), BF16)
    vmasks = (low_half, jnp.logical_not(low_half))
    chains = [(h, r) for r in range(2) for h in range(2)]
    lanes_of = [slice(pr * LANES, (pr + 1) * LANES) for pr in range(pairs)]
    qcs = []
    for sl in lanes_of:
        q = q_ref[:, sl]
        qhs = [jnp.where(m, q, zero) for m in vmasks]
        qcs.append([qhs[h][r * half:(r + 1) * half, :] for h, r in chains])

    def front(qc, kb, n, mask):
        return front_tail(_dot_nt(qc, kb[0:n, :]), n, mask)

    def front_tail(z, n, mask):
        p = jnp.maximum(z, jnp.log2(1.0 + jnp.exp2(jnp.minimum(z, SB_CLAMP))))
        log_beta = z - p
        if mask is not None:
            p = jnp.where(mask, p, 0.0)
        later = _dot(p.astype(BF16), suffix[0:n, 0:n])
        return log_beta, p[:, 0:1], later

    def back(log_beta, p_first, later, spent, mask):
        w = jnp.exp2(log_beta - (later + spent))
        if mask is not None:
            w = jnp.where(mask, w, 0.0)
        return w.astype(BF16), spent + (later[:, 0:1] + p_first)

    def values(vb, n):
        return [jnp.where(m, vb[0:n, :], zero) for m in vmasks]

    d_start = pl.multiple_of(qi * qb, qb)
    p_start = pl.multiple_of(jnp.maximum(qi - 1, 0) * qb, qb)
    d_keys = [half if r == 0 else qb for _, r in chains]
    d_masks = [causal[0:half, 0:half] if r == 0 else causal[half:, :] for _, r in chains]
    seq_chains = []
    for pr in range(pairs):
        for prev in (False, True):
            for ci, (h, r) in enumerate(chains):
                seq_chains.append((pr, prev, ci, h, r))
    n_seq = len(seq_chains)
    p_vmasks = [jnp.logical_and(m, qi > 0) for m in vmasks]

    logits, fronts, spent_of = [None] * n_seq, [None] * n_seq, {}
    acc_of = {}
    def span(prev, r):
        if prev:
            return (0, qb) if r == 0 else (half, half)
        return (0, half) if r == 0 else (0, qb)

    for t in range(n_seq + 2 * SB_LAG):
        c2, c1 = t - 2 * SB_LAG, t - SB_LAG
        if 0 <= c2 < n_seq:
            pr, prev, ci, h, r = seq_chains[c2]
            k0, n = span(prev, r)
            mask = None if prev else d_masks[ci]
            spent = spent_of[(pr, ci)] if prev else jnp.zeros((half, 1), F32)
            log_beta, p_first, later = fronts[c2]
            fronts[c2] = None
            w, spent_of[(pr, ci)] = back(log_beta, p_first, later, spent, mask)
            start = p_start if prev else d_start
            vb = v_ref[pl.ds(start, qb), lanes_of[pr]]
            vh = jnp.where(p_vmasks[h] if prev else vmasks[h], vb[k0:k0 + n, :], zero)
            part = _dot(w, vh)
            acc_of[(pr, r)] = part if (pr, r) not in acc_of else acc_of[(pr, r)] + part
        if 0 <= c1 < n_seq:
            pr, prev, ci, h, r = seq_chains[c1]
            mask = None if prev else d_masks[ci]
            fronts[c1] = front_tail(logits[c1], span(prev, r)[1], mask)
            logits[c1] = None
        if t < n_seq:
            pr, prev, ci, h, r = seq_chains[t]
            k0, n = span(prev, r)
            start = p_start if prev else d_start
            kb = k_ref[pl.ds(start, qb), lanes_of[pr]]
            logits[t] = _dot_nt(qcs[pr][ci], kb[k0:k0 + n, :])
    accs_all = [[acc_of[(pr, r)] for r in range(2)] for pr in range(pairs)]
    bp_spent = [[spent_of[(pr, ci)] for ci in range(len(chains))] for pr in range(pairs)]

    late = [ci for ci, (_, r) in enumerate(chains) if r == 1]

    def least(spents):
        m = spents[0]
        for sp in spents[1:]:
            m = jnp.minimum(m, sp)
        return jnp.min(m)

    far_needed = [jnp.logical_and(qi > 0, least([bp_spent[pr][ci] for ci in late]) < SB_DEAD)
                  for pr in range(pairs)]
    more_needed = [jnp.logical_and(qi > 1, least(bp_spent[pr]) < SB_DEAD) for pr in range(pairs)]

    finals = []
    for pr, sl in enumerate(lanes_of):
        def far_half(args, pr=pr, sl=sl):
            acc, spents = args
            kb, vb = k_ref[pl.ds(p_start, half), sl], v_ref[pl.ds(p_start, half), sl]
            fs = [front(qcs[pr][ci], kb, half, None) for ci in late]
            bs = [back(*f, sp, None) for f, sp in zip(fs, spents)]
            vh = jnp.concatenate(values(vb, half), axis=0)
            return acc + _dot(jnp.concatenate([b[0] for b in bs], axis=1), vh), [b[1] for b in bs]

        def cond(s):
            j, spents, _ = s
            return jnp.logical_and(j >= 0, least(spents) < SB_DEAD)

        def body(s, pr=pr, sl=sl):
            j, spents, accs = s
            start = pl.multiple_of(j * qb, qb)
            kb, vb = k_ref[pl.ds(start, qb), sl], v_ref[pl.ds(start, qb), sl]
            fs = [front(qc, kb, qb, None) for qc in qcs[pr]]
            bs = [back(*f, sp, None) for f, sp in zip(fs, spents)]
            vh = jnp.concatenate(values(vb, qb), axis=0)
            accs = [acc + _dot(jnp.concatenate([bs[2 * r][0], bs[2 * r + 1][0]], axis=1), vh)
                    for r, acc in enumerate(accs)]
            return j - 1, [b[1] for b in bs], accs

        def sweep_on(args, pr=pr, far_half=far_half, cond=cond, body=body):
            accs, spents = args
            late_spent = [spents[ci] for ci in late]
            acc_late, late_spent = lax.cond(far_needed[pr], far_half, lambda a: a, (accs[1], late_spent))
            spents = list(spents)
            for ci, sp in zip(late, late_spent):
                spents[ci] = sp
            _, _, accs = lax.while_loop(cond, body, (qi - 2, spents, [accs[0], acc_late]))
            return accs, spents

        accs, _ = lax.cond(jnp.logical_or(far_needed[pr], more_needed[pr]), sweep_on, lambda a: a,
                           (accs_all[pr], bp_spent[pr]))
        finals.append(jnp.concatenate(accs, axis=0))

    outs = []
    for acc in finals:
        ms = _pair_head_sum(acc * acc, low_half) * (1.0 / HEAD_DIM)
        outs.append(acc * lax.rsqrt(ms + EPS))
    o_ref[...] = (jnp.concatenate(outs, axis=1) * g_ref[...]).astype(o_ref.dtype)


def _sb_attention(sb, layer, gain, batch, seq):
    nq = seq // SB_BLOCK
    n = sb.shape[0]
    pairs = SB_WIDTH // LANES
    return pl.pallas_call(
        _sb_kernel,
        grid=(batch, nq),
        in_specs=[
            pl.BlockSpec((SB_BLOCK, SB_WIDTH), lambda b, i: (b * nq + i, 0)),
            pl.BlockSpec((seq, SB_WIDTH), lambda b, i: (b, 1)),
            pl.BlockSpec((seq, SB_WIDTH), lambda b, i: (b, 2)),
            _layer_block(gain, layer, 2),
        ],
        out_specs=pl.BlockSpec((SB_BLOCK, SB_WIDTH), lambda b, i: (b * nq + i, 0)),
        out_shape=jax.ShapeDtypeStruct((n, SB_WIDTH), BF16),
        compiler_params=_cparams(("parallel", "arbitrary")),
        name="sb_attention",
    )(sb, sb, sb, gain)


RW_BLOCK = 512
RW_PART = 128


def _rw_body(x_ref, mu_ref, w0_ref, a0_ref, kk_ref, ka_ref, rk_ref, gng_ref, gnb_ref,
             uphi_ref, uplo_ref, o_ref,
             state_s, prev_s, r_s, k_s, v_s, lw_s, a_s, b_s, y_s, lc_s,
             rp_s, yp_s, p_s, q_s, ge_s, bonus_s, gate_s, filler=None):
    tb = RW_BLOCK
    hb = RW_PART
    c = CHUNK
    pairs = RW_WIDTH // LANES

    lane = lax.broadcasted_iota(jnp.int32, (1, LANES), 1)
    low_half = lane < HEAD_DIM
    tr = lax.broadcasted_iota(jnp.int32, (hb, hb), 0)
    tc = lax.broadcasted_iota(jnp.int32, (hb, hb), 1)
    tri = jnp.logical_and(tr // c == tc // c, tc <= tr).astype(BF16)
    first = lax.broadcasted_iota(jnp.int32, (hb, 1), 0) == 0

    def prologue(part):
        rows = slice(part * hb, (part + 1) * hb)
        x = x_ref[rows, :]
        before = prev_s[...] if part == 0 else x_ref[part * hb - 1:part * hb, :]
        prev = jnp.where(first, before, pltpu.roll(x, 1, 0))
        xs = x + (prev - x) * mu_ref[...]
        yield
        r = xs[:, 0:RW_WIDTH]
        k = xs[:, RW_WIDTH:2 * RW_WIDTH]
        v = xs[:, 2 * RW_WIDTH:3 * RW_WIDTH]
        code = xs[:, 3 * RW_WIDTH:]
        act = jnp.where(lane < 32, jnp.tanh(code), jnp.where(lane < 64, code, jax.nn.sigmoid(code)))
        up = jnp.concatenate(
            [_dot_x3(act, uphi_ref[:, 0:RW_WIDTH], uplo_ref[:, 0:RW_WIDTH]),
             _dot(act.astype(BF16), uphi_ref[:, RW_WIDTH:])], axis=1)
        r_s[rows, :] = r
        v_s[rows, :] = v
        gate_s[rows, :] = up[:, 2 * RW_WIDTH:]
        yield
        wpre = w0_ref[...] + up[:, 0:RW_WIDTH]
        log_w = -(jnp.maximum(-wpre, 0.0) + jnp.log(1.0 + jnp.exp(-jnp.abs(wpre)))) - 0.5
        lw = -jnp.exp(log_w)
        lw_s[rows, :] = lw
        yield
        lc_s[rows, :] = _dot_exact_lhs(tri, lw)
        yield
        a = jax.nn.sigmoid(a0_ref[...] + up[:, RW_WIDTH:2 * RW_WIDTH])
        kmod = k * (1.0 + (a - 1.0) * ka_ref[...])
        k_s[rows, :] = kmod
        yield
        kk = k * kk_ref[...]
        bonus_in = r * kmod * rk_ref[...]
        for p in range(pairs):
            sl = slice(p * LANES, (p + 1) * LANES)
            ss = _pair_head_sum(kk[:, sl] * kk[:, sl], low_half)
            kkn = kk[:, sl] * lax.rsqrt(jnp.maximum(ss, 1e-12))
            a_s[rows, sl] = -kkn
            b_s[rows, sl] = kkn * a[:, sl]
            bonus_s[rows, sl] = _pair_head_sum(bonus_in[:, sl], low_half) * v[:, sl]
            yield

    rr = lax.broadcasted_iota(jnp.int32, (2 * c, 2 * c), 0)
    cc = lax.broadcasted_iota(jnp.int32, (2 * c, 2 * c), 1)
    same_head = (rr // c) == (cc // c)
    strict = jnp.logical_and(same_head, cc < rr)
    incl = jnp.logical_and(same_head, cc <= rr)
    eye = (rr == cc).astype(F32)
    m_lo = low_half
    m_hi = jnp.logical_not(low_half)

    def stack(xp):
        return jnp.concatenate([jnp.where(m_lo, xp, 0.0), jnp.where(m_hi, xp, 0.0)], axis=0)

    def prepare(chunks, tick):
        chains = [(ci, p) for ci in chunks for p in range(pairs)]
        pre = []
        for ci, p in chains:
            rows = slice(ci * c, (ci + 1) * c)
            sl = slice(p * LANES, (p + 1) * LANES)
            lwc = lw_s[rows, sl]
            lc = lc_s[rows, sl]
            e_incl = jnp.exp(lc)
            e_excl = jnp.exp(lc - lwc)
            e_inv = jnp.exp(-lc)
            g_end = e_incl[c - 1:c, :]
            ag = stack(a_s[rows, sl] * e_excl).astype(BF16)
            rg = stack(r_s[rows, sl] * e_incl).astype(BF16)
            bd = b_s[rows, sl] * e_inv
            kd = k_s[rows, sl] * e_inv
            vst = stack(v_s[rows, sl]).astype(BF16)
            lhs = jnp.concatenate([ag, rg], axis=0)
            rhs = jnp.concatenate([bd, bd, kd, kd], axis=0).astype(BF16)
            bde = stack(bd * g_end).astype(BF16)
            kde = stack(kd * g_end).astype(BF16)
            idx = ci * pairs + p
            ge_s[idx] = jnp.broadcast_to(g_end, (LANES, LANES)).T
            pre.append((idx, ag, rg, vst, lhs, rhs, kde, bde))
        tick()
        gs = [_dot_nt(t[4], t[5]) for t in pre]
        tick()
        npows, xs_, avs, arbs = [], [], [], []
        for t, g in zip(pre, gs):
            a_ab = jnp.where(strict, g[0:2 * c, 0:2 * c], 0.0)
            a_ak = jnp.where(strict, g[0:2 * c, 2 * c:], 0.0)
            a_rb = jnp.where(incl, g[2 * c:, 0:2 * c], 0.0)
            a_rk = jnp.where(incl, g[2 * c:, 2 * c:], 0.0)
            arbs.append(a_rb.astype(BF16))
            npows.append(a_ab)
            avs.append(jnp.concatenate([a_ak, a_rk], axis=0).astype(BF16))
        avs = [_dot(av, t[3]) for av, t in zip(avs, pre)]
        svs = [_dot_tn(t[6], t[3]) for t in pre]
        tick()
        for t, av in zip(pre, avs):
            xs_.append(jnp.concatenate([t[1], av[0:2 * c, :].astype(BF16)], axis=1))
        tinvs = [eye + n for n in npows]
        nbs = [n.astype(BF16) for n in npows]
        npows = [_dot(nb, nb) for nb in nbs]
        tick()
        for it in range(1, 6):
            nbs = [n.astype(BF16) for n in npows]
            tbs = [tv.astype(BF16) for tv in tinvs]
            if it < 5:
                prods = [_dot(nb, jnp.concatenate([nb, tb], axis=1)) for nb, tb in zip(nbs, tbs)]
                npows = [pr[:, 0:LANES] for pr in prods]
                tinvs = [tv + pr[:, LANES:] for tv, pr in zip(tinvs, prods)]
            else:
                tinvs = [tv + _dot(nb, tb) for tv, nb, tb in zip(tinvs, nbs, tbs)]
            tick()
        xbs = [_dot(tv.astype(BF16), x0).astype(BF16) for tv, x0 in zip(tinvs, xs_)]
        ras = [_dot(arb, xb) for arb, xb in zip(arbs, xbs)]
        pqs = [_dot_tn(t[7], xb) for xb, t in zip(xbs, pre)]
        for t, av, sv, ra, pq in zip(pre, avs, svs, ras, pqs):
            idx = t[0]
            rp_s[idx] = (t[2].astype(F32) + ra[:, 0:LANES]).astype(BF16)
            yp_s[idx] = ra[:, LANES:] + av[2 * c:, :]
            p_s[idx] = pq[:, 0:LANES].astype(BF16)
            q_s[idx] = pq[:, LANES:] + sv
        tick()

    half_chunks = hb // c
    states = [state_s[p] for p in range(pairs)]

    def finish(part):
        for ci in range(part * half_chunks, (part + 1) * half_chunks):
            for p in range(pairs):
                idx = ci * pairs + p
                sb = states[p].astype(BF16)
                y = _dot(rp_s[idx], sb) + yp_s[idx]
                y_s[ci * c:(ci + 1) * c, p * LANES:(p + 1) * LANES] = y[0:c, :] + y[c:2 * c, :]
                states[p] = states[p] * ge_s[idx] + _dot(p_s[idx], sb) + q_s[idx]
            yield
        rows = slice(part * hb, (part + 1) * hb)
        for p in range(pairs):
            sl = slice(p * LANES, (p + 1) * LANES)
            y = y_s[rows, sl]
            mean = _pair_head_sum(y, low_half) * (1.0 / HEAD_DIM)
            d = y - mean
            var = _pair_head_sum(d * d, low_half) * (1.0 / HEAD_DIM)
            yn = d * lax.rsqrt(var + RW_GN_EPS) * gng_ref[:, sl] + gnb_ref[:, sl]
            o_ref[rows, sl] = ((yn + bonus_s[rows, sl]) * gate_s[rows, sl]).astype(o_ref.dtype)
            yield

    def advance(*gens):
        def tick():
            for gen in gens:
                if gen is not None:
                    next(gen, None)
        return tick

    def drain(gen):
        for _ in gen:
            pass

    n_parts = tb // hb
    drain(prologue(0))
    for part in range(n_parts):
        ahead = prologue(part + 1) if part + 1 < n_parts else None
        behind = finish(part - 1) if part >= 1 else None
        prepare(range(part * half_chunks, (part + 1) * half_chunks), advance(ahead, behind, filler))
        for gen in (ahead, behind):
            if gen is not None:
                drain(gen)
    prev_s[...] = x_ref[tb - 1:tb, :]
    drain(finish(n_parts - 1))
    for p in range(pairs):
        state_s[p] = states[p]


RW_N_IN, RW_N_SCRATCH = 11, 17
GLA_N_IN, GLA_N_SCRATCH = 5, 5


def _recurrent_kernel(*refs):
    n_in = RW_N_IN + GLA_N_IN
    rw_in, gl_in = refs[:RW_N_IN], refs[RW_N_IN:n_in]
    o_rw, o_gl = refs[n_in], refs[n_in + 1]
    rw_scr = refs[n_in + 2:n_in + 2 + RW_N_SCRATCH]
    gl_scr = refs[n_in + 2 + RW_N_SCRATCH:]

    @pl.when(pl.program_id(1) == 0)
    def _():
        for state in (rw_scr[0], rw_scr[1], gl_scr[0]):
            state[...] = jnp.zeros_like(state)

    gla = _gla_body(*gl_in, o_gl, *gl_scr)
    _rw_body(*rw_in, o_rw, *rw_scr, filler=gla)
    for _ in gla:
        pass


def _recurrent_mixers(rw, gl, layer, rprm, gprm, batch, seq):
    n = rw.shape[0]
    tb = RW_BLOCK
    nb = seq // tb
    rows = lambda width: pl.BlockSpec((tb, width), lambda b, i: (b * nb + i, 0))
    rw_names = ("mu", "w0", "a0", "k_k", "k_a", "r_k", "gn_g", "gn_b", "up_hi", "up_lo")
    gl_names = ("gate_b", "norm_g", "up_hi", "up_lo")
    rw_params = [rprm[k] for k in rw_names]
    gl_params = [gprm[k] for k in gl_names]
    assert 1 + len(rw_params) == RW_N_IN and 1 + len(gl_params) == GLA_N_IN
    blk = lambda: pltpu.VMEM((tb, RW_WIDTH), F32)
    n_chain = (tb // CHUNK) * (RW_WIDTH // LANES)
    tile = lambda r, dt: pltpu.VMEM((n_chain, r, LANES), dt)
    rw_scratch = [pltpu.VMEM((RW_WIDTH // LANES, LANES, LANES), F32),
                  pltpu.VMEM((1, RW_IN_WIDTH), F32),
                  blk(), blk(), blk(), blk(), blk(), blk(), blk(), blk(),
                  tile(LANES, BF16), tile(LANES, F32), tile(LANES, BF16), tile(LANES, F32),
                  tile(LANES, F32), blk(), blk()]
    gl_scratch = [pltpu.VMEM((GLA_VAL_WIDTH, GLA_KEY_WIDTH), F32),
                  pltpu.VMEM((tb, GLA_KEY_WIDTH), F32),
                  pltpu.VMEM((tb, GLA_KEY_WIDTH), F32),
                  pltpu.VMEM((tb, GLA_KEY_WIDTH), F32),
                  pltpu.VMEM((tb, GLA_VAL_WIDTH), F32)]
    assert len(rw_scratch) == RW_N_SCRATCH and len(gl_scratch) == GLA_N_SCRATCH
    return pl.pallas_call(
        _recurrent_kernel,
        grid=(batch, nb),
        in_specs=([rows(RW_IN_WIDTH)] + [_layer_block(a, layer, 2) for a in rw_params]
                  + [rows(GLA_PAD_WIDTH)] + [_layer_block(a, layer, 2) for a in gl_params]),
        out_specs=[rows(RW_WIDTH), rows(GLA_VAL_WIDTH)],
        out_shape=[jax.ShapeDtypeStruct((n, RW_WIDTH), BF16),
                   jax.ShapeDtypeStruct((n, GLA_VAL_WIDTH), BF16)],
        scratch_shapes=rw_scratch + gl_scratch,
        compiler_params=_cparams(("parallel", "arbitrary")),
        name="rwkv7_gla",
    )(rw, *rw_params, gl, *gl_params)


GLA_BLOCK = RW_BLOCK
GLA_SUB = 16
GLA_EXP_CAP = 80.0


def _gla_body(x_ref, gb_ref, ng_ref, uphi_ref, uplo_ref, o_ref, state_s, q_s, k_s, la_s, o_s):
    tb = GLA_BLOCK
    c = CHUNK
    heads = GLA_VAL_WIDTH // HEAD_DIM

    kw, vw = GLA_KEY_WIDTH, GLA_VAL_WIDTH
    code = x_ref[:, 2 * kw + 2 * vw:]
    pre = _dot_x3(code, uphi_ref[...], uplo_ref[...]) + gb_ref[...]
    la_s[...] = -(jnp.maximum(-pre, 0.0) + jnp.log(1.0 + jnp.exp(-jnp.abs(pre)))) * (1.0 / GLA_GATE_NORMALIZER)
    q_s[...] = x_ref[:, 0:kw] * (GLA_DK ** -0.5)
    k_s[...] = x_ref[:, kw:2 * kw]
    yield

    tr = lax.broadcasted_iota(jnp.int32, (c, c), 0)
    tc = lax.broadcasted_iota(jnp.int32, (c, c), 1)
    tri = (tc <= tr).astype(BF16)
    sub = GLA_SUB
    n_sub = c // sub
    srow = lax.broadcasted_iota(jnp.int32, (heads * c, c), 0)
    scol = lax.broadcasted_iota(jnp.int32, (heads * c, c), 1)
    causal = scol <= (srow // (heads * sub)) * sub + srow % sub
    klane = lax.broadcasted_iota(jnp.int32, (1, kw), 1) // GLA_DK
    vlane = lax.broadcasted_iota(jnp.int32, (1, vw), 1) // HEAD_DIM
    st_r = lax.broadcasted_iota(jnp.int32, (vw, kw), 0) // HEAD_DIM
    st_c = lax.broadcasted_iota(jnp.int32, (vw, kw), 1) // GLA_DK
    st_mask = st_r == st_c

    n_chunk = tb // c
    pre = []
    for ci in range(n_chunk):
        rows = slice(ci * c, (ci + 1) * c)
        bc = _dot_exact_lhs(tri, la_s[rows, :])
        e_pos = jnp.exp(bc)
        qc = q_s[rows, :]
        kc = k_s[rows, :]
        k2 = (kc * jnp.exp(bc[c - 1:c, :] - bc)).astype(BF16)
        vb = x_ref[rows, 2 * kw:2 * kw + vw].astype(BF16)
        qsts, kes = [], []
        for i in range(n_sub):
            srows = slice(i * sub, (i + 1) * sub)
            ref = bc[i * sub - 1:i * sub, :] if i > 0 else jnp.zeros((1, kw), F32)
            qi = qc[srows, :] * jnp.exp(bc[srows, :] - ref)
            qsts.append(jnp.concatenate([jnp.where(klane == h, qi, 0.0) for h in range(heads)],
                                        axis=0).astype(BF16))
            kes.append((kc * jnp.exp(jnp.minimum(ref - bc, GLA_EXP_CAP))).astype(BF16))
        pre.append(((qc * e_pos).astype(BF16), k2, vb, qsts, kes, e_pos[c - 1:c, :]))
        yield
    scores = []
    for t in pre:
        scores.append(jnp.where(
            causal, jnp.concatenate([_dot_nt(qs, ke) for qs, ke in zip(t[3], t[4])], axis=0),
            0.0).astype(BF16))
        yield
    o_sts = [_dot(sc, t[2]) for sc, t in zip(scores, pre)]
    incs = [jnp.where(st_mask, _dot_tn(t[2], t[1]), 0.0) for t in pre]
    yield
    state = state_s[...]
    for ci in range(n_chunk):
        intra = []
        for i in range(n_sub):
            acc = None
            for h in range(heads):
                r0 = (i * heads + h) * sub
                part = jnp.where(vlane == h, o_sts[ci][r0:r0 + sub, :], 0.0)
                acc = part if acc is None else acc + part
            intra.append(acc)
        o_s[ci * c:(ci + 1) * c, :] = _dot_nt(pre[ci][0], state.astype(BF16)) + jnp.concatenate(intra, axis=0)
        state = state * pre[ci][5] + incs[ci]
        yield
    state_s[...] = state

    lane = lax.broadcasted_iota(jnp.int32, (1, LANES), 1)
    low_half = lane < HEAD_DIM
    outs = []
    for p in range(vw // LANES):
        o = o_s[:, p * LANES:(p + 1) * LANES]
        ms = _pair_head_sum(o * o, low_half) * (1.0 / HEAD_DIM)
        outs.append(o * lax.rsqrt(ms + EPS))
    g = x_ref[:, 2 * kw + vw:2 * kw + 2 * vw]
    on = jnp.concatenate(outs, axis=1) * ng_ref[...]
    o_ref[...] = (on * (g * jax.nn.sigmoid(g))).astype(o_ref.dtype)


MLP_ROWS = 1024
MLP_FF_TILE = 1024


def _mlp_kernel(x_ref, sb_ref, rw_ref, gl_ref, wsb_ref, wrw_ref, wgl_ref, gmix_ref,
                gpre_ref, w1_ref, w2_ref, gpost_ref, o_ref):
    mixed = _dot(sb_ref[...], wsb_ref[...]) + _dot(rw_ref[...], wrw_ref[...]) + _dot(gl_ref[...], wgl_ref[...])
    x1 = x_ref[...] + _rms(mixed, gmix_ref[...])
    h = _rms(x1, gpre_ref[...]).astype(BF16)
    ff = None
    for j in range(D_FF // MLP_FF_TILE):
        cols = slice(j * MLP_FF_TILE, (j + 1) * MLP_FF_TILE)
        a = jnp.maximum(_dot(h, w1_ref[:, cols]), 0.0)
        part = _dot((a * a).astype(BF16), w2_ref[cols, :])
        ff = part if ff is None else ff + part
    o_ref[...] = x1 + _rms(ff, gpost_ref[...])


def _mlp(x2, o_sb, o_rw, o_gl, layer, w_out, g_mix, g_pre, w1, w2, g_post):
    n = x2.shape[0]
    tm = MLP_ROWS
    full = lambda w: _layer_block(w, layer, 1, single_buffer=True)
    row = lambda width: pl.BlockSpec((tm, width), lambda i: (i, 0))

    def out_rows(width, first):
        assert first % width == 0
        return pl.BlockSpec((None, width, D_MODEL), lambda i: (layer, first // width, 0),
                            pipeline_mode=pl.Buffered(1))

    return pl.pallas_call(
        _mlp_kernel,
        grid=(n // tm,),
        in_specs=[row(D_MODEL), row(SB_WIDTH), row(RW_WIDTH), row(GLA_VAL_WIDTH),
                  out_rows(SB_WIDTH, 0), out_rows(RW_WIDTH, SB_WIDTH),
                  out_rows(GLA_VAL_WIDTH, SB_WIDTH + RW_WIDTH), full(g_mix),
                  full(g_pre), full(w1), full(w2), full(g_post)],
        out_specs=row(D_MODEL),
        out_shape=jax.ShapeDtypeStruct((n, D_MODEL), F32),
        compiler_params=_cparams(("parallel",)),
        name="out_proj_mlp",
    )(x2, o_sb, o_rw, o_gl, w_out, w_out, w_out, g_mix, g_pre, w1, w2, g_post)


def _hi_lo(w):
    hi = w.astype(BF16)
    return hi, (w - hi.astype(F32)).astype(BF16)


def _prep_weights(w_in, rw_w_up, rw_a_up, rw_g_up, gla_gate_up, w_out, w_ff1, w_ff2):
    depth = w_in.shape[0]
    sb_end = 3 * SB_WIDTH
    rw_end = sb_end + RW_IN_WIDTH
    w_sb = w_in[:, :, :sb_end].astype(BF16)
    w_rw = w_in[:, :, sb_end:rw_end].astype(BF16)
    g0 = rw_end
    kw, vw, gr = GLA_KEY_WIDTH, GLA_VAL_WIDTH, GLA_GATE_RANK
    w_gl = jnp.concatenate([
        w_in[:, :, g0:g0 + 2 * kw + vw],
        w_in[:, :, g0 + 2 * kw + vw + gr:g0 + GLA_IN_WIDTH],
        w_in[:, :, g0 + 2 * kw + vw:g0 + 2 * kw + vw + gr],
        jnp.zeros((depth, D_MODEL, GLA_PAD_WIDTH - GLA_IN_WIDTH), F32)], axis=2).astype(BF16)
    up = jnp.zeros((depth, RW_CODE, 3 * RW_WIDTH), F32)
    up = up.at[:, 0:32, 0:RW_WIDTH].set(rw_w_up)
    up = up.at[:, 32:64, RW_WIDTH:2 * RW_WIDTH].set(rw_a_up)
    up = up.at[:, 64:128, 2 * RW_WIDTH:].set(rw_g_up)
    rw_up_hi, rw_up_lo = _hi_lo(up)
    gup = jnp.zeros((depth, LANES, GLA_KEY_WIDTH), F32).at[:, 0:gr, :].set(gla_gate_up)
    gl_up_hi, gl_up_lo = _hi_lo(gup)
    return dict(w_sb=w_sb, w_rw=w_rw, w_gl=w_gl, rw_up_hi=rw_up_hi, rw_up_lo=rw_up_lo,
                gl_up_hi=gl_up_hi, gl_up_lo=gl_up_lo, wo=w_out.astype(BF16),
                w1=w_ff1.astype(BF16), w2=w_ff2.astype(BF16))


def kernel(x, pre_mix_g, w_in, sb_norm_g, rw_mu, rw_w0, rw_w_up, rw_a0, rw_a_up, rw_g_up, rw_k_k, rw_k_a, rw_r_k, rw_gn_g, rw_gn_b, gla_gate_up, gla_gate_b, gla_norm_g, w_out, post_mix_g, pre_ffn_g, w_ff1, w_ff2, post_ffn_g):
    batch, seq, d = x.shape
    depth = w_in.shape[0]
    x2 = x.reshape(batch * seq, d)
    vec = lambda t: t[:, None, :]
    wp = _prep_weights(w_in, rw_w_up, rw_a_up, rw_g_up, gla_gate_up, w_out, w_ff1, w_ff2)
    rw_prm = dict(mu=vec(rw_mu), w0=vec(rw_w0), a0=vec(rw_a0), k_k=vec(rw_k_k), k_a=vec(rw_k_a),
                  r_k=vec(rw_r_k), gn_g=vec(rw_gn_g), gn_b=vec(rw_gn_b),
                  up_hi=wp["rw_up_hi"], up_lo=wp["rw_up_lo"])
    gl_prm = dict(gate_b=vec(gla_gate_b), norm_g=vec(gla_norm_g),
                  up_hi=wp["gl_up_hi"], up_lo=wp["gl_up_lo"])
    for l in range(depth):
        sb, rw, gl = _inproj(x2, l, vec(pre_mix_g), wp["w_sb"], wp["w_rw"], wp["w_gl"])
        o_sb = _sb_attention(sb, l, vec(sb_norm_g), batch, seq)
        o_rw, o_gl = _recurrent_mixers(rw, gl, l, rw_prm, gl_prm, batch, seq)
        x2 = _mlp(x2, o_sb, o_rw, o_gl, l, wp["wo"], vec(post_mix_g), vec(pre_ffn_g),
                  wp["w1"], wp["w2"], vec(post_ffn_g))
    return x2.reshape(batch, seq, d)
```

```python
import functools

import jax
import jax.numpy as jnp
from jax import lax
from jax.experimental import pallas as pl
from jax.experimental.pallas import tpu as pltpu

F32 = jnp.float32
BF16 = jnp.bfloat16

D_MODEL = 1024
HEAD_DIM = 64
LANES = 128
SB_WIDTH = 384
RW_WIDTH = 384
RW_IN_WIDTH = 1280
RW_CODE = 128
GLA_KEY_WIDTH = 128
GLA_VAL_WIDTH = 256
GLA_DK = 32
GLA_GATE_RANK = 16
GLA_IN_WIDTH = 784
GLA_PAD_WIDTH = 896
GLA_GATE_NORMALIZER = 16.0
D_FF = 4096
EPS = 1e-6
RW_GN_EPS = 64e-5
CHUNK = 64
VMEM_LIMIT = 58 * 1024 * 1024

SB_BLOCK = 256
LOG2E = 1.4426950408889634
SB_DEAD = 110.0 * LOG2E
SB_CLAMP = 120.0
SB_STEP_BLOCKS = 2
SB_LAG = 4


def _cparams(sem):
    return pltpu.CompilerParams(dimension_semantics=sem, vmem_limit_bytes=VMEM_LIMIT)


def _dot(a, b):
    return jnp.dot(a, b, preferred_element_type=F32)


def _dot_nt(a, b):
    return lax.dot_general(a, b, (((1,), (1,)), ((), ())), preferred_element_type=F32)


def _dot_tn(a, b):
    return lax.dot_general(a, b, (((0,), (0,)), ((), ())), preferred_element_type=F32)


def _split2(x):
    hi = x.astype(BF16)
    lo = (x - hi.astype(F32)).astype(BF16)
    return hi, lo


def _split3(x):
    hi = x.astype(BF16)
    r1 = x - hi.astype(F32)
    mid = r1.astype(BF16)
    lo = (r1 - mid.astype(F32)).astype(BF16)
    return hi, mid, lo


def _dot_x3(a, w_hi, w_lo):
    a_hi, a_lo = _split2(a)
    return _dot(a_hi, w_hi) + (_dot(a_lo, w_hi) + _dot(a_hi, w_lo))


def _dot_exact_lhs(m_bf16, x):
    hi, mid, lo = _split3(x)
    return _dot(m_bf16, hi) + (_dot(m_bf16, mid) + _dot(m_bf16, lo))


def _rms(x, gain):
    return x * lax.rsqrt(jnp.mean(x * x, axis=-1, keepdims=True) + EPS) * gain


def _pair_head_sum(x, low_half):
    s_all = jnp.sum(x, axis=-1, keepdims=True)
    s_low = jnp.sum(jnp.where(low_half, x, 0.0), axis=-1, keepdims=True)
    return jnp.where(low_half, s_low, s_all - s_low)


def _inproj_kernel(x_ref, g_ref, wsb_ref, wrw_ref, wgl_ref, sb_ref, rw_ref, gl_ref):
    h = _rms(x_ref[...], g_ref[...]).astype(BF16)
    col = lax.broadcasted_iota(jnp.int32, (1, 3 * SB_WIDTH), 1)
    qscale = jnp.where(col < SB_WIDTH, LOG2E * HEAD_DIM ** -0.5, 1.0)
    sb_ref[...] = (_dot(h, wsb_ref[...]) * qscale).astype(BF16)
    rw_ref[...] = _dot(h, wrw_ref[...])
    gl_ref[...] = _dot(h, wgl_ref[...])


INPROJ_ROWS = 1024


def _layer_block(arr, layer, grid_rank, single_buffer=False):
    index = lambda *_: (layer, 0, 0)
    assert arr.ndim == 3 and grid_rank in (1, 2)
    if single_buffer:
        return pl.BlockSpec((None,) + arr.shape[1:], index, pipeline_mode=pl.Buffered(1))
    return pl.BlockSpec((None,) + arr.shape[1:], index)


def _inproj(x2, layer, gain, w_sb, w_rw, w_gl, tm=INPROJ_ROWS):
    n = x2.shape[0]
    full = lambda w: _layer_block(w, layer, 1, single_buffer=True)
    row = lambda width: pl.BlockSpec((tm, width), lambda i: (i, 0))
    return pl.pallas_call(
        _inproj_kernel,
        grid=(n // tm,),
        in_specs=[row(D_MODEL), full(gain), full(w_sb), full(w_rw), full(w_gl)],
        out_specs=[row(3 * SB_WIDTH), row(RW_IN_WIDTH), row(GLA_PAD_WIDTH)],
        out_shape=[jax.ShapeDtypeStruct((n, 3 * SB_WIDTH), BF16),
                   jax.ShapeDtypeStruct((n, RW_IN_WIDTH), F32),
                   jax.ShapeDtypeStruct((n, GLA_PAD_WIDTH), F32)],
        compiler_params=_cparams(("parallel",)),
        name="in_proj",
    )(x2, gain, w_sb, w_rw, w_gl)


def _sb_kernel(q_ref, k_ref, v_ref, g_ref, o_ref):
    qb = SB_BLOCK
    qis = [pl.program_id(1) * SB_STEP_BLOCKS + blk for blk in range(SB_STEP_BLOCKS)]
    lane = lax.broadcasted_iota(jnp.int32, (1, LANES), 1)
    low_half = lane < HEAD_DIM
    row = lax.broadcasted_iota(jnp.int32, (qb, qb), 0)
    col = lax.broadcasted_iota(jnp.int32, (qb, qb), 1)
    causal = col < row
    suffix = (row > col).astype(BF16)
    half = qb // 2
    pairs = SB_WIDTH // LANES
    zero = jnp.zeros((), BF16)
    vmasks = (low_half, jnp.logical_not(low_half))
    chains = [(h, r) for r in range(2) for h in range(2)]
    units = [(blk, pr) for blk in range(SB_STEP_BLOCKS) for pr in range(pairs)]
    lanes_of = [slice(pr * LANES, (pr + 1) * LANES) for _, pr in units]
    qi_of = [qis[blk] for blk, _ in units]
    qcs = []
    for (blk, _), sl in zip(units, lanes_of):
        q = q_ref[blk * qb:(blk + 1) * qb, sl]
        qhs = [jnp.where(m, q, zero) for m in vmasks]
        qcs.append([qhs[h][r * half:(r + 1) * half, :] for h, r in chains])

    def front(qc, kb, n, mask):
        return front_tail(_dot_nt(qc, kb[0:n, :]), n, mask)

    def front_tail(z, n, mask):
        p = jnp.maximum(z, jnp.log2(1.0 + jnp.exp2(jnp.minimum(z, SB_CLAMP))))
        log_beta = z - p
        if mask is not None:
            p = jnp.where(mask, p, 0.0)
        later = _dot(p.astype(BF16), suffix[0:n, 0:n])
        return log_beta, p[:, 0:1], later

    def back(log_beta, p_first, later, spent, mask):
        w = jnp.exp2(log_beta - (later + spent))
        if mask is not None:
            w = jnp.where(mask, w, 0.0)
        return w.astype(BF16), spent + (later[:, 0:1] + p_first)

    def values(vb, n):
        return [jnp.where(m, vb[0:n, :], zero) for m in vmasks]

    d_starts = [pl.multiple_of(qi * qb, qb) for qi in qi_of]
    p_starts = [pl.multiple_of(jnp.maximum(qi - 1, 0) * qb, qb) for qi in qi_of]
    d_keys = [half if r == 0 else qb for _, r in chains]
    d_masks = [causal[0:half, 0:half] if r == 0 else causal[half:, :] for _, r in chains]
    seq_chains = []
    for pr in range(len(units)):
        for prev in (False, True):
            for ci, (h, r) in enumerate(chains):
                seq_chains.append((pr, prev, ci, h, r))
    n_seq = len(seq_chains)
    p_vmasks = [[jnp.logical_and(m, qi > 0) for m in vmasks] for qi in qi_of]

    logits, fronts, spent_of = [None] * n_seq, [None] * n_seq, {}
    acc_of = {}
    def span(prev, r):
        if prev:
            return (0, qb) if r == 0 else (half, half)
        return (0, half) if r == 0 else (0, qb)

    for t in range(n_seq + 2 * SB_LAG):
        c2, c1 = t - 2 * SB_LAG, t - SB_LAG
        if 0 <= c2 < n_seq:
            pr, prev, ci, h, r = seq_chains[c2]
            k0, n = span(prev, r)
            mask = None if prev else d_masks[ci]
            spent = spent_of[(pr, ci)] if prev else jnp.zeros((half, 1), F32)
            log_beta, p_first, later = fronts[c2]
            fronts[c2] = None
            w, spent_of[(pr, ci)] = back(log_beta, p_first, later, spent, mask)
            start = p_starts[pr] if prev else d_starts[pr]
            vb = v_ref[pl.ds(start, qb), lanes_of[pr]]
            vh = jnp.where(p_vmasks[pr][h] if prev else vmasks[h], vb[k0:k0 + n, :], zero)
            part = _dot(w, vh)
            acc_of[(pr, r)] = part if (pr, r) not in acc_of else acc_of[(pr, r)] + part
        if 0 <= c1 < n_seq:
            pr, prev, ci, h, r = seq_chains[c1]
            mask = None if prev else d_masks[ci]
            fronts[c1] = front_tail(logits[c1], span(prev, r)[1], mask)
            logits[c1] = None
        if t < n_seq:
            pr, prev, ci, h, r = seq_chains[t]
            k0, n = span(prev, r)
            start = p_starts[pr] if prev else d_starts[pr]
            kb = k_ref[pl.ds(start, qb), lanes_of[pr]]
            logits[t] = _dot_nt(qcs[pr][ci], kb[k0:k0 + n, :])
    accs_all = [[acc_of[(pr, r)] for r in range(2)] for pr in range(len(units))]
    bp_spent = [[spent_of[(pr, ci)] for ci in range(len(chains))] for pr in range(len(units))]

    late = [ci for ci, (_, r) in enumerate(chains) if r == 1]

    def least(spents):
        m = spents[0]
        for sp in spents[1:]:
            m = jnp.minimum(m, sp)
        return jnp.min(m)

    far_needed = [jnp.logical_and(qi > 0, least([bp_spent[pr][ci] for ci in late]) < SB_DEAD)
                  for pr, qi in enumerate(qi_of)]
    more_needed = [jnp.logical_and(qi > 1, least(bp_spent[pr]) < SB_DEAD) for pr, qi in enumerate(qi_of)]

    def normalised(accs):
        acc = jnp.concatenate(accs, axis=0)
        ms = _pair_head_sum(acc * acc, low_half) * (1.0 / HEAD_DIM)
        return acc * lax.rsqrt(ms + EPS)

    eager = [normalised(accs) for accs in accs_all]

    finals = []
    for pr, sl in enumerate(lanes_of):
        qi, p_start = qi_of[pr], p_starts[pr]

        def far_half(args, pr=pr, sl=sl, p_start=p_start):
            acc, spents = args
            kb, vb = k_ref[pl.ds(p_start, half), sl], v_ref[pl.ds(p_start, half), sl]
            fs = [front(qcs[pr][ci], kb, half, None) for ci in late]
            bs = [back(*f, sp, None) for f, sp in zip(fs, spents)]
            vh = jnp.concatenate(values(vb, half), axis=0)
            return acc + _dot(jnp.concatenate([b[0] for b in bs], axis=1), vh), [b[1] for b in bs]

        def cond(s):
            j, spents, _ = s
            return jnp.logical_and(j >= 0, least(spents) < SB_DEAD)

        def body(s, pr=pr, sl=sl):
            j, spents, accs = s
            start = pl.multiple_of(j * qb, qb)
            kb, vb = k_ref[pl.ds(start, qb), sl], v_ref[pl.ds(start, qb), sl]
            fs = [front(qc, kb, qb, None) for qc in qcs[pr]]
            bs = [back(*f, sp, None) for f, sp in zip(fs, spents)]
            vh = jnp.concatenate(values(vb, qb), axis=0)
            accs = [acc + _dot(jnp.concatenate([bs[2 * r][0], bs[2 * r + 1][0]], axis=1), vh)
                    for r, acc in enumerate(accs)]
            return j - 1, [b[1] for b in bs], accs

        def sweep_on(args, pr=pr, qi=qi, far_half=far_half, cond=cond, body=body):
            accs, spents, _ = args
            late_spent = [spents[ci] for ci in late]
            acc_late, late_spent = lax.cond(far_needed[pr], far_half, lambda a: a, (accs[1], late_spent))
            spents = list(spents)
            for ci, sp in zip(late, late_spent):
                spents[ci] = sp
            _, _, accs = lax.while_loop(cond, body, (qi - 2, spents, [accs[0], acc_late]))
            return normalised(accs)

        finals.append(lax.cond(jnp.logical_or(far_needed[pr], more_needed[pr]), sweep_on,
                               lambda args: args[2], (accs_all[pr], bp_spent[pr], eager[pr])))

    for blk in range(SB_STEP_BLOCKS):
        outs = [finals[pr] for pr, (b, _) in enumerate(units) if b == blk]
        o_ref[blk * qb:(blk + 1) * qb, :] = (jnp.concatenate(outs, axis=1) * g_ref[...]).astype(o_ref.dtype)


def _sb_attention(sb, layer, gain, batch, seq):
    rows = SB_BLOCK * SB_STEP_BLOCKS
    nq = seq // rows
    n = sb.shape[0]
    return pl.pallas_call(
        _sb_kernel,
        grid=(batch, nq),
        in_specs=[
            pl.BlockSpec((rows, SB_WIDTH), lambda b, i: (b * nq + i, 0)),
            pl.BlockSpec((seq, SB_WIDTH), lambda b, i: (b, 1)),
            pl.BlockSpec((seq, SB_WIDTH), lambda b, i: (b, 2)),
            _layer_block(gain, layer, 2),
        ],
        out_specs=pl.BlockSpec((rows, SB_WIDTH), lambda b, i: (b * nq + i, 0)),
        out_shape=jax.ShapeDtypeStruct((n, SB_WIDTH), BF16),
        compiler_params=_cparams(("parallel", "arbitrary")),
        name="sb_attention",
    )(sb, sb, sb, gain)


RW_BLOCK = 512
RW_PART = 128


def _rw_body(x_ref, mu_ref, w0_ref, a0_ref, kk_ref, ka_ref, rk_ref, gng_ref, gnb_ref,
             uphi_ref, uplo_ref, o_ref,
             state_s, prev_s, r_s, k_s, v_s, lw_s, a_s, b_s, y_s, lc_s,
             rp_s, yp_s, p_s, q_s, ge_s, bonus_s, gate_s, filler=None):
    tb = RW_BLOCK
    hb = RW_PART
    c = CHUNK
    pairs = RW_WIDTH // LANES

    lane = lax.broadcasted_iota(jnp.int32, (1, LANES), 1)
    low_half = lane < HEAD_DIM
    tr = lax.broadcasted_iota(jnp.int32, (hb, hb), 0)
    tc = lax.broadcasted_iota(jnp.int32, (hb, hb), 1)
    tri = jnp.logical_and(tr // c == tc // c, tc <= tr).astype(BF16)
    first = lax.broadcasted_iota(jnp.int32, (hb, 1), 0) == 0

    def prologue(part):
        rows = slice(part * hb, (part + 1) * hb)
        x = x_ref[rows, :]
        before = prev_s[...] if part == 0 else x_ref[part * hb - 1:part * hb, :]
        prev = jnp.where(first, before, pltpu.roll(x, 1, 0))
        xs = x + (prev - x) * mu_ref[...]
        yield
        r = xs[:, 0:RW_WIDTH]
        k = xs[:, RW_WIDTH:2 * RW_WIDTH]
        v = xs[:, 2 * RW_WIDTH:3 * RW_WIDTH]
        code = xs[:, 3 * RW_WIDTH:]
        act = jnp.where(lane < 32, jnp.tanh(code), jnp.where(lane < 64, code, jax.nn.sigmoid(code)))
        up = jnp.concatenate(
            [_dot_x3(act, uphi_ref[:, 0:RW_WIDTH], uplo_ref[:, 0:RW_WIDTH]),
             _dot(act.astype(BF16), uphi_ref[:, RW_WIDTH:])], axis=1)
        r_s[rows, :] = r
        v_s[rows, :] = v
        gate_s[rows, :] = up[:, 2 * RW_WIDTH:]
        yield
        wpre = w0_ref[...] + up[:, 0:RW_WIDTH]
        log_w = -(jnp.maximum(-wpre, 0.0) + jnp.log(1.0 + jnp.exp(-jnp.abs(wpre)))) - 0.5
        lw = -jnp.exp(log_w)
        lw_s[rows, :] = lw
        yield
        lc_s[rows, :] = _dot_exact_lhs(tri, lw)
        yield
        a = jax.nn.sigmoid(a0_ref[...] + up[:, RW_WIDTH:2 * RW_WIDTH])
        kmod = k * (1.0 + (a - 1.0) * ka_ref[...])
        k_s[rows, :] = kmod
        yield
        kk = k * kk_ref[...]
        bonus_in = r * kmod * rk_ref[...]
        for p in range(pairs):
            sl = slice(p * LANES, (p + 1) * LANES)
            ss = _pair_head_sum(kk[:, sl] * kk[:, sl], low_half)
            kkn = kk[:, sl] * lax.rsqrt(jnp.maximum(ss, 1e-12))
            a_s[rows, sl] = -kkn
            b_s[rows, sl] = kkn * a[:, sl]
            bonus_s[rows, sl] = _pair_head_sum(bonus_in[:, sl], low_half) * v[:, sl]
            yield

    rr = lax.broadcasted_iota(jnp.int32, (2 * c, 2 * c), 0)
    cc = lax.broadcasted_iota(jnp.int32, (2 * c, 2 * c), 1)
    same_head = (rr // c) == (cc // c)
    strict = jnp.logical_and(same_head, cc < rr)
    incl = jnp.logical_and(same_head, cc <= rr)
    eye = (rr == cc).astype(F32)
    m_lo = low_half
    m_hi = jnp.logical_not(low_half)

    def stack(xp):
        return jnp.concatenate([jnp.where(m_lo, xp, 0.0), jnp.where(m_hi, xp, 0.0)], axis=0)

    def prepare(chunks, tick):
        chains = [(ci, p) for ci in chunks for p in range(pairs)]
        pre = []
        for ci, p in chains:
            rows = slice(ci * c, (ci + 1) * c)
            sl = slice(p * LANES, (p + 1) * LANES)
            lwc = lw_s[rows, sl]
            lc = lc_s[rows, sl]
            e_incl = jnp.exp(lc)
            e_excl = jnp.exp(lc - lwc)
            e_inv = jnp.exp(-lc)
            g_end = e_incl[c - 1:c, :]
            ag = stack(a_s[rows, sl] * e_excl).astype(BF16)
            rg = stack(r_s[rows, sl] * e_incl).astype(BF16)
            bd = b_s[rows, sl] * e_inv
            kd = k_s[rows, sl] * e_inv
            vst = stack(v_s[rows, sl]).astype(BF16)
            lhs = jnp.concatenate([ag, rg], axis=0)
            rhs = jnp.concatenate([bd, bd, kd, kd], axis=0).astype(BF16)
            bde = stack(bd * g_end).astype(BF16)
            kde = stack(kd * g_end).astype(BF16)
            idx = ci * pairs + p
            ge_s[idx] = jnp.broadcast_to(g_end, (LANES, LANES)).T
            pre.append((idx, ag, rg, vst, lhs, rhs, kde, bde))
        tick()
        gs = [_dot_nt(t[4], t[5]) for t in pre]
        tick()
        npows, xs_, avs, arbs = [], [], [], []
        for t, g in zip(pre, gs):
            a_ab = jnp.where(strict, g[0:2 * c, 0:2 * c], 0.0)
            a_ak = jnp.where(strict, g[0:2 * c, 2 * c:], 0.0)
            a_rb = jnp.where(incl, g[2 * c:, 0:2 * c], 0.0)
            a_rk = jnp.where(incl, g[2 * c:, 2 * c:], 0.0)
            arbs.append(a_rb.astype(BF16))
            npows.append(a_ab)
            avs.append(jnp.concatenate([a_ak, a_rk], axis=0).astype(BF16))
        avs = [_dot(av, t[3]) for av, t in zip(avs, pre)]
        svs = [_dot_tn(t[6], t[3]) for t in pre]
        tick()
        for t, av in zip(pre, avs):
            xs_.append(jnp.concatenate([t[1], av[0:2 * c, :].astype(BF16)], axis=1))
        tinvs = [eye + n for n in npows]
        nbs = [n.astype(BF16) for n in npows]
        npows = [_dot(nb, nb) for nb in nbs]
        tick()
        for it in range(1, 6):
            nbs = [n.astype(BF16) for n in npows]
            tbs = [tv.astype(BF16) for tv in tinvs]
            if it < 5:
                prods = [_dot(nb, jnp.concatenate([nb, tb], axis=1)) for nb, tb in zip(nbs, tbs)]
                npows = [pr[:, 0:LANES] for pr in prods]
                tinvs = [tv + pr[:, LANES:] for tv, pr in zip(tinvs, prods)]
            else:
                tinvs = [tv + _dot(nb, tb) for tv, nb, tb in zip(tinvs, nbs, tbs)]
            tick()
        xbs = [_dot(tv.astype(BF16), x0).astype(BF16) for tv, x0 in zip(tinvs, xs_)]
        ras = [_dot(arb, xb) for arb, xb in zip(arbs, xbs)]
        pqs = [_dot_tn(t[7], xb) for xb, t in zip(xbs, pre)]
        for t, av, sv, ra, pq in zip(pre, avs, svs, ras, pqs):
            idx = t[0]
            rp_s[idx] = (t[2].astype(F32) + ra[:, 0:LANES]).astype(BF16)
            yp_s[idx] = ra[:, LANES:] + av[2 * c:, :]
            p_s[idx] = pq[:, 0:LANES].astype(BF16)
            q_s[idx] = pq[:, LANES:] + sv
        tick()

    half_chunks = hb // c
    states = [state_s[p] for p in range(pairs)]

    def finish(part):
        for ci in range(part * half_chunks, (part + 1) * half_chunks):
            for p in range(pairs):
                idx = ci * pairs + p
                sb = states[p].astype(BF16)
                y = _dot(rp_s[idx], sb) + yp_s[idx]
                y_s[ci * c:(ci + 1) * c, p * LANES:(p + 1) * LANES] = y[0:c, :] + y[c:2 * c, :]
                states[p] = states[p] * ge_s[idx] + _dot(p_s[idx], sb) + q_s[idx]
            yield
        rows = slice(part * hb, (part + 1) * hb)
        for p in range(pairs):
            sl = slice(p * LANES, (p + 1) * LANES)
            y = y_s[rows, sl]
            mean = _pair_head_sum(y, low_half) * (1.0 / HEAD_DIM)
            d = y - mean
            var = _pair_head_sum(d * d, low_half) * (1.0 / HEAD_DIM)
            yn = d * lax.rsqrt(var + RW_GN_EPS) * gng_ref[:, sl] + gnb_ref[:, sl]
            o_ref[rows, sl] = ((yn + bonus_s[rows, sl]) * gate_s[rows, sl]).astype(o_ref.dtype)
            yield

    def advance(*gens):
        def tick():
            for gen in gens:
                if gen is not None:
                    next(gen, None)
        return tick

    def drain(gen):
        for _ in gen:
            pass

    n_parts = tb // hb
    drain(prologue(0))
    for part in range(n_parts):
        ahead = prologue(part + 1) if part + 1 < n_parts else None
        behind = finish(part - 1) if part >= 1 else None
        prepare(range(part * half_chunks, (part + 1) * half_chunks), advance(ahead, behind, filler))
        for gen in (ahead, behind):
            if gen is not None:
                drain(gen)
    prev_s[...] = x_ref[tb - 1:tb, :]
    drain(finish(n_parts - 1))
    for p in range(pairs):
        state_s[p] = states[p]


RW_N_IN, RW_N_SCRATCH = 11, 17
GLA_N_IN, GLA_N_SCRATCH = 5, 5


def _recurrent_kernel(*refs):
    n_in = RW_N_IN + GLA_N_IN
    rw_in, gl_in = refs[:RW_N_IN], refs[RW_N_IN:n_in]
    o_rw, o_gl = refs[n_in], refs[n_in + 1]
    rw_scr = refs[n_in + 2:n_in + 2 + RW_N_SCRATCH]
    gl_scr = refs[n_in + 2 + RW_N_SCRATCH:]

    @pl.when(pl.program_id(1) == 0)
    def _():
        for state in (rw_scr[0], rw_scr[1], gl_scr[0]):
            state[...] = jnp.zeros_like(state)

    gla = _gla_body(*gl_in, o_gl, *gl_scr)
    _rw_body(*rw_in, o_rw, *rw_scr, filler=gla)
    for _ in gla:
        pass


def _recurrent_mixers(rw, gl, layer, rprm, gprm, batch, seq):
    n = rw.shape[0]
    tb = RW_BLOCK
    nb = seq // tb
    rows = lambda width: pl.BlockSpec((tb, width), lambda b, i: (b * nb + i, 0))
    rw_names = ("mu", "w0", "a0", "k_k", "k_a", "r_k", "gn_g", "gn_b", "up_hi", "up_lo")
    gl_names = ("gate_b", "norm_g", "up_hi", "up_lo")
    rw_params = [rprm[k] for k in rw_names]
    gl_params = [gprm[k] for k in gl_names]
    assert 1 + len(rw_params) == RW_N_IN and 1 + len(gl_params) == GLA_N_IN
    blk = lambda: pltpu.VMEM((tb, RW_WIDTH), F32)
    n_chain = (tb // CHUNK) * (RW_WIDTH // LANES)
    tile = lambda r, dt: pltpu.VMEM((n_chain, r, LANES), dt)
    rw_scratch = [pltpu.VMEM((RW_WIDTH // LANES, LANES, LANES), F32),
                  pltpu.VMEM((1, RW_IN_WIDTH), F32),
                  blk(), blk(), blk(), blk(), blk(), blk(), blk(), blk(),
                  tile(LANES, BF16), tile(LANES, F32), tile(LANES, BF16), tile(LANES, F32),
                  tile(LANES, F32), blk(), blk()]
    gl_scratch = [pltpu.VMEM((GLA_VAL_WIDTH, GLA_KEY_WIDTH), F32),
                  pltpu.VMEM((tb, GLA_KEY_WIDTH), F32),
                  pltpu.VMEM((tb, GLA_KEY_WIDTH), F32),
                  pltpu.VMEM((tb, GLA_KEY_WIDTH), F32),
                  pltpu.VMEM((tb, GLA_VAL_WIDTH), F32)]
    assert len(rw_scratch) == RW_N_SCRATCH and len(gl_scratch) == GLA_N_SCRATCH
    return pl.pallas_call(
        _recurrent_kernel,
        grid=(batch, nb),
        in_specs=([rows(RW_IN_WIDTH)] + [_layer_block(a, layer, 2) for a in rw_params]
                  + [rows(GLA_PAD_WIDTH)] + [_layer_block(a, layer, 2) for a in gl_params]),
        out_specs=[rows(RW_WIDTH), rows(GLA_VAL_WIDTH)],
        out_shape=[jax.ShapeDtypeStruct((n, RW_WIDTH), BF16),
                   jax.ShapeDtypeStruct((n, GLA_VAL_WIDTH), BF16)],
        scratch_shapes=rw_scratch + gl_scratch,
        compiler_params=_cparams(("parallel", "arbitrary")),
        name="rwkv7_gla",
    )(rw, *rw_params, gl, *gl_params)


GLA_BLOCK = RW_BLOCK
GLA_SUB = 16
GLA_EXP_CAP = 80.0


def _gla_body(x_ref, gb_ref, ng_ref, uphi_ref, uplo_ref, o_ref, state_s, q_s, k_s, la_s, o_s):
    tb = GLA_BLOCK
    c = CHUNK
    heads = GLA_VAL_WIDTH // HEAD_DIM

    kw, vw = GLA_KEY_WIDTH, GLA_VAL_WIDTH
    code = x_ref[:, 2 * kw + 2 * vw:]
    pre = _dot_x3(code, uphi_ref[...], uplo_ref[...]) + gb_ref[...]
    la_s[...] = -(jnp.maximum(-pre, 0.0) + jnp.log(1.0 + jnp.exp(-jnp.abs(pre)))) * (1.0 / GLA_GATE_NORMALIZER)
    q_s[...] = x_ref[:, 0:kw] * (GLA_DK ** -0.5)
    k_s[...] = x_ref[:, kw:2 * kw]
    yield

    tr = lax.broadcasted_iota(jnp.int32, (c, c), 0)
    tc = lax.broadcasted_iota(jnp.int32, (c, c), 1)
    tri = (tc <= tr).astype(BF16)
    sub = GLA_SUB
    n_sub = c // sub
    srow = lax.broadcasted_iota(jnp.int32, (heads * c, c), 0)
    scol = lax.broadcasted_iota(jnp.int32, (heads * c, c), 1)
    causal = scol <= (srow // (heads * sub)) * sub + srow % sub
    klane = lax.broadcasted_iota(jnp.int32, (1, kw), 1) // GLA_DK
    vlane = lax.broadcasted_iota(jnp.int32, (1, vw), 1) // HEAD_DIM
    st_r = lax.broadcasted_iota(jnp.int32, (vw, kw), 0) // HEAD_DIM
    st_c = lax.broadcasted_iota(jnp.int32, (vw, kw), 1) // GLA_DK
    st_mask = st_r == st_c

    n_chunk = tb // c
    pre = []
    for ci in range(n_chunk):
        rows = slice(ci * c, (ci + 1) * c)
        bc = _dot_exact_lhs(tri, la_s[rows, :])
        e_pos = jnp.exp(bc)
        qc = q_s[rows, :]
        kc = k_s[rows, :]
        k2 = (kc * jnp.exp(bc[c - 1:c, :] - bc)).astype(BF16)
        vb = x_ref[rows, 2 * kw:2 * kw + vw].astype(BF16)
        qsts, kes = [], []
        for i in range(n_sub):
            srows = slice(i * sub, (i + 1) * sub)
            ref = bc[i * sub - 1:i * sub, :] if i > 0 else jnp.zeros((1, kw), F32)
            qi = qc[srows, :] * jnp.exp(bc[srows, :] - ref)
            qsts.append(jnp.concatenate([jnp.where(klane == h, qi, 0.0) for h in range(heads)],
                                        axis=0).astype(BF16))
            kes.append((kc * jnp.exp(jnp.minimum(ref - bc, GLA_EXP_CAP))).astype(BF16))
        pre.append(((qc * e_pos).astype(BF16), k2, vb, qsts, kes, e_pos[c - 1:c, :]))
        yield
    scores = []
    for t in pre:
        scores.append(jnp.where(
            causal, jnp.concatenate([_dot_nt(qs, ke) for qs, ke in zip(t[3], t[4])], axis=0),
            0.0).astype(BF16))
        yield
    o_sts = [_dot(sc, t[2]) for sc, t in zip(scores, pre)]
    incs = [jnp.where(st_mask, _dot_tn(t[2], t[1]), 0.0) for t in pre]
    yield
    state = state_s[...]
    for ci in range(n_chunk):
        intra = []
        for i in range(n_sub):
            acc = None
            for h in range(heads):
                r0 = (i * heads + h) * sub
                part = jnp.where(vlane == h, o_sts[ci][r0:r0 + sub, :], 0.0)
                acc = part if acc is None else acc + part
            intra.append(acc)
        o_s[ci * c:(ci + 1) * c, :] = _dot_nt(pre[ci][0], state.astype(BF16)) + jnp.concatenate(intra, axis=0)
        state = state * pre[ci][5] + incs[ci]
        yield
    state_s[...] = state

    lane = lax.broadcasted_iota(jnp.int32, (1, LANES), 1)
    low_half = lane < HEAD_DIM
    outs = []
    for p in range(vw // LANES):
        o = o_s[:, p * LANES:(p + 1) * LANES]
        ms = _pair_head_sum(o * o, low_half) * (1.0 / HEAD_DIM)
        outs.append(o * lax.rsqrt(ms + EPS))
    g = x_ref[:, 2 * kw + vw:2 * kw + 2 * vw]
    on = jnp.concatenate(outs, axis=1) * ng_ref[...]
    o_ref[...] = (on * (g * jax.nn.sigmoid(g))).astype(o_ref.dtype)


MLP_ROWS = 1024
MLP_FF_TILE = 1024


def _mlp_kernel(x_ref, sb_ref, rw_ref, gl_ref, wsb_ref, wrw_ref, wgl_ref, gmix_ref,
                gpre_ref, w1_ref, w2_ref, gpost_ref, o_ref):
    mixed = _dot(sb_ref[...], wsb_ref[...]) + _dot(rw_ref[...], wrw_ref[...]) + _dot(gl_ref[...], wgl_ref[...])
    x1 = x_ref[...] + _rms(mixed, gmix_ref[...])
    h = _rms(x1, gpre_ref[...]).astype(BF16)
    ff = None
    for j in range(D_FF // MLP_FF_TILE):
        cols = slice(j * MLP_FF_TILE, (j + 1) * MLP_FF_TILE)
        a = jnp.maximum(_dot(h, w1_ref[:, cols]), 0.0)
        part = _dot((a * a).astype(BF16), w2_ref[cols, :])
        ff = part if ff is None else ff + part
    o_ref[...] = x1 + _rms(ff, gpost_ref[...])


def _mlp(x2, o_sb, o_rw, o_gl, layer, w_out, g_mix, g_pre, w1, w2, g_post):
    n = x2.shape[0]
    tm = MLP_ROWS
    full = lambda w: _layer_block(w, layer, 1, single_buffer=True)
    row = lambda width: pl.BlockSpec((tm, width), lambda i: (i, 0))

    def out_rows(width, first):
        assert first % width == 0
        return pl.BlockSpec((None, width, D_MODEL), lambda i: (layer, first // width, 0),
                            pipeline_mode=pl.Buffered(1))

    return pl.pallas_call(
        _mlp_kernel,
        grid=(n // tm,),
        in_specs=[row(D_MODEL), row(SB_WIDTH), row(RW_WIDTH), row(GLA_VAL_WIDTH),
                  out_rows(SB_WIDTH, 0), out_rows(RW_WIDTH, SB_WIDTH),
                  out_rows(GLA_VAL_WIDTH, SB_WIDTH + RW_WIDTH), full(g_mix),
                  full(g_pre), full(w1), full(w2), full(g_post)],
        out_specs=row(D_MODEL),
        out_shape=jax.ShapeDtypeStruct((n, D_MODEL), F32),
        compiler_params=_cparams(("parallel",)),
        name="out_proj_mlp",
    )(x2, o_sb, o_rw, o_gl, w_out, w_out, w_out, g_mix, g_pre, w1, w2, g_post)


def _hi_lo(w):
    hi = w.astype(BF16)
    return hi, (w - hi.astype(F32)).astype(BF16)


def _prep_weights(w_in, rw_w_up, rw_a_up, rw_g_up, gla_gate_up, w_out, w_ff1, w_ff2):
    depth = w_in.shape[0]
    sb_end = 3 * SB_WIDTH
    rw_end = sb_end + RW_IN_WIDTH
    w_sb = w_in[:, :, :sb_end].astype(BF16)
    w_rw = w_in[:, :, sb_end:rw_end].astype(BF16)
    g0 = rw_end
    kw, vw, gr = GLA_KEY_WIDTH, GLA_VAL_WIDTH, GLA_GATE_RANK
    w_gl = jnp.concatenate([
        w_in[:, :, g0:g0 + 2 * kw + vw],
        w_in[:, :, g0 + 2 * kw + vw + gr:g0 + GLA_IN_WIDTH],
        w_in[:, :, g0 + 2 * kw + vw:g0 + 2 * kw + vw + gr],
        jnp.zeros((depth, D_MODEL, GLA_PAD_WIDTH - GLA_IN_WIDTH), F32)], axis=2).astype(BF16)
    up = jnp.zeros((depth, RW_CODE, 3 * RW_WIDTH), F32)
    up = up.at[:, 0:32, 0:RW_WIDTH].set(rw_w_up)
    up = up.at[:, 32:64, RW_WIDTH:2 * RW_WIDTH].set(rw_a_up)
    up = up.at[:, 64:128, 2 * RW_WIDTH:].set(rw_g_up)
    rw_up_hi, rw_up_lo = _hi_lo(up)
    gup = jnp.zeros((depth, LANES, GLA_KEY_WIDTH), F32).at[:, 0:gr, :].set(gla_gate_up)
    gl_up_hi, gl_up_lo = _hi_lo(gup)
    return dict(w_sb=w_sb, w_rw=w_rw, w_gl=w_gl, rw_up_hi=rw_up_hi, rw_up_lo=rw_up_lo,
                gl_up_hi=gl_up_hi, gl_up_lo=gl_up_lo, wo=w_out.astype(BF16),
                w1=w_ff1.astype(BF16), w2=w_ff2.astype(BF16))


def kernel(x, pre_mix_g, w_in, sb_norm_g, rw_mu, rw_w0, rw_w_up, rw_a0, rw_a_up, rw_g_up, rw_k_k, rw_k_a, rw_r_k, rw_gn_g, rw_gn_b, gla_gate_up, gla_gate_b, gla_norm_g, w_out, post_mix_g, pre_ffn_g, w_ff1, w_ff2, post_ffn_g):
    batch, seq, d = x.shape
    depth = w_in.shape[0]
    x2 = x.reshape(batch * seq, d)
    vec = lambda t: t[:, None, :]
    wp = _prep_weights(w_in, rw_w_up, rw_a_up, rw_g_up, gla_gate_up, w_out, w_ff1, w_ff2)
    rw_prm = dict(mu=vec(rw_mu), w0=vec(rw_w0), a0=vec(rw_a0), k_k=vec(rw_k_k), k_a=vec(rw_k_a),
                  r_k=vec(rw_r_k), gn_g=vec(rw_gn_g), gn_b=vec(rw_gn_b),
                  up_hi=wp["rw_up_hi"], up_lo=wp["rw_up_lo"])
    gl_prm = dict(gate_b=vec(gla_gate_b), norm_g=vec(gla_norm_g),
                  up_hi=wp["gl_up_hi"], up_lo=wp["gl_up_lo"])
    for l in range(depth):
        sb, rw, gl = _inproj(x2, l, vec(pre_mix_g), wp["w_sb"], wp["w_rw"], wp["w_gl"])
        o_sb = _sb_attention(sb, l, vec(sb_norm_g), batch, seq)
        o_rw, o_gl = _recurrent_mixers(rw, gl, l, rw_prm, gl_prm, batch, seq)
        x2 = _mlp(x2, o_sb, o_rw, o_gl, l, wp["wo"], vec(post_mix_g), vec(pre_ffn_g),
                  wp["w1"], wp["w2"], vec(post_ffn_g))
    return x2.reshape(batch, seq, d)
```

```python
import functools

import jax
import jax.numpy as jnp
from jax import lax
from jax.experimental import pallas as pl
from jax.experimental.pallas import tpu as pltpu

F32 = jnp.float32
BF16 = jnp.bfloat16

D_MODEL = 1024
HEAD_DIM = 64
LANES = 128
SB_WIDTH = 384
RW_WIDTH = 384
RW_IN_WIDTH = 1280
RW_CODE = 128
GLA_KEY_WIDTH = 128
GLA_VAL_WIDTH = 256
GLA_DK = 32
GLA_GATE_RANK = 16
GLA_IN_WIDTH = 784
GLA_PAD_WIDTH = 896
GLA_GATE_NORMALIZER = 16.0
D_FF = 4096
EPS = 1e-6
RW_GN_EPS = 64e-5
CHUNK = 64
VMEM_LIMIT = 58 * 1024 * 1024

SB_BLOCK = 256
LOG2E = 1.4426950408889634
SB_DEAD = 110.0 * LOG2E
SB_CLAMP = 120.0
SB_STEP_BLOCKS = 4
SB_LAG = 4


def _cparams(sem):
    return pltpu.CompilerParams(dimension_semantics=sem, vmem_limit_bytes=VMEM_LIMIT)


def _dot(a, b):
    return jnp.dot(a, b, preferred_element_type=F32)


def _dot_nt(a, b):
    return lax.dot_general(a, b, (((1,), (1,)), ((), ())), preferred_element_type=F32)


def _dot_tn(a, b):
    return lax.dot_general(a, b, (((0,), (0,)), ((), ())), preferred_element_type=F32)


def _split2(x):
    hi = x.astype(BF16)
    lo = (x - hi.astype(F32)).astype(BF16)
    return hi, lo


def _split3(x):
    hi = x.astype(BF16)
    r1 = x - hi.astype(F32)
    mid = r1.astype(BF16)
    lo = (r1 - mid.astype(F32)).astype(BF16)
    return hi, mid, lo


def _dot_x3(a, w_hi, w_lo):
    a_hi, a_lo = _split2(a)
    return _dot(a_hi, w_hi) + (_dot(a_lo, w_hi) + _dot(a_hi, w_lo))


def _dot_exact_lhs(m_bf16, x):
    hi, mid, lo = _split3(x)
    return _dot(m_bf16, hi) + (_dot(m_bf16, mid) + _dot(m_bf16, lo))


def _rms(x, gain):
    return x * lax.rsqrt(jnp.mean(x * x, axis=-1, keepdims=True) + EPS) * gain


def _pair_head_sum(x, low_half):
    s_all = jnp.sum(x, axis=-1, keepdims=True)
    s_low = jnp.sum(jnp.where(low_half, x, 0.0), axis=-1, keepdims=True)
    return jnp.where(low_half, s_low, s_all - s_low)


def _inproj_kernel(x_ref, g_ref, wsb_ref, wrw_ref, wgl_ref, sb_ref, rw_ref, gl_ref):
    h = _rms(x_ref[...], g_ref[...]).astype(BF16)
    col = lax.broadcasted_iota(jnp.int32, (1, 3 * SB_WIDTH), 1)
    qscale = jnp.where(col < SB_WIDTH, LOG2E * HEAD_DIM ** -0.5, 1.0)
    sb_ref[...] = (_dot(h, wsb_ref[...]) * qscale).astype(BF16)
    rw_ref[...] = _dot(h, wrw_ref[...])
    gl_ref[...] = _dot(h, wgl_ref[...])


INPROJ_ROWS = 1024


def _layer_block(arr, layer, grid_rank, single_buffer=False):
    index = lambda *_: (layer, 0, 0)
    assert arr.ndim == 3 and grid_rank in (1, 2)
    if single_buffer:
        return pl.BlockSpec((None,) + arr.shape[1:], index, pipeline_mode=pl.Buffered(1))
    return pl.BlockSpec((None,) + arr.shape[1:], index)


def _inproj(x2, layer, gain, w_sb, w_rw, w_gl, tm=INPROJ_ROWS):
    n = x2.shape[0]
    full = lambda w: _layer_block(w, layer, 1, single_buffer=True)
    row = lambda width: pl.BlockSpec((tm, width), lambda i: (i, 0))
    return pl.pallas_call(
        _inproj_kernel,
        grid=(n // tm,),
        in_specs=[row(D_MODEL), full(gain), full(w_sb), full(w_rw), full(w_gl)],
        out_specs=[row(3 * SB_WIDTH), row(RW_IN_WIDTH), row(GLA_PAD_WIDTH)],
        out_shape=[jax.ShapeDtypeStruct((n, 3 * SB_WIDTH), BF16),
                   jax.ShapeDtypeStruct((n, RW_IN_WIDTH), F32),
                   jax.ShapeDtypeStruct((n, GLA_PAD_WIDTH), F32)],
        compiler_params=_cparams(("parallel",)),
        name="in_proj",
    )(x2, gain, w_sb, w_rw, w_gl)


def _sb_kernel(q_ref, k_ref, v_ref, g_ref, o_ref):
    qb = SB_BLOCK
    qis = [pl.program_id(1) * SB_STEP_BLOCKS + blk for blk in range(SB_STEP_BLOCKS)]
    lane = lax.broadcasted_iota(jnp.int32, (1, LANES), 1)
    low_half = lane < HEAD_DIM
    row = lax.broadcasted_iota(jnp.int32, (qb, qb), 0)
    col = lax.broadcasted_iota(jnp.int32, (qb, qb), 1)
    causal = col < row
    suffix = (row > col).astype(BF16)
    half = qb // 2
    pairs = SB_WIDTH // LANES
    zero = jnp.zeros((), BF16)
    vmasks = (low_half, jnp.logical_not(low_half))
    chains = [(h, r) for r in range(2) for h in range(2)]
    units = [(blk, pr) for blk in range(SB_STEP_BLOCKS) for pr in range(pairs)]
    lanes_of = [slice(pr * LANES, (pr + 1) * LANES) for _, pr in units]
    qi_of = [qis[blk] for blk, _ in units]
    qcs = []
    for (blk, _), sl in zip(units, lanes_of):
        q = q_ref[blk * qb:(blk + 1) * qb, sl]
        qhs = [jnp.where(m, q, zero) for m in vmasks]
        qcs.append([qhs[h][r * half:(r + 1) * half, :] for h, r in chains])

    def front(qc, kb, n, mask):
        return front_tail(_dot_nt(qc, kb[0:n, :]), n, mask)

    def front_tail(z, n, mask):
        p = jnp.maximum(z, jnp.log2(1.0 + jnp.exp2(jnp.minimum(z, SB_CLAMP))))
        log_beta = z - p
        if mask is not None:
            p = jnp.where(mask, p, 0.0)
        later = _dot(p.astype(BF16), suffix[0:n, 0:n])
        return log_beta, p[:, 0:1], later

    def back(log_beta, p_first, later, spent, mask):
        w = jnp.exp2(log_beta - (later + spent))
        if mask is not None:
            w = jnp.where(mask, w, 0.0)
        return w.astype(BF16), spent + (later[:, 0:1] + p_first)

    def values(vb, n):
        return [jnp.where(m, vb[0:n, :], zero) for m in vmasks]

    d_starts = [pl.multiple_of(qi * qb, qb) for qi in qi_of]
    p_starts = [pl.multiple_of(jnp.maximum(qi - 1, 0) * qb, qb) for qi in qi_of]
    d_keys = [half if r == 0 else qb for _, r in chains]
    d_masks = [causal[0:half, 0:half] if r == 0 else causal[half:, :] for _, r in chains]
    seq_chains = []
    for pr in range(len(units)):
        for prev in (False, True):
            for ci, (h, r) in enumerate(chains):
                seq_chains.append((pr, prev, ci, h, r))
    n_seq = len(seq_chains)
    p_vmasks = [[jnp.logical_and(m, qi > 0) for m in vmasks] for qi in qi_of]

    logits, fronts, spent_of = [None] * n_seq, [None] * n_seq, {}
    acc_of = {}
    def span(prev, r):
        if prev:
            return (0, qb) if r == 0 else (half, half)
        return (0, half) if r == 0 else (0, qb)

    for t in range(n_seq + 2 * SB_LAG):
        c2, c1 = t - 2 * SB_LAG, t - SB_LAG
        if 0 <= c2 < n_seq:
            pr, prev, ci, h, r = seq_chains[c2]
            k0, n = span(prev, r)
            mask = None if prev else d_masks[ci]
            spent = spent_of[(pr, ci)] if prev else jnp.zeros((half, 1), F32)
            log_beta, p_first, later = fronts[c2]
            fronts[c2] = None
            w, spent_of[(pr, ci)] = back(log_beta, p_first, later, spent, mask)
            start = p_starts[pr] if prev else d_starts[pr]
            vb = v_ref[pl.ds(start, qb), lanes_of[pr]]
            vh = jnp.where(p_vmasks[pr][h] if prev else vmasks[h], vb[k0:k0 + n, :], zero)
            part = _dot(w, vh)
            acc_of[(pr, r)] = part if (pr, r) not in acc_of else acc_of[(pr, r)] + part
        if 0 <= c1 < n_seq:
            pr, prev, ci, h, r = seq_chains[c1]
            mask = None if prev else d_masks[ci]
            fronts[c1] = front_tail(logits[c1], span(prev, r)[1], mask)
            logits[c1] = None
        if t < n_seq:
            pr, prev, ci, h, r = seq_chains[t]
            k0, n = span(prev, r)
            start = p_starts[pr] if prev else d_starts[pr]
            kb = k_ref[pl.ds(start, qb), lanes_of[pr]]
            logits[t] = _dot_nt(qcs[pr][ci], kb[k0:k0 + n, :])
    accs_all = [[acc_of[(pr, r)] for r in range(2)] for pr in range(len(units))]
    bp_spent = [[spent_of[(pr, ci)] for ci in range(len(chains))] for pr in range(len(units))]

    late = [ci for ci, (_, r) in enumerate(chains) if r == 1]

    def least(spents):
        m = spents[0]
        for sp in spents[1:]:
            m = jnp.minimum(m, sp)
        return jnp.min(m)

    far_needed = [jnp.logical_and(qi > 0, least([bp_spent[pr][ci] for ci in late]) < SB_DEAD)
                  for pr, qi in enumerate(qi_of)]
    more_needed = [jnp.logical_and(qi > 1, least(bp_spent[pr]) < SB_DEAD) for pr, qi in enumerate(qi_of)]

    def normalised(accs):
        acc = jnp.concatenate(accs, axis=0)
        ms = _pair_head_sum(acc * acc, low_half) * (1.0 / HEAD_DIM)
        return acc * lax.rsqrt(ms + EPS)

    eager = [normalised(accs) for accs in accs_all]

    finals = []
    for pr, sl in enumerate(lanes_of):
        qi, p_start = qi_of[pr], p_starts[pr]

        def far_half(args, pr=pr, sl=sl, p_start=p_start):
            acc, spents = args
            kb, vb = k_ref[pl.ds(p_start, half), sl], v_ref[pl.ds(p_start, half), sl]
            fs = [front(qcs[pr][ci], kb, half, None) for ci in late]
            bs = [back(*f, sp, None) for f, sp in zip(fs, spents)]
            vh = jnp.concatenate(values(vb, half), axis=0)
            return acc + _dot(jnp.concatenate([b[0] for b in bs], axis=1), vh), [b[1] for b in bs]

        def cond(s):
            j, spents, _ = s
            return jnp.logical_and(j >= 0, least(spents) < SB_DEAD)

        def body(s, pr=pr, sl=sl):
            j, spents, accs = s
            start = pl.multiple_of(j * qb, qb)
            kb, vb = k_ref[pl.ds(start, qb), sl], v_ref[pl.ds(start, qb), sl]
            fs = [front(qc, kb, qb, None) for qc in qcs[pr]]
            bs = [back(*f, sp, None) for f, sp in zip(fs, spents)]
            vh = jnp.concatenate(values(vb, qb), axis=0)
            accs = [acc + _dot(jnp.concatenate([bs[2 * r][0], bs[2 * r + 1][0]], axis=1), vh)
                    for r, acc in enumerate(accs)]
            return j - 1, [b[1] for b in bs], accs

        def sweep_on(args, pr=pr, qi=qi, far_half=far_half, cond=cond, body=body):
            accs, spents, _ = args
            late_spent = [spents[ci] for ci in late]
            acc_late, late_spent = lax.cond(far_needed[pr], far_half, lambda a: a, (accs[1], late_spent))
            spents = list(spents)
            for ci, sp in zip(late, late_spent):
                spents[ci] = sp
            _, _, accs = lax.while_loop(cond, body, (qi - 2, spents, [accs[0], acc_late]))
            return normalised(accs)

        finals.append(lax.cond(jnp.logical_or(far_needed[pr], more_needed[pr]), sweep_on,
                               lambda args: args[2], (accs_all[pr], bp_spent[pr], eager[pr])))

    for blk in range(SB_STEP_BLOCKS):
        outs = [finals[pr] for pr, (b, _) in enumerate(units) if b == blk]
        o_ref[blk * qb:(blk + 1) * qb, :] = (jnp.concatenate(outs, axis=1) * g_ref[...]).astype(o_ref.dtype)


def _sb_attention(sb, layer, gain, batch, seq):
    rows = SB_BLOCK * SB_STEP_BLOCKS
    nq = seq // rows
    n = sb.shape[0]
    return pl.pallas_call(
        _sb_kernel,
        grid=(batch, nq),
        in_specs=[
            pl.BlockSpec((rows, SB_WIDTH), lambda b, i: (b * nq + i, 0)),
            pl.BlockSpec((seq, SB_WIDTH), lambda b, i: (b, 1)),
            pl.BlockSpec((seq, SB_WIDTH), lambda b, i: (b, 2)),
            _layer_block(gain, layer, 2),
        ],
        out_specs=pl.BlockSpec((rows, SB_WIDTH), lambda b, i: (b * nq + i, 0)),
        out_shape=jax.ShapeDtypeStruct((n, SB_WIDTH), BF16),
        compiler_params=_cparams(("parallel", "arbitrary")),
        name="sb_attention",
    )(sb, sb, sb, gain)


RW_BLOCK = 1024
RW_PART = 128


def _rw_body(x_ref, mu_ref, w0_ref, a0_ref, kk_ref, ka_ref, rk_ref, gng_ref, gnb_ref,
             uphi_ref, uplo_ref, o_ref,
             state_s, prev_s, r_s, k_s, v_s, lw_s, a_s, b_s, y_s, lc_s,
             rp_s, yp_s, p_s, q_s, ge_s, bonus_s, gate_s, filler=None):
    tb = RW_BLOCK
    hb = RW_PART
    c = CHUNK
    pairs = RW_WIDTH // LANES

    lane = lax.broadcasted_iota(jnp.int32, (1, LANES), 1)
    low_half = lane < HEAD_DIM
    tr = lax.broadcasted_iota(jnp.int32, (hb, hb), 0)
    tc = lax.broadcasted_iota(jnp.int32, (hb, hb), 1)
    tri = jnp.logical_and(tr // c == tc // c, tc <= tr).astype(BF16)
    first = lax.broadcasted_iota(jnp.int32, (hb, 1), 0) == 0

    def prologue(part):
        rows = slice(part * hb, (part + 1) * hb)
        x = x_ref[rows, :]
        before = prev_s[...] if part == 0 else x_ref[part * hb - 1:part * hb, :]
        prev = jnp.where(first, before, pltpu.roll(x, 1, 0))
        xs = x + (prev - x) * mu_ref[...]
        yield
        r = xs[:, 0:RW_WIDTH]
        k = xs[:, RW_WIDTH:2 * RW_WIDTH]
        v = xs[:, 2 * RW_WIDTH:3 * RW_WIDTH]
        code = xs[:, 3 * RW_WIDTH:]
        act = jnp.where(lane < 32, jnp.tanh(code), jnp.where(lane < 64, code, jax.nn.sigmoid(code)))
        up = jnp.concatenate(
            [_dot_x3(act, uphi_ref[:, 0:RW_WIDTH], uplo_ref[:, 0:RW_WIDTH]),
             _dot(act.astype(BF16), uphi_ref[:, RW_WIDTH:])], axis=1)
        r_s[rows, :] = r
        v_s[rows, :] = v
        gate_s[rows, :] = up[:, 2 * RW_WIDTH:]
        yield
        wpre = w0_ref[...] + up[:, 0:RW_WIDTH]
        log_w = -(jnp.maximum(-wpre, 0.0) + jnp.log(1.0 + jnp.exp(-jnp.abs(wpre)))) - 0.5
        lw = -jnp.exp(log_w)
        lw_s[rows, :] = lw
        yield
        lc_s[rows, :] = _dot_exact_lhs(tri, lw)
        yield
        a = jax.nn.sigmoid(a0_ref[...] + up[:, RW_WIDTH:2 * RW_WIDTH])
        kmod = k * (1.0 + (a - 1.0) * ka_ref[...])
        k_s[rows, :] = kmod
        yield
        kk = k * kk_ref[...]
        bonus_in = r * kmod * rk_ref[...]
        for p in range(pairs):
            sl = slice(p * LANES, (p + 1) * LANES)
            ss = _pair_head_sum(kk[:, sl] * kk[:, sl], low_half)
            kkn = kk[:, sl] * lax.rsqrt(jnp.maximum(ss, 1e-12))
            a_s[rows, sl] = -kkn
            b_s[rows, sl] = kkn * a[:, sl]
            bonus_s[rows, sl] = _pair_head_sum(bonus_in[:, sl], low_half) * v[:, sl]
            yield

    rr = lax.broadcasted_iota(jnp.int32, (2 * c, 2 * c), 0)
    cc = lax.broadcasted_iota(jnp.int32, (2 * c, 2 * c), 1)
    same_head = (rr // c) == (cc // c)
    strict = jnp.logical_and(same_head, cc < rr)
    incl = jnp.logical_and(same_head, cc <= rr)
    eye = (rr == cc).astype(F32)
    m_lo = low_half
    m_hi = jnp.logical_not(low_half)

    def stack(xp):
        return jnp.concatenate([jnp.where(m_lo, xp, 0.0), jnp.where(m_hi, xp, 0.0)], axis=0)

    def prepare(chunks, tick):
        chains = [(ci, p) for ci in chunks for p in range(pairs)]
        pre = []
        for ci, p in chains:
            rows = slice(ci * c, (ci + 1) * c)
            sl = slice(p * LANES, (p + 1) * LANES)
            lwc = lw_s[rows, sl]
            lc = lc_s[rows, sl]
            e_incl = jnp.exp(lc)
            e_excl = jnp.exp(lc - lwc)
            e_inv = jnp.exp(-lc)
            g_end = e_incl[c - 1:c, :]
            ag = stack(a_s[rows, sl] * e_excl).astype(BF16)
            rg = stack(r_s[rows, sl] * e_incl).astype(BF16)
            bd = b_s[rows, sl] * e_inv
            kd = k_s[rows, sl] * e_inv
            vst = stack(v_s[rows, sl]).astype(BF16)
            lhs = jnp.concatenate([ag, rg], axis=0)
            rhs = jnp.concatenate([bd, bd, kd, kd], axis=0).astype(BF16)
            bde = stack(bd * g_end).astype(BF16)
            kde = stack(kd * g_end).astype(BF16)
            idx = ci * pairs + p
            ge_s[idx] = jnp.broadcast_to(g_end, (LANES, LANES)).T
            pre.append((idx, ag, rg, vst, lhs, rhs, kde, bde))
        tick()
        gs = [_dot_nt(t[4], t[5]) for t in pre]
        tick()
        npows, xs_, avs, arbs = [], [], [], []
        for t, g in zip(pre, gs):
            a_ab = jnp.where(strict, g[0:2 * c, 0:2 * c], 0.0)
            a_ak = jnp.where(strict, g[0:2 * c, 2 * c:], 0.0)
            a_rb = jnp.where(incl, g[2 * c:, 0:2 * c], 0.0)
            a_rk = jnp.where(incl, g[2 * c:, 2 * c:], 0.0)
            arbs.append(a_rb.astype(BF16))
            npows.append(a_ab)
            avs.append(jnp.concatenate([a_ak, a_rk], axis=0).astype(BF16))
        avs = [_dot(av, t[3]) for av, t in zip(avs, pre)]
        svs = [_dot_tn(t[6], t[3]) for t in pre]
        tick()
        for t, av in zip(pre, avs):
            xs_.append(jnp.concatenate([t[1], av[0:2 * c, :].astype(BF16)], axis=1))
        tinvs = [eye + n for n in npows]
        nbs = [n.astype(BF16) for n in npows]
        npows = [_dot(nb, nb) for nb in nbs]
        tick()
        for it in range(1, 6):
            nbs = [n.astype(BF16) for n in npows]
            tbs = [tv.astype(BF16) for tv in tinvs]
            if it < 5:
                prods = [_dot(nb, jnp.concatenate([nb, tb], axis=1)) for nb, tb in zip(nbs, tbs)]
                npows = [pr[:, 0:LANES] for pr in prods]
                tinvs = [tv + pr[:, LANES:] for tv, pr in zip(tinvs, prods)]
            else:
                tinvs = [tv + _dot(nb, tb) for tv, nb, tb in zip(tinvs, nbs, tbs)]
            tick()
        xbs = [_dot(tv.astype(BF16), x0).astype(BF16) for tv, x0 in zip(tinvs, xs_)]
        ras = [_dot(arb, xb) for arb, xb in zip(arbs, xbs)]
        pqs = [_dot_tn(t[7], xb) for xb, t in zip(xbs, pre)]
        for t, av, sv, ra, pq in zip(pre, avs, svs, ras, pqs):
            idx = t[0]
            rp_s[idx] = (t[2].astype(F32) + ra[:, 0:LANES]).astype(BF16)
            yp_s[idx] = ra[:, LANES:] + av[2 * c:, :]
            p_s[idx] = pq[:, 0:LANES].astype(BF16)
            q_s[idx] = pq[:, LANES:] + sv
        tick()

    half_chunks = hb // c
    states = [state_s[p] for p in range(pairs)]

    def finish(part):
        for ci in range(part * half_chunks, (part + 1) * half_chunks):
            for p in range(pairs):
                idx = ci * pairs + p
                sb = states[p].astype(BF16)
                y = _dot(rp_s[idx], sb) + yp_s[idx]
                y_s[ci * c:(ci + 1) * c, p * LANES:(p + 1) * LANES] = y[0:c, :] + y[c:2 * c, :]
                states[p] = states[p] * ge_s[idx] + _dot(p_s[idx], sb) + q_s[idx]
            yield
        rows = slice(part * hb, (part + 1) * hb)
        for p in range(pairs):
            sl = slice(p * LANES, (p + 1) * LANES)
            y = y_s[rows, sl]
            mean = _pair_head_sum(y, low_half) * (1.0 / HEAD_DIM)
            d = y - mean
            var = _pair_head_sum(d * d, low_half) * (1.0 / HEAD_DIM)
            yn = d * lax.rsqrt(var + RW_GN_EPS) * gng_ref[:, sl] + gnb_ref[:, sl]
            o_ref[rows, sl] = ((yn + bonus_s[rows, sl]) * gate_s[rows, sl]).astype(o_ref.dtype)
            yield

    def advance(*gens):
        def tick():
            for gen in gens:
                if gen is not None:
                    next(gen, None)
        return tick

    def drain(gen):
        for _ in gen:
            pass

    n_parts = tb // hb
    drain(prologue(0))
    for part in range(n_parts):
        ahead = prologue(part + 1) if part + 1 < n_parts else None
        behind = finish(part - 1) if part >= 1 else None
        prepare(range(part * half_chunks, (part + 1) * half_chunks), advance(ahead, behind, filler))
        for gen in (ahead, behind):
            if gen is not None:
                drain(gen)
    prev_s[...] = x_ref[tb - 1:tb, :]
    drain(finish(n_parts - 1))
    for p in range(pairs):
        state_s[p] = states[p]


RW_N_IN, RW_N_SCRATCH = 11, 17
GLA_N_IN, GLA_N_SCRATCH = 5, 5


def _recurrent_kernel(*refs):
    n_in = RW_N_IN + GLA_N_IN
    rw_in, gl_in = refs[:RW_N_IN], refs[RW_N_IN:n_in]
    o_rw, o_gl = refs[n_in], refs[n_in + 1]
    rw_scr = refs[n_in + 2:n_in + 2 + RW_N_SCRATCH]
    gl_scr = refs[n_in + 2 + RW_N_SCRATCH:]

    @pl.when(pl.program_id(1) == 0)
    def _():
        for state in (rw_scr[0], rw_scr[1], gl_scr[0]):
            state[...] = jnp.zeros_like(state)

    gla = _gla_body(*gl_in, o_gl, *gl_scr)
    _rw_body(*rw_in, o_rw, *rw_scr, filler=gla)
    for _ in gla:
        pass


def _recurrent_mixers(rw, gl, layer, rprm, gprm, batch, seq):
    n = rw.shape[0]
    tb = RW_BLOCK
    nb = seq // tb
    rows = lambda width: pl.BlockSpec((tb, width), lambda b, i: (b * nb + i, 0))
    rw_names = ("mu", "w0", "a0", "k_k", "k_a", "r_k", "gn_g", "gn_b", "up_hi", "up_lo")
    gl_names = ("gate_b", "norm_g", "up_hi", "up_lo")
    rw_params = [rprm[k] for k in rw_names]
    gl_params = [gprm[k] for k in gl_names]
    assert 1 + len(rw_params) == RW_N_IN and 1 + len(gl_params) == GLA_N_IN
    blk = lambda: pltpu.VMEM((tb, RW_WIDTH), F32)
    n_chain = (tb // CHUNK) * (RW_WIDTH // LANES)
    tile = lambda r, dt: pltpu.VMEM((n_chain, r, LANES), dt)
    rw_scratch = [pltpu.VMEM((RW_WIDTH // LANES, LANES, LANES), F32),
                  pltpu.VMEM((1, RW_IN_WIDTH), F32),
                  blk(), blk(), blk(), blk(), blk(), blk(), blk(), blk(),
                  tile(LANES, BF16), tile(LANES, F32), tile(LANES, BF16), tile(LANES, F32),
                  tile(LANES, F32), blk(), blk()]
    gl_scratch = [pltpu.VMEM((GLA_VAL_WIDTH, GLA_KEY_WIDTH), F32),
                  pltpu.VMEM((tb, GLA_KEY_WIDTH), F32),
                  pltpu.VMEM((tb, GLA_KEY_WIDTH), F32),
                  pltpu.VMEM((tb, GLA_KEY_WIDTH), F32),
                  pltpu.VMEM((tb, GLA_VAL_WIDTH), F32)]
    assert len(rw_scratch) == RW_N_SCRATCH and len(gl_scratch) == GLA_N_SCRATCH
    return pl.pallas_call(
        _recurrent_kernel,
        grid=(batch, nb),
        in_specs=([rows(RW_IN_WIDTH)] + [_layer_block(a, layer, 2) for a in rw_params]
                  + [rows(GLA_PAD_WIDTH)] + [_layer_block(a, layer, 2) for a in gl_params]),
        out_specs=[rows(RW_WIDTH), rows(GLA_VAL_WIDTH)],
        out_shape=[jax.ShapeDtypeStruct((n, RW_WIDTH), BF16),
                   jax.ShapeDtypeStruct((n, GLA_VAL_WIDTH), BF16)],
        scratch_shapes=rw_scratch + gl_scratch,
        compiler_params=_cparams(("parallel", "arbitrary")),
        name="rwkv7_gla",
    )(rw, *rw_params, gl, *gl_params)


GLA_BLOCK = RW_BLOCK
GLA_SUB = 16
GLA_EXP_CAP = 80.0


def _gla_body(x_ref, gb_ref, ng_ref, uphi_ref, uplo_ref, o_ref, state_s, q_s, k_s, la_s, o_s):
    tb = GLA_BLOCK
    c = CHUNK
    heads = GLA_VAL_WIDTH // HEAD_DIM

    kw, vw = GLA_KEY_WIDTH, GLA_VAL_WIDTH
    code = x_ref[:, 2 * kw + 2 * vw:]
    pre = _dot_x3(code, uphi_ref[...], uplo_ref[...]) + gb_ref[...]
    la_s[...] = -(jnp.maximum(-pre, 0.0) + jnp.log(1.0 + jnp.exp(-jnp.abs(pre)))) * (1.0 / GLA_GATE_NORMALIZER)
    q_s[...] = x_ref[:, 0:kw] * (GLA_DK ** -0.5)
    k_s[...] = x_ref[:, kw:2 * kw]
    yield

    tr = lax.broadcasted_iota(jnp.int32, (c, c), 0)
    tc = lax.broadcasted_iota(jnp.int32, (c, c), 1)
    tri = (tc <= tr).astype(BF16)
    sub = GLA_SUB
    n_sub = c // sub
    srow = lax.broadcasted_iota(jnp.int32, (heads * c, c), 0)
    scol = lax.broadcasted_iota(jnp.int32, (heads * c, c), 1)
    causal = scol <= (srow // (heads * sub)) * sub + srow % sub
    klane = lax.broadcasted_iota(jnp.int32, (1, kw), 1) // GLA_DK
    vlane = lax.broadcasted_iota(jnp.int32, (1, vw), 1) // HEAD_DIM
    st_r = lax.broadcasted_iota(jnp.int32, (vw, kw), 0) // HEAD_DIM
    st_c = lax.broadcasted_iota(jnp.int32, (vw, kw), 1) // GLA_DK
    st_mask = st_r == st_c

    n_chunk = tb // c
    pre = []
    for ci in range(n_chunk):
        rows = slice(ci * c, (ci + 1) * c)
        bc = _dot_exact_lhs(tri, la_s[rows, :])
        e_pos = jnp.exp(bc)
        qc = q_s[rows, :]
        kc = k_s[rows, :]
        k2 = (kc * jnp.exp(bc[c - 1:c, :] - bc)).astype(BF16)
        vb = x_ref[rows, 2 * kw:2 * kw + vw].astype(BF16)
        qsts, kes = [], []
        for i in range(n_sub):
            srows = slice(i * sub, (i + 1) * sub)
            ref = bc[i * sub - 1:i * sub, :] if i > 0 else jnp.zeros((1, kw), F32)
            qi = qc[srows, :] * jnp.exp(bc[srows, :] - ref)
            qsts.append(jnp.concatenate([jnp.where(klane == h, qi, 0.0) for h in range(heads)],
                                        axis=0).astype(BF16))
            kes.append((kc * jnp.exp(jnp.minimum(ref - bc, GLA_EXP_CAP))).astype(BF16))
        pre.append(((qc * e_pos).astype(BF16), k2, vb, qsts, kes, e_pos[c - 1:c, :]))
        yield
    scores = []
    for t in pre:
        scores.append(jnp.where(
            causal, jnp.concatenate([_dot_nt(qs, ke) for qs, ke in zip(t[3], t[4])], axis=0),
            0.0).astype(BF16))
        yield
    o_sts = [_dot(sc, t[2]) for sc, t in zip(scores, pre)]
    incs = [jnp.where(st_mask, _dot_tn(t[2], t[1]), 0.0) for t in pre]
    yield
    state = state_s[...]
    for ci in range(n_chunk):
        intra = []
        for i in range(n_sub):
            acc = None
            for h in range(heads):
                r0 = (i * heads + h) * sub
                part = jnp.where(vlane == h, o_sts[ci][r0:r0 + sub, :], 0.0)
                acc = part if acc is None else acc + part
            intra.append(acc)
        o_s[ci * c:(ci + 1) * c, :] = _dot_nt(pre[ci][0], state.astype(BF16)) + jnp.concatenate(intra, axis=0)
        state = state * pre[ci][5] + incs[ci]
        yield
    state_s[...] = state

    lane = lax.broadcasted_iota(jnp.int32, (1, LANES), 1)
    low_half = lane < HEAD_DIM
    outs = []
    for p in range(vw // LANES):
        o = o_s[:, p * LANES:(p + 1) * LANES]
        ms = _pair_head_sum(o * o, low_half) * (1.0 / HEAD_DIM)
        outs.append(o * lax.rsqrt(ms + EPS))
    g = x_ref[:, 2 * kw + vw:2 * kw + 2 * vw]
    on = jnp.concatenate(outs, axis=1) * ng_ref[...]
    o_ref[...] = (on * (g * jax.nn.sigmoid(g))).astype(o_ref.dtype)


MLP_ROWS = 1024
MLP_FF_TILE = 1024


def _mlp_kernel(x_ref, sb_ref, rw_ref, gl_ref, wsb_ref, wrw_ref, wgl_ref, gmix_ref,
                gpre_ref, w1_ref, w2_ref, gpost_ref, o_ref):
    mixed = _dot(sb_ref[...], wsb_ref[...]) + _dot(rw_ref[...], wrw_ref[...]) + _dot(gl_ref[...], wgl_ref[...])
    x1 = x_ref[...] + _rms(mixed, gmix_ref[...])
    h = _rms(x1, gpre_ref[...]).astype(BF16)
    ff = None
    for j in range(D_FF // MLP_FF_TILE):
        cols = slice(j * MLP_FF_TILE, (j + 1) * MLP_FF_TILE)
        a = jnp.maximum(_dot(h, w1_ref[:, cols]), 0.0)
        part = _dot((a * a).astype(BF16), w2_ref[cols, :])
        ff = part if ff is None else ff + part
    o_ref[...] = x1 + _rms(ff, gpost_ref[...])


def _mlp(x2, o_sb, o_rw, o_gl, layer, w_out, g_mix, g_pre, w1, w2, g_post):
    n = x2.shape[0]
    tm = MLP_ROWS
    full = lambda w: _layer_block(w, layer, 1, single_buffer=True)
    row = lambda width: pl.BlockSpec((tm, width), lambda i: (i, 0))

    def out_rows(width, first):
        assert first % width == 0
        return pl.BlockSpec((None, width, D_MODEL), lambda i: (layer, first // width, 0),
                            pipeline_mode=pl.Buffered(1))

    return pl.pallas_call(
        _mlp_kernel,
        grid=(n // tm,),
        in_specs=[row(D_MODEL), row(SB_WIDTH), row(RW_WIDTH), row(GLA_VAL_WIDTH),
                  out_rows(SB_WIDTH, 0), out_rows(RW_WIDTH, SB_WIDTH),
                  out_rows(GLA_VAL_WIDTH, SB_WIDTH + RW_WIDTH), full(g_mix),
                  full(g_pre), full(w1), full(w2), full(g_post)],
        out_specs=row(D_MODEL),
        out_shape=jax.ShapeDtypeStruct((n, D_MODEL), F32),
        compiler_params=_cparams(("parallel",)),
        name="out_proj_mlp",
    )(x2, o_sb, o_rw, o_gl, w_out, w_out, w_out, g_mix, g_pre, w1, w2, g_post)


def _hi_lo(w):
    hi = w.astype(BF16)
    return hi, (w - hi.astype(F32)).astype(BF16)


def _prep_weights(w_in, rw_w_up, rw_a_up, rw_g_up, gla_gate_up, w_out, w_ff1, w_ff2):
    depth = w_in.shape[0]
    sb_end = 3 * SB_WIDTH
    rw_end = sb_end + RW_IN_WIDTH
    w_sb = w_in[:, :, :sb_end].astype(BF16)
    w_rw = w_in[:, :, sb_end:rw_end].astype(BF16)
    g0 = rw_end
    kw, vw, gr = GLA_KEY_WIDTH, GLA_VAL_WIDTH, GLA_GATE_RANK
    w_gl = jnp.concatenate([
        w_in[:, :, g0:g0 + 2 * kw + vw],
        w_in[:, :, g0 + 2 * kw + vw + gr:g0 + GLA_IN_WIDTH],
        w_in[:, :, g0 + 2 * kw + vw:g0 + 2 * kw + vw + gr],
        jnp.zeros((depth, D_MODEL, GLA_PAD_WIDTH - GLA_IN_WIDTH), F32)], axis=2).astype(BF16)
    up = jnp.zeros((depth, RW_CODE, 3 * RW_WIDTH), F32)
    up = up.at[:, 0:32, 0:RW_WIDTH].set(rw_w_up)
    up = up.at[:, 32:64, RW_WIDTH:2 * RW_WIDTH].set(rw_a_up)
    up = up.at[:, 64:128, 2 * RW_WIDTH:].set(rw_g_up)
    rw_up_hi, rw_up_lo = _hi_lo(up)
    gup = jnp.zeros((depth, LANES, GLA_KEY_WIDTH), F32).at[:, 0:gr, :].set(gla_gate_up)
    gl_up_hi, gl_up_lo = _hi_lo(gup)
    return dict(w_sb=w_sb, w_rw=w_rw, w_gl=w_gl, rw_up_hi=rw_up_hi, rw_up_lo=rw_up_lo,
                gl_up_hi=gl_up_hi, gl_up_lo=gl_up_lo, wo=w_out.astype(BF16),
                w1=w_ff1.astype(BF16), w2=w_ff2.astype(BF16))


def kernel(x, pre_mix_g, w_in, sb_norm_g, rw_mu, rw_w0, rw_w_up, rw_a0, rw_a_up, rw_g_up, rw_k_k, rw_k_a, rw_r_k, rw_gn_g, rw_gn_b, gla_gate_up, gla_gate_b, gla_norm_g, w_out, post_mix_g, pre_ffn_g, w_ff1, w_ff2, post_ffn_g):
    batch, seq, d = x.shape
    depth = w_in.shape[0]
    x2 = x.reshape(batch * seq, d)
    vec = lambda t: t[:, None, :]
    wp = _prep_weights(w_in, rw_w_up, rw_a_up, rw_g_up, gla_gate_up, w_out, w_ff1, w_ff2)
    rw_prm = dict(mu=vec(rw_mu), w0=vec(rw_w0), a0=vec(rw_a0), k_k=vec(rw_k_k), k_a=vec(rw_k_a),
                  r_k=vec(rw_r_k), gn_g=vec(rw_gn_g), gn_b=vec(rw_gn_b),
                  up_hi=wp["rw_up_hi"], up_lo=wp["rw_up_lo"])
    gl_prm = dict(gate_b=vec(gla_gate_b), norm_g=vec(gla_norm_g),
                  up_hi=wp["gl_up_hi"], up_lo=wp["gl_up_lo"])
    for l in range(depth):
        sb, rw, gl = _inproj(x2, l, vec(pre_mix_g), wp["w_sb"], wp["w_rw"], wp["w_gl"])
        o_sb = _sb_attention(sb, l, vec(sb_norm_g), batch, seq)
        o_rw, o_gl = _recurrent_mixers(rw, gl, l, rw_prm, gl_prm, batch, seq)
        x2 = _mlp(x2, o_sb, o_rw, o_gl, l, wp["wo"], vec(post_mix_g), vec(pre_ffn_g),
                  wp["w1"], wp["w2"], vec(post_ffn_g))
    return x2.reshape(batch, seq, d)
```

```python
import functools

import jax
import jax.numpy as jnp
from jax import lax
from jax.experimental import pallas as pl
from jax.experimental.pallas import tpu as pltpu

F32 = jnp.float32
BF16 = jnp.bfloat16

D_MODEL = 1024
HEAD_DIM = 64
LANES = 128
SB_WIDTH = 384
RW_WIDTH = 384
RW_IN_WIDTH = 1280
RW_CODE = 128
GLA_KEY_WIDTH = 128
GLA_VAL_WIDTH = 256
GLA_DK = 32
GLA_GATE_RANK = 16
GLA_IN_WIDTH = 784
GLA_PAD_WIDTH = 896
GLA_GATE_NORMALIZER = 16.0
D_FF = 4096
EPS = 1e-6
RW_GN_EPS = 64e-5
CHUNK = 64
VMEM_LIMIT = 58 * 1024 * 1024

SB_BLOCK = 256
LOG2E = 1.4426950408889634
SB_DEAD = 110.0 * LOG2E
SB_CLAMP = 120.0
SB_STEP_BLOCKS = 4
SB_LAG = 4


def _cparams(sem):
    return pltpu.CompilerParams(dimension_semantics=sem, vmem_limit_bytes=VMEM_LIMIT)


def _dot(a, b):
    return jnp.dot(a, b, preferred_element_type=F32)


def _dot_nt(a, b):
    return lax.dot_general(a, b, (((1,), (1,)), ((), ())), preferred_element_type=F32)


def _dot_tn(a, b):
    return lax.dot_general(a, b, (((0,), (0,)), ((), ())), preferred_element_type=F32)


def _split2(x):
    hi = x.astype(BF16)
    lo = (x - hi.astype(F32)).astype(BF16)
    return hi, lo


def _split3(x):
    hi = x.astype(BF16)
    r1 = x - hi.astype(F32)
    mid = r1.astype(BF16)
    lo = (r1 - mid.astype(F32)).astype(BF16)
    return hi, mid, lo


def _dot_x3(a, w_hi, w_lo):
    a_hi, a_lo = _split2(a)
    return _dot(a_hi, w_hi) + (_dot(a_lo, w_hi) + _dot(a_hi, w_lo))


def _dot_exact_lhs(m_bf16, x):
    hi, mid, lo = _split3(x)
    return _dot(m_bf16, hi) + (_dot(m_bf16, mid) + _dot(m_bf16, lo))


def _rms(x, gain):
    return x * lax.rsqrt(jnp.mean(x * x, axis=-1, keepdims=True) + EPS) * gain


def _pair_head_sum(x, low_half):
    s_all = jnp.sum(x, axis=-1, keepdims=True)
    s_low = jnp.sum(jnp.where(low_half, x, 0.0), axis=-1, keepdims=True)
    return jnp.where(low_half, s_low, s_all - s_low)


def _inproj_kernel(x_ref, g_ref, wsb_ref, wrw_ref, wgl_ref, sb_ref, rw_ref, gl_ref):
    h = _rms(x_ref[...], g_ref[...]).astype(BF16)
    col = lax.broadcasted_iota(jnp.int32, (1, 3 * SB_WIDTH), 1)
    qscale = jnp.where(col < SB_WIDTH, LOG2E * HEAD_DIM ** -0.5, 1.0)
    sb_ref[...] = (_dot(h, wsb_ref[...]) * qscale).astype(BF16)
    rw_ref[...] = _dot(h, wrw_ref[...])
    gl_ref[...] = _dot(h, wgl_ref[...])


INPROJ_ROWS = 1024


def _layer_block(arr, layer, grid_rank, single_buffer=False):
    index = lambda *_: (layer, 0, 0)
    assert arr.ndim == 3 and grid_rank in (1, 2)
    if single_buffer:
        return pl.BlockSpec((None,) + arr.shape[1:], index, pipeline_mode=pl.Buffered(1))
    return pl.BlockSpec((None,) + arr.shape[1:], index)


def _inproj(x2, layer, gain, w_sb, w_rw, w_gl, tm=INPROJ_ROWS):
    n = x2.shape[0]
    full = lambda w: _layer_block(w, layer, 1, single_buffer=True)
    row = lambda width: pl.BlockSpec((tm, width), lambda i: (i, 0))
    return pl.pallas_call(
        _inproj_kernel,
        grid=(n // tm,),
        in_specs=[row(D_MODEL), full(gain), full(w_sb), full(w_rw), full(w_gl)],
        out_specs=[row(3 * SB_WIDTH), row(RW_IN_WIDTH), row(GLA_PAD_WIDTH)],
        out_shape=[jax.ShapeDtypeStruct((n, 3 * SB_WIDTH), BF16),
                   jax.ShapeDtypeStruct((n, RW_IN_WIDTH), F32),
                   jax.ShapeDtypeStruct((n, GLA_PAD_WIDTH), F32)],
        compiler_params=_cparams(("parallel",)),
        name="in_proj",
    )(x2, gain, w_sb, w_rw, w_gl)


def _sb_kernel(q_ref, k_ref, v_ref, g_ref, o_ref):
    qb = SB_BLOCK
    qis = [pl.program_id(1) * SB_STEP_BLOCKS + blk for blk in range(SB_STEP_BLOCKS)]
    lane = lax.broadcasted_iota(jnp.int32, (1, LANES), 1)
    low_half = lane < HEAD_DIM
    row = lax.broadcasted_iota(jnp.int32, (qb, qb), 0)
    col = lax.broadcasted_iota(jnp.int32, (qb, qb), 1)
    causal = col < row
    suffix = (row > col).astype(BF16)
    half = qb // 2
    pairs = SB_WIDTH // LANES
    zero = jnp.zeros((), BF16)
    vmasks = (low_half, jnp.logical_not(low_half))
    chains = [(h, r) for r in range(2) for h in range(2)]
    units = [(blk, pr) for blk in range(SB_STEP_BLOCKS) for pr in range(pairs)]
    lanes_of = [slice(pr * LANES, (pr + 1) * LANES) for _, pr in units]
    qi_of = [qis[blk] for blk, _ in units]
    qcs = []
    for (blk, _), sl in zip(units, lanes_of):
        q = q_ref[blk * qb:(blk + 1) * qb, sl]
        qhs = [jnp.where(m, q, zero) for m in vmasks]
        qcs.append([qhs[h][r * half:(r + 1) * half, :] for h, r in chains])

    def front(qc, kb, n, mask):
        return front_tail(_dot_nt(qc, kb[0:n, :]), n, mask)

    def front_tail(z, n, mask):
        p = jnp.maximum(z, jnp.log2(1.0 + jnp.exp2(jnp.minimum(z, SB_CLAMP))))
        log_beta = z - p
        if mask is not None:
            p = jnp.where(mask, p, 0.0)
        later = _dot(p.astype(BF16), suffix[0:n, 0:n])
        return log_beta, p[:, 0:1], later

    def back(log_beta, p_first, later, spent, mask):
        w = jnp.exp2(log_beta - (later + spent))
        if mask is not None:
            w = jnp.where(mask, w, 0.0)
        return w.astype(BF16), spent + (later[:, 0:1] + p_first)

    def values(vb, n):
        return [jnp.where(m, vb[0:n, :], zero) for m in vmasks]

    d_starts = [pl.multiple_of(qi * qb, qb) for qi in qi_of]
    p_starts = [pl.multiple_of(jnp.maximum(qi - 1, 0) * qb, qb) for qi in qi_of]
    d_keys = [half if r == 0 else qb for _, r in chains]
    d_masks = [causal[0:half, 0:half] if r == 0 else causal[half:, :] for _, r in chains]
    seq_chains = []
    for pr in range(len(units)):
        for prev in (False, True):
            for ci, (h, r) in enumerate(chains):
                seq_chains.append((pr, prev, ci, h, r))
    n_seq = len(seq_chains)
    p_vmasks = [[jnp.logical_and(m, qi > 0) for m in vmasks] for qi in qi_of]

    logits, fronts, spent_of = [None] * n_seq, [None] * n_seq, {}
    acc_of = {}
    def span(prev, r):
        if prev:
            return (0, qb) if r == 0 else (half, half)
        return (0, half) if r == 0 else (0, qb)

    for t in range(n_seq + 2 * SB_LAG):
        c2, c1 = t - 2 * SB_LAG, t - SB_LAG
        if 0 <= c2 < n_seq:
            pr, prev, ci, h, r = seq_chains[c2]
            k0, n = span(prev, r)
            mask = None if prev else d_masks[ci]
            spent = spent_of[(pr, ci)] if prev else jnp.zeros((half, 1), F32)
            log_beta, p_first, later = fronts[c2]
            fronts[c2] = None
            w, spent_of[(pr, ci)] = back(log_beta, p_first, later, spent, mask)
            start = p_starts[pr] if prev else d_starts[pr]
            vb = v_ref[pl.ds(start, qb), lanes_of[pr]]
            vh = jnp.where(p_vmasks[pr][h] if prev else vmasks[h], vb[k0:k0 + n, :], zero)
            part = _dot(w, vh)
            acc_of[(pr, r)] = part if (pr, r) not in acc_of else acc_of[(pr, r)] + part
        if 0 <= c1 < n_seq:
            pr, prev, ci, h, r = seq_chains[c1]
            mask = None if prev else d_masks[ci]
            fronts[c1] = front_tail(logits[c1], span(prev, r)[1], mask)
            logits[c1] = None
        if t < n_seq:
            pr, prev, ci, h, r = seq_chains[t]
            k0, n = span(prev, r)
            start = p_starts[pr] if prev else d_starts[pr]
            kb = k_ref[pl.ds(start, qb), lanes_of[pr]]
            logits[t] = _dot_nt(qcs[pr][ci], kb[k0:k0 + n, :])
    accs_all = [[acc_of[(pr, r)] for r in range(2)] for pr in range(len(units))]
    bp_spent = [[spent_of[(pr, ci)] for ci in range(len(chains))] for pr in range(len(units))]

    late = [ci for ci, (_, r) in enumerate(chains) if r == 1]

    def least(spents):
        m = spents[0]
        for sp in spents[1:]:
            m = jnp.minimum(m, sp)
        return jnp.min(m)

    far_needed = [jnp.logical_and(qi > 0, least([bp_spent[pr][ci] for ci in late]) < SB_DEAD)
                  for pr, qi in enumerate(qi_of)]
    more_needed = [jnp.logical_and(qi > 1, least(bp_spent[pr]) < SB_DEAD) for pr, qi in enumerate(qi_of)]

    def normalised(accs):
        acc = jnp.concatenate(accs, axis=0)
        ms = _pair_head_sum(acc * acc, low_half) * (1.0 / HEAD_DIM)
        return acc * lax.rsqrt(ms + EPS)

    eager = [normalised(accs) for accs in accs_all]

    finals = []
    for pr, sl in enumerate(lanes_of):
        qi, p_start = qi_of[pr], p_starts[pr]

        def far_half(args, pr=pr, sl=sl, p_start=p_start):
            acc, spents = args
            kb, vb = k_ref[pl.ds(p_start, half), sl], v_ref[pl.ds(p_start, half), sl]
            fs = [front(qcs[pr][ci], kb, half, None) for ci in late]
            bs = [back(*f, sp, None) for f, sp in zip(fs, spents)]
            vh = jnp.concatenate(values(vb, half), axis=0)
            return acc + _dot(jnp.concatenate([b[0] for b in bs], axis=1), vh), [b[1] for b in bs]

        def cond(s):
            j, spents, _ = s
            return jnp.logical_and(j >= 0, least(spents) < SB_DEAD)

        def body(s, pr=pr, sl=sl):
            j, spents, accs = s
            start = pl.multiple_of(j * qb, qb)
            kb, vb = k_ref[pl.ds(start, qb), sl], v_ref[pl.ds(start, qb), sl]
            fs = [front(qc, kb, qb, None) for qc in qcs[pr]]
            bs = [back(*f, sp, None) for f, sp in zip(fs, spents)]
            vh = jnp.concatenate(values(vb, qb), axis=0)
            accs = [acc + _dot(jnp.concatenate([bs[2 * r][0], bs[2 * r + 1][0]], axis=1), vh)
                    for r, acc in enumerate(accs)]
            return j - 1, [b[1] for b in bs], accs

        def sweep_on(args, pr=pr, qi=qi, far_half=far_half, cond=cond, body=body):
            accs, spents, _ = args
            late_spent = [spents[ci] for ci in late]
            acc_late, late_spent = lax.cond(far_needed[pr], far_half, lambda a: a, (accs[1], late_spent))
            spents = list(spents)
            for ci, sp in zip(late, late_spent):
                spents[ci] = sp
            _, _, accs = lax.while_loop(cond, body, (qi - 2, spents, [accs[0], acc_late]))
            return normalised(accs)

        finals.append(lax.cond(jnp.logical_or(far_needed[pr], more_needed[pr]), sweep_on,
                               lambda args: args[2], (accs_all[pr], bp_spent[pr], eager[pr])))

    for blk in range(SB_STEP_BLOCKS):
        outs = [finals[pr] for pr, (b, _) in enumerate(units) if b == blk]
        o_ref[blk * qb:(blk + 1) * qb, :] = (jnp.concatenate(outs, axis=1) * g_ref[...]).astype(o_ref.dtype)


def _sb_attention(sb, layer, gain, batch, seq):
    rows = SB_BLOCK * SB_STEP_BLOCKS
    nq = seq // rows
    n = sb.shape[0]
    return pl.pallas_call(
        _sb_kernel,
        grid=(batch, nq),
        in_specs=[
            pl.BlockSpec((rows, SB_WIDTH), lambda b, i: (b * nq + i, 0)),
            pl.BlockSpec((seq, SB_WIDTH), lambda b, i: (b, 1)),
            pl.BlockSpec((seq, SB_WIDTH), lambda b, i: (b, 2)),
            _layer_block(gain, layer, 2),
        ],
        out_specs=pl.BlockSpec((rows, SB_WIDTH), lambda b, i: (b * nq + i, 0)),
        out_shape=jax.ShapeDtypeStruct((n, SB_WIDTH), BF16),
        compiler_params=_cparams(("parallel", "arbitrary")),
        name="sb_attention",
    )(sb, sb, sb, gain)


RW_BLOCK = 1024
RW_PART = 128


def _rw_body(x_ref, mu_ref, w0_ref, a0_ref, kk_ref, ka_ref, rk_ref, gng_ref, gnb_ref,
             uphi_ref, uplo_ref, o_ref,
             state_s, prev_s, r_s, k_s, v_s, lw_s, a_s, b_s, y_s, lc_s,
             rp_s, yp_s, p_s, q_s, ge_s, bonus_s, gate_s, filler=None):
    tb = RW_BLOCK
    hb = RW_PART
    c = CHUNK
    pairs = RW_WIDTH // LANES

    lane = lax.broadcasted_iota(jnp.int32, (1, LANES), 1)
    low_half = lane < HEAD_DIM
    tr = lax.broadcasted_iota(jnp.int32, (hb, hb), 0)
    tc = lax.broadcasted_iota(jnp.int32, (hb, hb), 1)
    tri = jnp.logical_and(tr // c == tc // c, tc <= tr).astype(BF16)
    first = lax.broadcasted_iota(jnp.int32, (hb, 1), 0) == 0

    def prologue(part):
        rows = slice(part * hb, (part + 1) * hb)
        x = x_ref[rows, :]
        before = prev_s[...] if part == 0 else x_ref[part * hb - 1:part * hb, :]
        prev = jnp.where(first, before, pltpu.roll(x, 1, 0))
        xs = x + (prev - x) * mu_ref[...]
        yield
        r = xs[:, 0:RW_WIDTH]
        k = xs[:, RW_WIDTH:2 * RW_WIDTH]
        v = xs[:, 2 * RW_WIDTH:3 * RW_WIDTH]
        code = xs[:, 3 * RW_WIDTH:]
        act = jnp.where(lane < 32, jnp.tanh(code), jnp.where(lane < 64, code, jax.nn.sigmoid(code)))
        up = jnp.concatenate(
            [_dot_x3(act, uphi_ref[:, 0:RW_WIDTH], uplo_ref[:, 0:RW_WIDTH]),
             _dot(act.astype(BF16), uphi_ref[:, RW_WIDTH:])], axis=1)
        r_s[rows, :] = r
        v_s[rows, :] = v
        gate_s[rows, :] = up[:, 2 * RW_WIDTH:]
        yield
        wpre = w0_ref[...] + up[:, 0:RW_WIDTH]
        log_w = -(jnp.maximum(-wpre, 0.0) + jnp.log(1.0 + jnp.exp(-jnp.abs(wpre)))) - 0.5
        lw = -jnp.exp(log_w)
        lw_s[rows, :] = lw
        yield
        lc_s[rows, :] = _dot_exact_lhs(tri, lw)
        yield
        a = jax.nn.sigmoid(a0_ref[...] + up[:, RW_WIDTH:2 * RW_WIDTH])
        kmod = k * (1.0 + (a - 1.0) * ka_ref[...])
        k_s[rows, :] = kmod
        yield
        kk = k * kk_ref[...]
        bonus_in = r * kmod * rk_ref[...]
        for p in range(pairs):
            sl = slice(p * LANES, (p + 1) * LANES)
            ss = _pair_head_sum(kk[:, sl] * kk[:, sl], low_half)
            kkn = kk[:, sl] * lax.rsqrt(jnp.maximum(ss, 1e-12))
            a_s[rows, sl] = -kkn
            b_s[rows, sl] = kkn * a[:, sl]
            bonus_s[rows, sl] = _pair_head_sum(bonus_in[:, sl], low_half) * v[:, sl]
            yield

    rr = lax.broadcasted_iota(jnp.int32, (2 * c, 2 * c), 0)
    cc = lax.broadcasted_iota(jnp.int32, (2 * c, 2 * c), 1)
    same_head = (rr // c) == (cc // c)
    strict = jnp.logical_and(same_head, cc < rr)
    incl = jnp.logical_and(same_head, cc <= rr)
    eye = (rr == cc).astype(F32)
    m_lo = low_half
    m_hi = jnp.logical_not(low_half)

    def stack(xp):
        xb = xp.astype(BF16)
        none = jnp.zeros((), BF16)
        return jnp.concatenate([jnp.where(m_lo, xb, none), jnp.where(m_hi, xb, none)], axis=0)

    def prepare(chunks, tick):
        chains = [(ci, p) for ci in chunks for p in range(pairs)]
        pre = []
        for ci, p in chains:
            rows = slice(ci * c, (ci + 1) * c)
            sl = slice(p * LANES, (p + 1) * LANES)
            lwc = lw_s[rows, sl]
            lc = lc_s[rows, sl]
            e_incl = jnp.exp(lc)
            e_excl = jnp.exp(lc - lwc)
            e_inv = jnp.exp(-lc)
            g_end = e_incl[c - 1:c, :]
            ag = stack(a_s[rows, sl] * e_excl)
            rg = stack(r_s[rows, sl] * e_incl)
            bd = b_s[rows, sl] * e_inv
            kd = k_s[rows, sl] * e_inv
            vst = stack(v_s[rows, sl])
            lhs = jnp.concatenate([ag, rg], axis=0)
            rhs = jnp.concatenate([bd, bd, kd, kd], axis=0).astype(BF16)
            bde = stack(bd * g_end)
            kde = stack(kd * g_end)
            idx = ci * pairs + p
            ge_s[idx] = jnp.broadcast_to(g_end, (LANES, LANES)).T
            pre.append((idx, ag, rg, vst, lhs, rhs, kde, bde))
        tick()
        gs = [_dot_nt(t[4], t[5]) for t in pre]
        tick()
        npows, xs_, avs, arbs = [], [], [], []
        for t, g in zip(pre, gs):
            a_ab = jnp.where(strict, g[0:2 * c, 0:2 * c], 0.0)
            a_ak = jnp.where(strict, g[0:2 * c, 2 * c:], 0.0)
            a_rb = jnp.where(incl, g[2 * c:, 0:2 * c], 0.0)
            a_rk = jnp.where(incl, g[2 * c:, 2 * c:], 0.0)
            arbs.append(a_rb.astype(BF16))
            npows.append(a_ab)
            avs.append(jnp.concatenate([a_ak, a_rk], axis=0).astype(BF16))
        avs = [_dot(av, t[3]) for av, t in zip(avs, pre)]
        svs = [_dot_tn(t[6], t[3]) for t in pre]
        tick()
        for t, av in zip(pre, avs):
            xs_.append(jnp.concatenate([t[1], av[0:2 * c, :].astype(BF16)], axis=1))
        tinvs = [eye + n for n in npows]
        nbs = [n.astype(BF16) for n in npows]
        npows = [_dot(nb, nb) for nb in nbs]
        tick()
        for it in range(1, 6):
            nbs = [n.astype(BF16) for n in npows]
            tbs = [tv.astype(BF16) for tv in tinvs]
            if it < 5:
                prods = [_dot(nb, jnp.concatenate([nb, tb], axis=1)) for nb, tb in zip(nbs, tbs)]
                npows = [pr[:, 0:LANES] for pr in prods]
                tinvs = [tv + pr[:, LANES:] for tv, pr in zip(tinvs, prods)]
            else:
                tinvs = [tv + _dot(nb, tb) for tv, nb, tb in zip(tinvs, nbs, tbs)]
            tick()
        xbs = [_dot(tv.astype(BF16), x0).astype(BF16) for tv, x0 in zip(tinvs, xs_)]
        ras = [_dot(arb, xb) for arb, xb in zip(arbs, xbs)]
        pqs = [_dot_tn(t[7], xb) for xb, t in zip(xbs, pre)]
        for t, av, sv, ra, pq in zip(pre, avs, svs, ras, pqs):
            idx = t[0]
            rp_s[idx] = (t[2].astype(F32) + ra[:, 0:LANES]).astype(BF16)
            yp_s[idx] = ra[:, LANES:] + av[2 * c:, :]
            p_s[idx] = pq[:, 0:LANES].astype(BF16)
            q_s[idx] = pq[:, LANES:] + sv
        tick()

    half_chunks = hb // c
    states = [state_s[p] for p in range(pairs)]

    def finish(part):
        for ci in range(part * half_chunks, (part + 1) * half_chunks):
            for p in range(pairs):
                idx = ci * pairs + p
                sb = states[p].astype(BF16)
                y = _dot(rp_s[idx], sb) + yp_s[idx]
                y_s[ci * c:(ci + 1) * c, p * LANES:(p + 1) * LANES] = y[0:c, :] + y[c:2 * c, :]
                states[p] = states[p] * ge_s[idx] + _dot(p_s[idx], sb) + q_s[idx]
            yield
        rows = slice(part * hb, (part + 1) * hb)
        for p in range(pairs):
            sl = slice(p * LANES, (p + 1) * LANES)
            y = y_s[rows, sl]
            mean = _pair_head_sum(y, low_half) * (1.0 / HEAD_DIM)
            d = y - mean
            var = _pair_head_sum(d * d, low_half) * (1.0 / HEAD_DIM)
            yn = d * lax.rsqrt(var + RW_GN_EPS) * gng_ref[:, sl] + gnb_ref[:, sl]
            o_ref[rows, sl] = ((yn + bonus_s[rows, sl]) * gate_s[rows, sl]).astype(o_ref.dtype)
            yield

    def advance(*gens):
        def tick():
            for gen in gens:
                if gen is not None:
                    next(gen, None)
        return tick

    def drain(gen):
        for _ in gen:
            pass

    n_parts = tb // hb
    drain(prologue(0))
    for part in range(n_parts):
        ahead = prologue(part + 1) if part + 1 < n_parts else None
        behind = finish(part - 1) if part >= 1 else None
        prepare(range(part * half_chunks, (part + 1) * half_chunks), advance(ahead, behind, filler))
        for gen in (ahead, behind):
            if gen is not None:
                drain(gen)
    prev_s[...] = x_ref[tb - 1:tb, :]
    drain(finish(n_parts - 1))
    for p in range(pairs):
        state_s[p] = states[p]


RW_N_IN, RW_N_SCRATCH = 11, 17
GLA_N_IN, GLA_N_SCRATCH = 5, 5


def _recurrent_kernel(*refs):
    n_in = RW_N_IN + GLA_N_IN
    rw_in, gl_in = refs[:RW_N_IN], refs[RW_N_IN:n_in]
    o_rw, o_gl = refs[n_in], refs[n_in + 1]
    rw_scr = refs[n_in + 2:n_in + 2 + RW_N_SCRATCH]
    gl_scr = refs[n_in + 2 + RW_N_SCRATCH:]

    @pl.when(pl.program_id(1) == 0)
    def _():
        for state in (rw_scr[0], rw_scr[1], gl_scr[0]):
            state[...] = jnp.zeros_like(state)

    gla = _gla_body(*gl_in, o_gl, *gl_scr)
    _rw_body(*rw_in, o_rw, *rw_scr, filler=gla)
    for _ in gla:
        pass


def _recurrent_mixers(rw, gl, layer, rprm, gprm, batch, seq):
    n = rw.shape[0]
    tb = RW_BLOCK
    nb = seq // tb
    rows = lambda width: pl.BlockSpec((tb, width), lambda b, i: (b * nb + i, 0))
    rw_names = ("mu", "w0", "a0", "k_k", "k_a", "r_k", "gn_g", "gn_b", "up_hi", "up_lo")
    gl_names = ("gate_b", "norm_g", "up_hi", "up_lo")
    rw_params = [rprm[k] for k in rw_names]
    gl_params = [gprm[k] for k in gl_names]
    assert 1 + len(rw_params) == RW_N_IN and 1 + len(gl_params) == GLA_N_IN
    blk = lambda: pltpu.VMEM((tb, RW_WIDTH), F32)
    n_chain = (tb // CHUNK) * (RW_WIDTH // LANES)
    tile = lambda r, dt: pltpu.VMEM((n_chain, r, LANES), dt)
    rw_scratch = [pltpu.VMEM((RW_WIDTH // LANES, LANES, LANES), F32),
                  pltpu.VMEM((1, RW_IN_WIDTH), F32),
                  blk(), blk(), blk(), blk(), blk(), blk(), blk(), blk(),
                  tile(LANES, BF16), tile(LANES, F32), tile(LANES, BF16), tile(LANES, F32),
                  tile(LANES, F32), blk(), blk()]
    gl_scratch = [pltpu.VMEM((GLA_VAL_WIDTH, GLA_KEY_WIDTH), F32),
                  pltpu.VMEM((tb, GLA_KEY_WIDTH), F32),
                  pltpu.VMEM((tb, GLA_KEY_WIDTH), F32),
                  pltpu.VMEM((tb, GLA_KEY_WIDTH), F32),
                  pltpu.VMEM((tb, GLA_VAL_WIDTH), F32)]
    assert len(rw_scratch) == RW_N_SCRATCH and len(gl_scratch) == GLA_N_SCRATCH
    return pl.pallas_call(
        _recurrent_kernel,
        grid=(batch, nb),
        in_specs=([rows(RW_IN_WIDTH)] + [_layer_block(a, layer, 2) for a in rw_params]
                  + [rows(GLA_PAD_WIDTH)] + [_layer_block(a, layer, 2) for a in gl_params]),
        out_specs=[rows(RW_WIDTH), rows(GLA_VAL_WIDTH)],
        out_shape=[jax.ShapeDtypeStruct((n, RW_WIDTH), BF16),
                   jax.ShapeDtypeStruct((n, GLA_VAL_WIDTH), BF16)],
        scratch_shapes=rw_scratch + gl_scratch,
        compiler_params=_cparams(("parallel", "arbitrary")),
        name="rwkv7_gla",
    )(rw, *rw_params, gl, *gl_params)


GLA_BLOCK = RW_BLOCK
GLA_SUB = 16
GLA_EXP_CAP = 80.0


def _gla_body(x_ref, gb_ref, ng_ref, uphi_ref, uplo_ref, o_ref, state_s, q_s, k_s, la_s, o_s):
    tb = GLA_BLOCK
    c = CHUNK
    heads = GLA_VAL_WIDTH // HEAD_DIM

    kw, vw = GLA_KEY_WIDTH, GLA_VAL_WIDTH
    code = x_ref[:, 2 * kw + 2 * vw:]
    pre = _dot_x3(code, uphi_ref[...], uplo_ref[...]) + gb_ref[...]
    la_s[...] = -(jnp.maximum(-pre, 0.0) + jnp.log(1.0 + jnp.exp(-jnp.abs(pre)))) * (1.0 / GLA_GATE_NORMALIZER)
    q_s[...] = x_ref[:, 0:kw] * (GLA_DK ** -0.5)
    k_s[...] = x_ref[:, kw:2 * kw]
    yield

    tr = lax.broadcasted_iota(jnp.int32, (c, c), 0)
    tc = lax.broadcasted_iota(jnp.int32, (c, c), 1)
    tri = (tc <= tr).astype(BF16)
    sub = GLA_SUB
    n_sub = c // sub
    srow = lax.broadcasted_iota(jnp.int32, (heads * c, c), 0)
    scol = lax.broadcasted_iota(jnp.int32, (heads * c, c), 1)
    causal = scol <= (srow // (heads * sub)) * sub + srow % sub
    klane = lax.broadcasted_iota(jnp.int32, (1, kw), 1) // GLA_DK
    vlane = lax.broadcasted_iota(jnp.int32, (1, vw), 1) // HEAD_DIM
    st_r = lax.broadcasted_iota(jnp.int32, (vw, kw), 0) // HEAD_DIM
    st_c = lax.broadcasted_iota(jnp.int32, (vw, kw), 1) // GLA_DK
    st_mask = st_r == st_c

    n_chunk = tb // c
    pre = []
    for ci in range(n_chunk):
        rows = slice(ci * c, (ci + 1) * c)
        bc = _dot_exact_lhs(tri, la_s[rows, :])
        e_pos = jnp.exp(bc)
        qc = q_s[rows, :]
        kc = k_s[rows, :]
        k2 = (kc * jnp.exp(bc[c - 1:c, :] - bc)).astype(BF16)
        vb = x_ref[rows, 2 * kw:2 * kw + vw].astype(BF16)
        qsts, kes = [], []
        for i in range(n_sub):
            srows = slice(i * sub, (i + 1) * sub)
            ref = bc[i * sub - 1:i * sub, :] if i > 0 else jnp.zeros((1, kw), F32)
            qi = qc[srows, :] * jnp.exp(bc[srows, :] - ref)
            qsts.append(jnp.concatenate([jnp.where(klane == h, qi, 0.0) for h in range(heads)],
                                        axis=0).astype(BF16))
            kes.append((kc * jnp.exp(jnp.minimum(ref - bc, GLA_EXP_CAP))).astype(BF16))
        pre.append(((qc * e_pos).astype(BF16), k2, vb, qsts, kes, e_pos[c - 1:c, :]))
        yield
    scores = []
    for t in pre:
        scores.append(jnp.where(
            causal, jnp.concatenate([_dot_nt(qs, ke) for qs, ke in zip(t[3], t[4])], axis=0),
            0.0).astype(BF16))
        yield
    o_sts = [_dot(sc, t[2]) for sc, t in zip(scores, pre)]
    incs = [jnp.where(st_mask, _dot_tn(t[2], t[1]), 0.0) for t in pre]
    yield
    state = state_s[...]
    for ci in range(n_chunk):
        intra = []
        for i in range(n_sub):
            acc = None
            for h in range(heads):
                r0 = (i * heads + h) * sub
                part = jnp.where(vlane == h, o_sts[ci][r0:r0 + sub, :], 0.0)
                acc = part if acc is None else acc + part
            intra.append(acc)
        o_s[ci * c:(ci + 1) * c, :] = _dot_nt(pre[ci][0], state.astype(BF16)) + jnp.concatenate(intra, axis=0)
        state = state * pre[ci][5] + incs[ci]
        yield
    state_s[...] = state

    lane = lax.broadcasted_iota(jnp.int32, (1, LANES), 1)
    low_half = lane < HEAD_DIM
    outs = []
    for p in range(vw // LANES):
        o = o_s[:, p * LANES:(p + 1) * LANES]
        ms = _pair_head_sum(o * o, low_half) * (1.0 / HEAD_DIM)
        outs.append(o * lax.rsqrt(ms + EPS))
    g = x_ref[:, 2 * kw + vw:2 * kw + 2 * vw]
    on = jnp.concatenate(outs, axis=1) * ng_ref[...]
    o_ref[...] = (on * (g * jax.nn.sigmoid(g))).astype(o_ref.dtype)


MLP_ROWS = 1024
MLP_FF_TILE = 1024


def _mlp_kernel(x_ref, sb_ref, rw_ref, gl_ref, wo_ref, gmix_ref,
                gpre_ref, w1_ref, w2_ref, gpost_ref, o_ref):
    mix_in = jnp.concatenate([sb_ref[...], rw_ref[...], gl_ref[...]], axis=1)
    mixed = _dot(mix_in, wo_ref[...])
    x1 = x_ref[...] + _rms(mixed, gmix_ref[...])
    h = _rms(x1, gpre_ref[...]).astype(BF16)
    ff = None
    for j in range(D_FF // MLP_FF_TILE):
        cols = slice(j * MLP_FF_TILE, (j + 1) * MLP_FF_TILE)
        a = jnp.maximum(_dot(h, w1_ref[:, cols]), 0.0)
        part = _dot((a * a).astype(BF16), w2_ref[cols, :])
        ff = part if ff is None else ff + part
    o_ref[...] = x1 + _rms(ff, gpost_ref[...])


def _mlp(x2, o_sb, o_rw, o_gl, layer, w_out, g_mix, g_pre, w1, w2, g_post):
    n = x2.shape[0]
    tm = MLP_ROWS
    full = lambda w: _layer_block(w, layer, 1, single_buffer=True)
    row = lambda width: pl.BlockSpec((tm, width), lambda i: (i, 0))

    return pl.pallas_call(
        _mlp_kernel,
        grid=(n // tm,),
        in_specs=[row(D_MODEL), row(SB_WIDTH), row(RW_WIDTH), row(GLA_VAL_WIDTH),
                  full(w_out), full(g_mix), full(g_pre), full(w1), full(w2), full(g_post)],
        out_specs=row(D_MODEL),
        out_shape=jax.ShapeDtypeStruct((n, D_MODEL), F32),
        compiler_params=_cparams(("parallel",)),
        name="out_proj_mlp",
    )(x2, o_sb, o_rw, o_gl, w_out, g_mix, g_pre, w1, w2, g_post)


def _hi_lo(w):
    hi = w.astype(BF16)
    return hi, (w - hi.astype(F32)).astype(BF16)


def _prep_weights(w_in, rw_w_up, rw_a_up, rw_g_up, gla_gate_up, w_out, w_ff1, w_ff2):
    depth = w_in.shape[0]
    sb_end = 3 * SB_WIDTH
    rw_end = sb_end + RW_IN_WIDTH
    w_sb = w_in[:, :, :sb_end].astype(BF16)
    w_rw = w_in[:, :, sb_end:rw_end].astype(BF16)
    g0 = rw_end
    kw, vw, gr = GLA_KEY_WIDTH, GLA_VAL_WIDTH, GLA_GATE_RANK
    w_gl = jnp.concatenate([
        w_in[:, :, g0:g0 + 2 * kw + vw],
        w_in[:, :, g0 + 2 * kw + vw + gr:g0 + GLA_IN_WIDTH],
        w_in[:, :, g0 + 2 * kw + vw:g0 + 2 * kw + vw + gr],
        jnp.zeros((depth, D_MODEL, GLA_PAD_WIDTH - GLA_IN_WIDTH), F32)], axis=2).astype(BF16)
    up = jnp.zeros((depth, RW_CODE, 3 * RW_WIDTH), F32)
    up = up.at[:, 0:32, 0:RW_WIDTH].set(rw_w_up)
    up = up.at[:, 32:64, RW_WIDTH:2 * RW_WIDTH].set(rw_a_up)
    up = up.at[:, 64:128, 2 * RW_WIDTH:].set(rw_g_up)
    rw_up_hi, rw_up_lo = _hi_lo(up)
    gup = jnp.zeros((depth, LANES, GLA_KEY_WIDTH), F32).at[:, 0:gr, :].set(gla_gate_up)
    gl_up_hi, gl_up_lo = _hi_lo(gup)
    return dict(w_sb=w_sb, w_rw=w_rw, w_gl=w_gl, rw_up_hi=rw_up_hi, rw_up_lo=rw_up_lo,
                gl_up_hi=gl_up_hi, gl_up_lo=gl_up_lo, wo=w_out.astype(BF16),
                w1=w_ff1.astype(BF16), w2=w_ff2.astype(BF16))


def kernel(x, pre_mix_g, w_in, sb_norm_g, rw_mu, rw_w0, rw_w_up, rw_a0, rw_a_up, rw_g_up, rw_k_k, rw_k_a, rw_r_k, rw_gn_g, rw_gn_b, gla_gate_up, gla_gate_b, gla_norm_g, w_out, post_mix_g, pre_ffn_g, w_ff1, w_ff2, post_ffn_g):
    batch, seq, d = x.shape
    depth = w_in.shape[0]
    x2 = x.reshape(batch * seq, d)
    vec = lambda t: t[:, None, :]
    wp = _prep_weights(w_in, rw_w_up, rw_a_up, rw_g_up, gla_gate_up, w_out, w_ff1, w_ff2)
    rw_prm = dict(mu=vec(rw_mu), w0=vec(rw_w0), a0=vec(rw_a0), k_k=vec(rw_k_k), k_a=vec(rw_k_a),
                  r_k=vec(rw_r_k), gn_g=vec(rw_gn_g), gn_b=vec(rw_gn_b),
                  up_hi=wp["rw_up_hi"], up_lo=wp["rw_up_lo"])
    gl_prm = dict(gate_b=vec(gla_gate_b), norm_g=vec(gla_norm_g),
                  up_hi=wp["gl_up_hi"], up_lo=wp["gl_up_lo"])
    for l in range(depth):
        sb, rw, gl = _inproj(x2, l, vec(pre_mix_g), wp["w_sb"], wp["w_rw"], wp["w_gl"])
        o_sb = _sb_attention(sb, l, vec(sb_norm_g), batch, seq)
        o_rw, o_gl = _recurrent_mixers(rw, gl, l, rw_prm, gl_prm, batch, seq)
        x2 = _mlp(x2, o_sb, o_rw, o_gl, l, wp["wo"], vec(post_mix_g), vec(pre_ffn_g),
                  wp["w1"], wp["w2"], vec(post_ffn_g))
    return x2.reshape(batch, seq, d)
```
